```python
import math
import numpy as np
import jax
import jax.numpy as jnp
from jax import lax

D_MODEL = 1024
BATCH = 16
SEQ = 256
DEPTH = 2
DEC_BATCH = 4
DEC_SEQ = 1024
PAST_LEN = 256

GRID_W = 64
D_MIX = D_MODEL // 2
D_FF = ((8 * D_MODEL // 3) + 127) // 128 * 128
N_MOD = 9
SSD_P = 64
SSD_H = D_MIX // SSD_P
SSD_N = 64
SSD_G = 2
SSD_CONV = 3
SSD_CHUNK = 64
SSD_XBC = D_MIX + 2 * SSD_G * SSD_N
SSD_COLS = D_MIX + SSD_XBC + 2 * SSD_H
GLA_H = 4
GLA_DK = D_MIX // 2 // GLA_H
GLA_DV = D_MIX // GLA_H
GLA_LR = 16
GLA_GATE_NORM = 16.0
GLA_CHUNK = 64
GLA_COLS = 2 * GLA_H * GLA_DK + 2 * D_MIX + 2 * GLA_LR
RW_N = 64
RW_H = D_MIX // RW_N
RW_LW = 64
RW_LA = 64
RW_LG = 128
RW_COLS = 3 * D_MIX + 2 * RW_LW + RW_LA + RW_LG
GATE_COLS = 3 * D_MODEL
N_IN = SSD_COLS + GLA_COLS + RW_COLS + GATE_COLS
RMS_EPS = 1e-6
RW_GN_EPS = 64e-5

kernel_name = 'hybrid_ssd_gla_rwkv7_diffusion_step'


def _split(x, sizes):
    return jnp.split(x, np.cumsum(sizes)[:-1].tolist(), axis=-1)


def _rev(t):
    return jnp.flip(t, axis=1)


def rmsnorm(x, g):
    xf = x.astype(jnp.float32)
    y = xf * lax.rsqrt(jnp.mean(xf * xf, axis=-1, keepdims=True) + RMS_EPS)
    return (y * g.astype(jnp.float32)).astype(x.dtype)


def modulate(h, shift, scale):
    return h * (1 + scale) + shift


def swiglu(h, wg, wu, wd):
    return (jax.nn.silu(h @ wg) * (h @ wu)) @ wd


def grid_pos_embed(rows, cols, dim):
    quarter = dim // 4
    omega = 1.0 / (10000.0 ** (jnp.arange(quarter, dtype=jnp.float32) / quarter))
    er = jnp.arange(rows, dtype=jnp.float32)[:, None] * omega
    ec = jnp.arange(cols, dtype=jnp.float32)[:, None] * omega
    er = jnp.concatenate([jnp.sin(er), jnp.cos(er)], axis=-1)
    ec = jnp.concatenate([jnp.sin(ec), jnp.cos(ec)], axis=-1)
    emb = jnp.concatenate([jnp.broadcast_to(er[:, None], (rows, cols, dim // 2)),
                           jnp.broadcast_to(ec[None], (rows, cols, dim // 2))], axis=-1)
    return emb.reshape(rows * cols, dim)


def segsum(x):
    T = x.shape[-1]
    xr = jnp.broadcast_to(x[..., None], x.shape + (T,))
    xr = jnp.where(jnp.tril(jnp.ones((T, T), bool), -1), xr, 0.0)
    ss = jnp.cumsum(xr, axis=-2)
    return jnp.where(jnp.tril(jnp.ones((T, T), bool)), ss, -jnp.inf)


def centred_conv(x, w, b):
    K = w.shape[0]
    half = K // 2
    L = x.shape[1]
    xp = jnp.pad(x, ((0, 0), (half, half), (0, 0)))
    out = b + xp[:, 0:L] * w[0]
    for i in range(1, K):
        out = out + xp[:, i:i + L] * w[i]
    return out


def centred_shift_mix(p, mu):
    zero = jnp.zeros_like(p[:, :1])
    prev = jnp.concatenate([zero, p[:, :-1]], axis=1)
    nxt = jnp.concatenate([p[:, 1:], zero], axis=1)
    return p + (0.5 * (prev + nxt) - p) * mu


def ssd_chunked(X, Adt, Bm, Cm, S0):
    b, l, H, P = X.shape
    dt = X.dtype
    G = Bm.shape[2]
    J = H // G
    N = Bm.shape[3]
    nc = l // SSD_CHUNK
    Xc = X.reshape(b, nc, SSD_CHUNK, G, J, P)
    Bc = Bm.reshape(b, nc, SSD_CHUNK, G, N)
    Cc = Cm.reshape(b, nc, SSD_CHUNK, G, N)
    A = Adt.astype(jnp.float32).reshape(b, nc, SSD_CHUNK, G, J).transpose(0, 3, 4, 1, 2)
    A_cs = jnp.cumsum(A, axis=-1)
    Lmat = jnp.exp(segsum(A))
    Y_diag = jnp.einsum('bclgn,bcsgn,bgjcls,bcsgjp->bclgjp', Cc, Bc, Lmat, Xc)
    decay_states = jnp.exp(A_cs[..., -1:] - A_cs)
    states = jnp.einsum('bcsgn,bgjcs,bcsgjp->bcgjpn', Bc, decay_states, Xc)
    S0r = S0.reshape(b, G, J, P, N).astype(states.dtype)
    states = jnp.concatenate([S0r[:, None], states], axis=1)
    chunk_end = jnp.pad(A_cs[..., -1], ((0, 0), (0, 0), (0, 0), (1, 0)))
    decay_chunk = jnp.exp(segsum(chunk_end))
    new_states = jnp.einsum('bgjzc,bcgjpn->bzgjpn', decay_chunk, states)
    S_in, S_fin = new_states[:, :-1], new_states[:, -1]
    Y_off = jnp.einsum('bclgn,bcgjpn,bgjcl->bclgjp', Cc, S_in, jnp.exp(A_cs))
    Y = (Y_diag + Y_off).reshape(b, l, H, P)
    return Y.astype(dt), S_fin.reshape(b, H, P, N).astype(dt)


def gla_chunked(q, k, v, lg, S0):
    b, l, h, dk = q.shape
    dv = v.shape[-1]
    dt = q.dtype
    C = GLA_CHUNK
    nc = l // C
    f32 = jnp.float32
    qc = q.astype(f32).reshape(b, nc, C, h, dk)
    kc = k.astype(f32).reshape(b, nc, C, h, dk)
    vc = v.astype(f32).reshape(b, nc, C, h, dv)
    bc = jnp.cumsum(lg.astype(f32).reshape(b, nc, C, h, dk), axis=2)
    causal = jnp.tril(jnp.ones((C, C), bool))[None, None, :, :, None, None]
    diff = bc[:, :, :, None] - bc[:, :, None, :]
    decay = jnp.exp(jnp.where(causal, diff, -jnp.inf))
    A = jnp.einsum('bnthk,bnshk,bntshk->bnhts', qc, kc, decay)
    o_intra = jnp.einsum('bnhts,bnshv->bnthv', A, vc)
    last = bc[:, :, -1]
    q_dec = qc * jnp.exp(bc)
    k_dec = kc * jnp.exp(last[:, :, None] - bc)
    kv = jnp.einsum('bnshk,bnshv->bnhkv', k_dec, vc)

    def step(S, inp):
        dec, kv_n = inp
        return dec[..., None] * S + kv_n, S

    S_fin, S_in = lax.scan(step, S0.astype(f32), (jnp.moveaxis(jnp.exp(last), 1, 0), jnp.moveaxis(kv, 1, 0)))
    S_in = jnp.moveaxis(S_in, 0, 1)
    o_inter = jnp.einsum('bnthk,bnhkv->bnthv', q_dec, S_in)
    o = (o_intra + o_inter).reshape(b, l, h, dv)
    return o.astype(dt), S_fin.astype(dt)


def rwkv_scan(r, w, k, v, kk, a, S0, reverse):
    def step(S, inp):
        r_t, w_t, k_t, v_t, kk_t, a_t = inp
        Skk = jnp.einsum('bhvk,bhk->bhv', S, kk_t)
        S = S * w_t[:, :, None, :] - Skk[..., None] * (kk_t * a_t)[:, :, None, :] + v_t[..., None] * k_t[:, :, None, :]
        return S, jnp.einsum('bhvk,bhk->bhv', S, r_t)

    xs = tuple(jnp.moveaxis(t, 1, 0) for t in (r, w, k, v, kk, a))
    S_fin, o = lax.scan(step, S0.astype(jnp.float32), xs, reverse=reverse)
    return jnp.moveaxis(o, 0, 1), S_fin


def ssd_branch(p, st, lp):
    b, L, _ = p.shape
    z, xbc, dt_raw = _split(p, [D_MIX, SSD_XBC, 2 * SSD_H])
    xbc = jax.nn.silu(centred_conv(xbc, lp['ssd_conv_w'], lp['ssd_conv_b']))
    xs, bm, cm = _split(xbc, [D_MIX, SSD_G * SSD_N, SSD_G * SSD_N])
    xs = xs.reshape(b, L, SSD_H, SSD_P)
    bm = bm.reshape(b, L, SSD_G, SSD_N)
    cm = cm.reshape(b, L, SSD_G, SSD_N)
    dt = jax.nn.softplus((dt_raw.reshape(b, L, 2, SSD_H) + lp['ssd_dt_bias']).astype(jnp.float32))
    a = -jnp.exp(lp['ssd_A_log'].astype(jnp.float32))
    y = xs * (lp['ssd_D'][0] + lp['ssd_D'][1])[:, None]
    finals = []
    for d in range(2):
        xd = (xs * dt[:, :, d, :, None]).astype(xs.dtype)
        ad = dt[:, :, d] * a[d]
        if d == 0:
            yd, sd = ssd_chunked(xd, ad, bm, cm, st[:, 0])
        else:
            yd, sd = ssd_chunked(_rev(xd), _rev(ad), _rev(bm), _rev(cm), st[:, 1])
            yd = _rev(yd)
        y = y + yd
        finals.append(sd)
    y = rmsnorm(y.reshape(b, L, D_MIX) * jax.nn.silu(z), lp['ssd_norm'])
    return y @ lp['w_ssd_o'], jnp.stack(finals, axis=1)


def gla_branch(p, st, lp):
    b, L, _ = p.shape
    q, k, v, g, lr = _split(p, [GLA_H * GLA_DK, GLA_H * GLA_DK, D_MIX, D_MIX, 2 * GLA_LR])
    q = q.reshape(b, L, GLA_H, GLA_DK) * (GLA_DK ** -0.5)
    k = k.reshape(b, L, GLA_H, GLA_DK)
    v = v.reshape(b, L, GLA_H, GLA_DV)
    lr = lr.reshape(b, L, 2, GLA_LR)
    outs = []
    finals = []
    for d in range(2):
        lg = jax.nn.log_sigmoid((lr[:, :, d] @ lp['gla_gk_w'][d] + lp['gla_gk_b'][d]).astype(jnp.float32)) / GLA_GATE_NORM
        lg = lg.reshape(b, L, GLA_H, GLA_DK)
        if d == 0:
            od, sd = gla_chunked(q, k, v, lg, st[:, 0])
        else:
            od, sd = gla_chunked(_rev(q), _rev(k), _rev(v), _rev(lg), st[:, 1])
            od = _rev(od)
        outs.append(od)
        finals.append(sd)
    o = rmsnorm(outs[0] + outs[1], lp['gla_norm']).reshape(b, L, D_MIX) * jax.nn.silu(g)
    return o @ lp['w_gla_o'], jnp.stack(finals, axis=1)


def rwkv_branch(p, st, lp):
    b, L, _ = p.shape
    f32 = jnp.float32
    p = centred_shift_mix(p, lp['rw_mu'])
    r, k, v, wlr, alr, glr = _split(p, [D_MIX, D_MIX, D_MIX, 2 * RW_LW, RW_LA, RW_LG])
    a = jax.nn.sigmoid((lp['rw_a0'] + alr @ lp['rw_a2']).astype(f32))
    g = jax.nn.sigmoid(glr) @ lp['rw_g2']

    def hd(t):
        return t.reshape(b, L, RW_H, RW_N)

    r, k, v, a = hd(r.astype(f32)), hd(k.astype(f32)), hd(v.astype(f32)), hd(a)
    kk = k * lp['rw_kk'].astype(f32).reshape(RW_H, RW_N)
    kk = kk / jnp.maximum(jnp.sqrt(jnp.sum(kk * kk, axis=-1, keepdims=True)), 1e-12)
    k = k * (1 + (a - 1) * lp['rw_ka'].astype(f32).reshape(RW_H, RW_N))
    wlr = wlr.reshape(b, L, 2, RW_LW)
    o = jnp.zeros_like(r)
    finals = []
    for d in range(2):
        wl = -jax.nn.softplus(-(lp['rw_w0'][d] + jnp.tanh(wlr[:, :, d]) @ lp['rw_w2'][d]).astype(f32)) - 0.5
        w = hd(jnp.exp(-jnp.exp(wl)))
        od, sd = rwkv_scan(r, w, k, v, kk, a, st[:, d], reverse=(d == 1))
        o = o + od
        finals.append(sd.astype(p.dtype))
    mu = jnp.mean(o, axis=-1, keepdims=True)
    var = jnp.mean(jnp.square(o - mu), axis=-1, keepdims=True)
    o = ((o - mu) * lax.rsqrt(var + RW_GN_EPS)).reshape(b, L, D_MIX)
    o = o * lp['rw_ln_w'].astype(f32) + lp['rw_ln_b'].astype(f32)
    o = o + (jnp.sum(r * k * lp['rw_rk'].astype(f32), axis=-1, keepdims=True) * v).reshape(b, L, D_MIX)
    o = o.astype(p.dtype) * g
    return o @ lp['w_rw_o'], jnp.stack(finals, axis=1)


def token_mixer(u, st_ssd, st_gla, st_rw, lp):
    b, L, _ = u.shape
    proj = u @ lp['w_in']
    p_ssd, p_gla, p_rw, p_gate = _split(proj, [SSD_COLS, GLA_COLS, RW_COLS, GATE_COLS])
    o_ssd, s_ssd = ssd_branch(p_ssd, st_ssd, lp)
    o_gla, s_gla = gla_branch(p_gla, st_gla, lp)
    o_rw, s_rw = rwkv_branch(p_rw, st_rw, lp)
    gates = jax.nn.sigmoid(p_gate.astype(jnp.float32)).astype(u.dtype).reshape(b, L, 3, D_MODEL)
    merged = gates[:, :, 0] * o_ssd + gates[:, :, 1] * o_gla + gates[:, :, 2] * o_rw
    return merged @ lp['w_out'], s_ssd, s_gla, s_rw


def trunk_layer(x, cond, st_ssd, st_gla, st_rw, lp):
    nb = cond.shape[0]
    ada = (jax.nn.silu(cond) @ lp['w_ada'] + lp['b_ada']).reshape(nb, 1, N_MOD, D_MODEL)
    sh1, sc1, g1, sh2, sc2, g2, sh3, sc3, g3 = [ada[:, :, i] for i in range(N_MOD)]
    h = modulate(rmsnorm(x, lp['norm_g'][0]), sh1, sc1)
    x = x + 0.5 * g1 * swiglu(h, lp['ffn_gate'][0], lp['ffn_up'][0], lp['ffn_down'][0])
    h = modulate(rmsnorm(x, lp['norm_g'][1]), sh2, sc2)
    m, s_ssd, s_gla, s_rw = token_mixer(h, st_ssd, st_gla, st_rw, lp)
    x = x + g2 * m
    h = modulate(rmsnorm(x, lp['norm_g'][2]), sh3, sc3)
    x = x + 0.5 * g3 * swiglu(h, lp['ffn_gate'][1], lp['ffn_up'][1], lp['ffn_down'][1])
    return x, s_ssd, s_gla, s_rw


def setup_inputs(seed: int = 0) -> dict:
    key = jax.random.key(seed)
    ks = iter(jax.random.split(key, 64))
    f32 = jnp.float32

    def nrm(shape, scale):
        return jax.random.normal(next(ks), shape, f32) * scale

    def uni(shape, lo, hi):
        return jax.random.uniform(next(ks), shape, f32, lo, hi)

    x_prompt = nrm((BATCH, SEQ, D_MODEL), 1.0)
    x_sample = nrm((DEC_BATCH, DEC_SEQ, D_MODEL), 1.0)
    state_ssd = nrm((DEC_BATCH, DEPTH, 2, SSD_H, SSD_P, SSD_N), 0.5)
    state_gla = nrm((DEC_BATCH, DEPTH, 2, GLA_H, GLA_DK, GLA_DV), 1.0)
    state_rwkv = nrm((DEC_BATCH, DEPTH, 2, RW_H, RW_N, RW_N), 0.5)
    c = nrm((DEC_BATCH, D_MODEL), 1.0)
    c_ctx = nrm((D_MODEL,), 1.0)
    norm_g = 1.0 + nrm((DEPTH, 3, D_MODEL), 0.02)
    w_ada = nrm((DEPTH, D_MODEL, N_MOD * D_MODEL), 0.5 * D_MODEL ** -0.5)
    b_ada = nrm((DEPTH, N_MOD * D_MODEL), 0.02)
    ffn_gate = nrm((DEPTH, 2, D_MODEL, D_FF), D_MODEL ** -0.5)
    ffn_up = nrm((DEPTH, 2, D_MODEL, D_FF), D_MODEL ** -0.5)
    ffn_down = nrm((DEPTH, 2, D_FF, D_MODEL), D_FF ** -0.5)
    w_in = nrm((DEPTH, D_MODEL, N_IN), D_MODEL ** -0.5)
    ssd_conv_w = nrm((DEPTH, SSD_CONV, SSD_XBC), SSD_CONV ** -0.5)
    ssd_conv_b = nrm((DEPTH, SSD_XBC), 0.02)
    dt0 = jnp.exp(uni((DEPTH, 2, SSD_H), math.log(1e-3), math.log(1e-1)))
    ssd_dt_bias = dt0 + jnp.log(-jnp.expm1(-dt0))
    ssd_A_log = jnp.log(uni((DEPTH, 2, SSD_H), 1.0, 16.0))
    ssd_D = 1.0 + nrm((DEPTH, 2, SSD_H), 0.1)
    ssd_norm = 1.0 + nrm((DEPTH, D_MIX), 0.02)
    w_ssd_o = nrm((DEPTH, D_MIX, D_MODEL), D_MIX ** -0.5)
    gla_gk_w = nrm((DEPTH, 2, GLA_LR, GLA_H * GLA_DK), GLA_LR ** -0.5)
    gla_gk_b = nrm((DEPTH, 2, GLA_H * GLA_DK), 0.5)
    gla_norm = 1.0 + nrm((DEPTH, GLA_DV), 0.02)
    w_gla_o = nrm((DEPTH, D_MIX, D_MODEL), D_MIX ** -0.5)
    rw_mu = uni((DEPTH, RW_COLS), 0.0, 1.0)
    rw_w0 = uni((DEPTH, 2, D_MIX), -6.0, 1.0)
    rw_w2 = nrm((DEPTH, 2, RW_LW, D_MIX), 0.5 * RW_LW ** -0.5)
    rw_a0 = nrm((DEPTH, D_MIX), 0.5)
    rw_a2 = nrm((DEPTH, RW_LA, D_MIX), 0.5 * RW_LA ** -0.5)
    rw_g2 = nrm((DEPTH, RW_LG, D_MIX), RW_LG ** -0.5)
    rw_kk = 0.85 + nrm((DEPTH, D_MIX), 0.02)
    rw_ka = 1.0 + nrm((DEPTH, D_MIX), 0.02)
    rw_rk = nrm((DEPTH, RW_H, RW_N), 0.1)
    rw_ln_w = 1.0 + nrm((DEPTH, D_MIX), 0.02)
    rw_ln_b = nrm((DEPTH, D_MIX), 0.02)
    w_rw_o = nrm((DEPTH, D_MIX, D_MODEL), D_MIX ** -0.5)
    w_out = nrm((DEPTH, D_MODEL, D_MODEL), D_MODEL ** -0.5)
    final_norm = 1.0 + nrm((D_MODEL,), 0.02)
    return {'x_prompt': x_prompt, 'x_sample': x_sample, 'state_ssd': state_ssd, 'state_gla': state_gla,
            'state_rwkv': state_rwkv, 'c': c, 'c_ctx': c_ctx, 'norm_g': norm_g, 'w_ada': w_ada, 'b_ada': b_ada,
            'ffn_gate': ffn_gate, 'ffn_up': ffn_up, 'ffn_down': ffn_down, 'w_in': w_in,
            'ssd_conv_w': ssd_conv_w, 'ssd_conv_b': ssd_conv_b, 'ssd_dt_bias': ssd_dt_bias, 'ssd_A_log': ssd_A_log,
            'ssd_D': ssd_D, 'ssd_norm': ssd_norm, 'w_ssd_o': w_ssd_o, 'gla_gk_w': gla_gk_w, 'gla_gk_b': gla_gk_b,
            'gla_norm': gla_norm, 'w_gla_o': w_gla_o, 'rw_mu': rw_mu, 'rw_w0': rw_w0, 'rw_w2': rw_w2,
            'rw_a0': rw_a0, 'rw_a2': rw_a2, 'rw_g2': rw_g2, 'rw_kk': rw_kk, 'rw_ka': rw_ka, 'rw_rk': rw_rk,
            'rw_ln_w': rw_ln_w, 'rw_ln_b': rw_ln_b, 'w_rw_o': w_rw_o, 'w_out': w_out, 'final_norm': final_norm}


def reference(x_prompt, x_sample, state_ssd, state_gla, state_rwkv, c, c_ctx, norm_g, w_ada, b_ada,
              ffn_gate, ffn_up, ffn_down, w_in, ssd_conv_w, ssd_conv_b, ssd_dt_bias, ssd_A_log, ssd_D,
              ssd_norm, w_ssd_o, gla_gk_w, gla_gk_b, gla_norm, w_gla_o, rw_mu, rw_w0, rw_w2, rw_a0, rw_a2,
              rw_g2, rw_kk, rw_ka, rw_rk, rw_ln_w, rw_ln_b, w_rw_o, w_out, final_norm):
    rows = x_sample.shape[1] // GRID_W
    xs = x_sample + grid_pos_embed(rows, GRID_W, D_MODEL).astype(x_sample.dtype)[None]
    xp = x_prompt
    bp = xp.shape[0]
    cond_ctx = c_ctx[None]
    new_ssd, new_gla, new_rw = [], [], []
    for l in range(DEPTH):
        lp = {'norm_g': norm_g[l], 'w_ada': w_ada[l], 'b_ada': b_ada[l], 'ffn_gate': ffn_gate[l],
              'ffn_up': ffn_up[l], 'ffn_down': ffn_down[l], 'w_in': w_in[l], 'ssd_conv_w': ssd_conv_w[l],
              'ssd_conv_b': ssd_conv_b[l], 'ssd_dt_bias': ssd_dt_bias[l], 'ssd_A_log': ssd_A_log[l],
              'ssd_D': ssd_D[l], 'ssd_norm': ssd_norm[l], 'w_ssd_o': w_ssd_o[l], 'gla_gk_w': gla_gk_w[l],
              'gla_gk_b': gla_gk_b[l], 'gla_norm': gla_norm[l], 'w_gla_o': w_gla_o[l], 'rw_mu': rw_mu[l],
              'rw_w0': rw_w0[l], 'rw_w2': rw_w2[l], 'rw_a0': rw_a0[l], 'rw_a2': rw_a2[l], 'rw_g2': rw_g2[l],
              'rw_kk': rw_kk[l], 'rw_ka': rw_ka[l], 'rw_rk': rw_rk[l], 'rw_ln_w': rw_ln_w[l],
              'rw_ln_b': rw_ln_b[l], 'w_rw_o': w_rw_o[l], 'w_out': w_out[l]}
        z_ssd = jnp.zeros((bp, 2, SSD_H, SSD_P, SSD_N), xp.dtype)
        z_gla = jnp.zeros((bp, 2, GLA_H, GLA_DK, GLA_DV), xp.dtype)
        z_rw = jnp.zeros((bp, 2, RW_H, RW_N, RW_N), xp.dtype)
        xp, s_ssd, s_gla, s_rw = trunk_layer(xp, cond_ctx, z_ssd, z_gla, z_rw, lp)
        new_ssd.append(s_ssd)
        new_gla.append(s_gla)
        new_rw.append(s_rw)
        xs, _, _, _ = trunk_layer(xs, c, state_ssd[:, l], state_gla[:, l], state_rwkv[:, l], lp)
    y_prompt = rmsnorm(xp, final_norm)
    y_sample = rmsnorm(xs, final_norm)
    return (y_prompt, y_sample, jnp.stack(new_ssd, axis=1), jnp.stack(new_gla, axis=1), jnp.stack(new_rw, axis=1))
```

```python
import functools
import math

import numpy as np
import jax
import jax.numpy as jnp
from jax import lax
from jax.experimental import pallas as pl
from jax.experimental.pallas import tpu as pltpu

F32 = jnp.float32
BF16 = jnp.bfloat16

D_MODEL = 1024
BATCH = 16
SEQ = 256
DEPTH = 2
DEC_BATCH = 4
DEC_SEQ = 1024
GRID_W = 64
D_MIX = 512
D_FF = 2816
N_MOD = 9
SSD_P = 64
SSD_H = 8
SSD_N = 64
SSD_G = 2
SSD_XBC = 768
GLA_H = 4
GLA_DK = 64
GLA_DV = 128
GLA_LR = 16
GLA_GATE_NORM = 16.0
GLA_CHUNK = 64
RW_N = 64
RW_H = 8
RW_LW = 64
RW_LA = 64
RW_LG = 128
RMS_EPS = 1e-6
RW_GN_EPS = 64e-5

LANE = 128
SUB = 8
RB = 256
N_CTX_TOK = BATCH * SEQ
N_TOK = N_CTX_TOK + DEC_BATCH * DEC_SEQ
N_BLK = N_TOK // RB
N_CTX_BLK = N_CTX_TOK // RB
LAT_BLK_PER_SEQ = DEC_SEQ // RB
SSD_W = D_MIX + SSD_XBC + LANE
GLA_QK = GLA_H * LANE
GLA_W = 2 * GLA_QK + 2 * D_MIX + LANE
RW_W = 3 * D_MIX + 3 * LANE
SCAN_TB = 16
VMEM_LIMIT = 56 * 1024 * 1024


def _dot(a, b):
    return jnp.dot(a.astype(BF16), b.astype(BF16), preferred_element_type=F32)


def _dot_nt(a, b):
    return lax.dot_general(a.astype(BF16), b.astype(BF16), (((1,), (1,)), ((), ())),
                           preferred_element_type=F32)


def _split3(x):
    hi = x.astype(BF16)
    r1 = x - hi.astype(F32)
    mid = r1.astype(BF16)
    lo = (r1 - mid.astype(F32)).astype(BF16)
    return hi, mid, lo


def _sel_dot(m01, x):
    hi, mid, lo = _split3(x)
    f = lambda p: jnp.dot(m01, p, preferred_element_type=F32)
    return f(hi) + f(mid) + f(lo)


def _dot_sel(x, m01):
    hi, mid, lo = _split3(x)
    f = lambda p: jnp.dot(p, m01, preferred_element_type=F32)
    return f(hi) + f(mid) + f(lo)


def _sigmoid(x):
    return jax.nn.sigmoid(x)


def _silu(x):
    return x * jax.nn.sigmoid(x)


def _softplus(x):
    return jnp.maximum(x, 0.0) + jnp.log1p(jnp.exp(-jnp.abs(x)))


def _rmsnorm(x, g):
    return x * lax.rsqrt(jnp.mean(x * x, axis=-1, keepdims=True) + RMS_EPS) * g


def _norm_mod(x, g, shift, scale):
    return _rmsnorm(x, g) * (1.0 + scale) + shift


def _iota(shape, dim):
    return lax.broadcasted_iota(jnp.int32, shape, dim)


def _tri01(n, lower, chunk=None):
    t = _iota((n, n), 0)
    s = _iota((n, n), 1)
    m = (s <= t) if lower else (s >= t)
    if chunk is not None:
        m = m & ((t // chunk) == (s // chunk))
    return m


def _ada_kernel(c_ref, w_ref, b_ref, o_ref):
    o_ref[0] = _dot(_silu(c_ref[...]), w_ref[0]) + b_ref[0]


def _ada_call(cond8, w_ada, b_ada):
    tn = 1024
    nj = (N_MOD * D_MODEL) // tn
    return pl.pallas_call(
        _ada_kernel,
        grid=(DEPTH, nj),
        in_specs=[pl.BlockSpec((SUB, D_MODEL), lambda l, j: (0, 0)),
                  pl.BlockSpec((1, D_MODEL, tn), lambda l, j: (l, 0, j)),
                  pl.BlockSpec((1, 1, tn), lambda l, j: (l, 0, j))],
        out_specs=pl.BlockSpec((1, SUB, tn), lambda l, j: (l, 0, j)),
        out_shape=jax.ShapeDtypeStruct((DEPTH, SUB, N_MOD * D_MODEL), F32),
        compiler_params=pltpu.CompilerParams(dimension_semantics=("arbitrary", "arbitrary"),
                                             vmem_limit_bytes=VMEM_LIMIT),
        name="ada",
    )(cond8, w_ada, b_ada.reshape(DEPTH, 1, N_MOD * D_MODEL))


def _ffn_kernel(*refs, mod_row, with_pos, with_final):
    it = iter(refs)
    x_ref = next(it)
    pos_ref = next(it) if with_pos else None
    mod_ref, ng_ref, wg_ref, wu_ref, wd_ref = next(it), next(it), next(it), next(it), next(it)
    fin_ref = next(it) if with_final else None
    o_ref = next(it)
    x = x_ref[...]
    if with_pos:
        x = x + pos_ref[...]
    shift = mod_ref[0, mod_row:mod_row + 1, :]
    scale = mod_ref[0, mod_row + 1:mod_row + 2, :]
    gate = mod_ref[0, mod_row + 2:mod_row + 3, :]
    h = _norm_mod(x, ng_ref[...], shift, scale).astype(BF16)
    a = _silu(jnp.dot(h, wg_ref[...], preferred_element_type=F32))
    a = (a * jnp.dot(h, wu_ref[...], preferred_element_type=F32)).astype(BF16)
    y = x + 0.5 * gate * jnp.dot(a, wd_ref[...], preferred_element_type=F32)
    if with_final:
        y = _rmsnorm(y, fin_ref[...])
    o_ref[...] = y


def _const_spec(shape):
    nd = len(shape)
    return pl.BlockSpec(shape, lambda i: (0,) * nd, pipeline_mode=pl.Buffered(1))


def _ffn_call(x, mod, ng, wg, wu, wd, *, mod_row, pos=None, final_g=None):
    with_pos = pos is not None
    with_final = final_g is not None
    row_spec = pl.BlockSpec((RB, D_MODEL), lambda i: (i, 0))
    in_specs = [row_spec]
    args = [x]
    if with_pos:
        in_specs.append(pl.BlockSpec(
            (RB, D_MODEL),
            lambda i: (jnp.where(i < N_CTX_BLK, 0, 1 + (i - N_CTX_BLK) % LAT_BLK_PER_SEQ), 0)))
        args.append(pos)
    in_specs += [pl.BlockSpec((1, N_MOD, D_MODEL), lambda i: (i, 0, 0)),
                 _const_spec((1, D_MODEL)), _const_spec((D_MODEL, D_FF)),
                 _const_spec((D_MODEL, D_FF)), _const_spec((D_FF, D_MODEL))]
    args += [mod, ng, wg, wu, wd]
    if with_final:
        in_specs.append(_const_spec((1, D_MODEL)))
        args.append(final_g)
    return pl.pallas_call(
        functools.partial(_ffn_kernel, mod_row=mod_row, with_pos=with_pos, with_final=with_final),
        grid=(N_BLK,),
        in_specs=in_specs,
        out_specs=row_spec,
        out_shape=jax.ShapeDtypeStruct((N_TOK, D_MODEL), F32),
        compiler_params=pltpu.CompilerParams(dimension_semantics=("arbitrary",),
                                             vmem_limit_bytes=VMEM_LIMIT),
        name="ffn",
    )(*args)


def _ssd_kernel(*refs, L, has_init, want_final):
    it = iter(refs)
    x_ref, mod_ref, ng_ref, w_ref = next(it), next(it), next(it), next(it)
    cw_ref, cb_ref, dtb_ref, alog_ref, d_ref, nw_ref = (next(it), next(it), next(it), next(it),
                                                        next(it), next(it))
    ef_ref, eb_ref = next(it), next(it)
    s0_ref = next(it) if has_init else None
    y_ref = next(it)
    sfin_ref = next(it) if want_final else None
    yacc = next(it)

    C = RB
    nc = L // C
    x = x_ref[...]
    u = _norm_mod(x, ng_ref[...], mod_ref[0, 3:4, :], mod_ref[0, 4:5, :]).astype(BF16)
    p = jnp.dot(u, w_ref[...], preferred_element_type=F32)
    z = p[:, :D_MIX]
    xbc = p[:, D_MIX:D_MIX + SSD_XBC]
    dtp = p[:, D_MIX + SSD_XBC:]
    rows = _iota((L, 1), 0)
    prev = jnp.where(rows == 0, 0.0, pltpu.roll(xbc, 1, 0))
    nxt = jnp.where(rows == L - 1, 0.0, pltpu.roll(xbc, L - 1, 0))
    xc = cb_ref[...] + prev * cw_ref[0:1, :] + xbc * cw_ref[1:2, :] + nxt * cw_ref[2:3, :]
    xc = _silu(xc)
    xs = xc[:, :D_MIX]
    bm = xc[:, D_MIX:D_MIX + SSD_G * SSD_N]
    cm = xc[:, D_MIX + SSD_G * SSD_N:]
    dt = _softplus(dtp + dtb_ref[...])
    adt = dt * (-jnp.exp(alog_ref[...]))

    tril = _tri01(C, True)
    triu = _tri01(C, False)
    tril_b = jnp.where(tril, 1.0, 0.0).astype(BF16)
    triu_b = jnp.where(triu, 1.0, 0.0).astype(BF16)
    ef = ef_ref[...]
    eb = eb_ref[...]

    cs, csT, rcs, rcsT, dtT, loc_f, loc_b = [], [], [], [], [], [], []
    for c in range(nc):
        a_c = adt[c * C:(c + 1) * C]
        a_cT = a_c.T
        cs.append(_sel_dot(tril_b, a_c))
        rcs.append(_sel_dot(triu_b, a_c))
        csT.append(_dot_sel(a_cT, triu_b))
        rcsT.append(_dot_sel(a_cT, tril_b))
        dtT.append(dt[c * C:(c + 1) * C].T)

    need_states = want_final or nc > 1
    if need_states:
        for c in range(nc):
            xs_c = xs[c * C:(c + 1) * C]
            dt_c = dt[c * C:(c + 1) * C]
            wf = jnp.exp(cs[c][C - 1:C, :] - cs[c]) * dt_c
            wb = jnp.exp(rcs[c][0:1, :] - rcs[c]) * dt_c
            xwf = (xs_c * _dot_sel(wf, ef)).T
            xwb = (xs_c * _dot_sel(wb, eb)).T
            lf, lb = [], []
            for h in range(SSD_H):
                g = h // (SSD_H // SSD_G)
                bm_g = bm[c * C:(c + 1) * C, g * SSD_N:(g + 1) * SSD_N]
                lf.append(_dot(xwf[h * SSD_P:(h + 1) * SSD_P, :], bm_g))
                lb.append(_dot(xwb[h * SSD_P:(h + 1) * SSD_P, :], bm_g))
            loc_f.append(lf)
            loc_b.append(lb)

    zero_s = jnp.zeros((SSD_P, SSD_N), F32)
    sin_f = [[None] * SSD_H for _ in range(nc + 1)]
    sin_b = [[None] * SSD_H for _ in range(nc + 1)]
    for h in range(SSD_H):
        sin_f[0][h] = s0_ref[0, 0, h] if has_init else zero_s
        sin_b[nc][h] = s0_ref[0, 1, h] if has_init else zero_s
    if need_states:
        for c in range(nc):
            dec = jnp.exp(cs[c][C - 1:C, :])
            for h in range(SSD_H):
                sin_f[c + 1][h] = dec[:, h:h + 1] * sin_f[c][h] + loc_f[c][h]
        for c in range(nc - 1, -1, -1):
            dec = jnp.exp(rcs[c][0:1, :])
            for h in range(SSD_H):
                sin_b[c][h] = dec[:, SSD_H + h:SSD_H + h + 1] * sin_b[c + 1][h] + loc_b[c][h]

    for c in range(nc):
        sl = slice(c * C, (c + 1) * C)
        ecs = jnp.exp(cs[c])
        ercs = jnp.exp(rcs[c])
        for h in range(SSD_H):
            g = h // (SSD_H // SSD_G)
            cm_g = cm[sl, g * SSD_N:(g + 1) * SSD_N]
            bm_g = bm[sl, g * SSD_N:(g + 1) * SSD_N]
            cb = _dot_nt(cm_g, bm_g)
            lf = jnp.exp(jnp.where(tril, cs[c][:, h:h + 1] - csT[c][h:h + 1, :], -jnp.inf))
            lb = jnp.exp(jnp.where(triu, rcs[c][:, SSD_H + h:SSD_H + h + 1]
                                   - rcsT[c][SSD_H + h:SSD_H + h + 1, :], -jnp.inf))
            m = cb * (lf * dtT[c][h:h + 1, :] + lb * dtT[c][SSD_H + h:SSD_H + h + 1, :])
            y_h = _dot(m, xs[sl, h * SSD_P:(h + 1) * SSD_P])
            if has_init or nc > 1:
                y_h = y_h + ecs[:, h:h + 1] * _dot_nt(cm_g, sin_f[c][h])
                y_h = y_h + ercs[:, SSD_H + h:SSD_H + h + 1] * _dot_nt(cm_g, sin_b[c + 1][h])
            yacc[sl, h * SSD_P:(h + 1) * SSD_P] = y_h

    y = yacc[...] + xs * (d_ref[0:1, :] + d_ref[1:2, :])
    y = _rmsnorm(y * _silu(z), nw_ref[...])
    y_ref[...] = y.astype(BF16)
    if want_final:
        for h in range(SSD_H):
            sfin_ref[0, 0, h] = sin_f[nc][h]
            sfin_ref[0, 1, h] = sin_b[0][h]


def _ssd_call(x, mod, ng, w, cw, cb, dtb, alog, dexp, nw, ef, eb, s0, *, ctx):
    L = SEQ if ctx else DEC_SEQ
    nseq = BATCH if ctx else DEC_BATCH
    blk0 = 0 if ctx else N_CTX_TOK // L
    mod_of = (lambda i: (i, 0, 0)) if ctx else (lambda i: (N_CTX_BLK + LAT_BLK_PER_SEQ * i, 0, 0))
    in_specs = [pl.BlockSpec((L, D_MODEL), lambda i: (blk0 + i, 0)),
                pl.BlockSpec((1, N_MOD, D_MODEL), mod_of),
                _const_spec((1, D_MODEL)), _const_spec((D_MODEL, SSD_W)),
                _const_spec((3, SSD_XBC)), _const_spec((1, SSD_XBC)),
                _const_spec((1, LANE)), _const_spec((1, LANE)),
                _const_spec((2, D_MIX)), _const_spec((1, D_MIX)),
                _const_spec((LANE, D_MIX)), _const_spec((LANE, D_MIX))]
    args = [x, mod, ng, w, cw, cb, dtb, alog, dexp, nw, ef, eb]
    st_spec = pl.BlockSpec((1, 2, SSD_H, SSD_P, SSD_N), lambda i: (i, 0, 0, 0, 0))
    out_specs = [pl.BlockSpec((L, D_MIX), lambda i: (i, 0))]
    out_shape = [jax.ShapeDtypeStruct((nseq * L, D_MIX), BF16)]
    if ctx:
        out_specs.append(st_spec)
        out_shape.append(jax.ShapeDtypeStruct((nseq, 2, SSD_H, SSD_P, SSD_N), F32))
    else:
        in_specs.append(st_spec)
        args.append(s0)
    return pl.pallas_call(
        functools.partial(_ssd_kernel, L=L, has_init=not ctx, want_final=ctx),
        grid=(nseq,),
        in_specs=in_specs,
        out_specs=out_specs,
        out_shape=out_shape,
        scratch_shapes=[pltpu.VMEM((L, D_MIX), F32)],
        compiler_params=pltpu.CompilerParams(dimension_semantics=("arbitrary",),
                                             vmem_limit_bytes=VMEM_LIMIT),
        name="ssd_ctx" if ctx else "ssd_lat",
    )(*args)


def _gla_kernel(*refs, L, has_init, want_final):
    it = iter(refs)
    x_ref, mod_ref, ng_ref, w_ref = next(it), next(it), next(it), next(it)
    gkw_ref, gkb_ref, nw_ref = next(it), next(it), next(it)
    s0_ref = next(it) if has_init else None
    y_ref = next(it)
    sfin_ref = next(it) if want_final else None
    oacc = next(it)

    C = GLA_CHUNK
    B = RB
    cpb = B // C
    nb = L // B
    x = x_ref[...]
    u = _norm_mod(x, ng_ref[...], mod_ref[0, 3:4, :], mod_ref[0, 4:5, :]).astype(BF16)
    p = jnp.dot(u, w_ref[...], preferred_element_type=F32)
    q = p[:, :GLA_QK] * (GLA_DK ** -0.5)
    k = p[:, GLA_QK:2 * GLA_QK]
    v = p[:, 2 * GLA_QK:2 * GLA_QK + D_MIX]
    gg = p[:, 2 * GLA_QK + D_MIX:2 * GLA_QK + 2 * D_MIX]
    lr = p[:, 2 * GLA_QK + 2 * D_MIX:].astype(BF16)
    lg = []
    for d in range(2):
        pre = jnp.dot(lr, gkw_ref[d], preferred_element_type=F32) + gkb_ref[d:d + 1, :]
        lg.append(-_softplus(-pre) / GLA_GATE_NORM)

    low = _tri01(B, True, C)
    upp = _tri01(B, False, C)
    low_b = jnp.where(low, 1.0, 0.0).astype(BF16)
    upp_b = jnp.where(upp, 1.0, 0.0).astype(BF16)
    rowi = _iota((B, 1), 0)

    zero_s = jnp.zeros((LANE, GLA_DV), F32)
    if has_init:
        pad = jnp.zeros((LANE - GLA_DK, GLA_DV), F32)
        s_f = [jnp.concatenate([s0_ref[0, 0, h], pad], axis=0) for h in range(GLA_H)]
        s_b0 = [jnp.concatenate([s0_ref[0, 1, h], pad], axis=0) for h in range(GLA_H)]
    else:
        s_f = [zero_s] * GLA_H
        s_b0 = [zero_s] * GLA_H

    blocks = []
    for b in range(nb):
        sl = slice(b * B, (b + 1) * B)
        bf = _sel_dot(low_b, lg[0][sl])
        rb = _sel_dot(upp_b, lg[1][sl])
        q_b, k_b = q[sl], k[sl]
        blocks.append(dict(sl=sl, bf=bf, rb=rb, bfT=bf.T, rbT=rb.T,
                           qf=q_b * jnp.exp(bf), kf=k_b * jnp.exp(-bf),
                           qb=q_b * jnp.exp(rb), kb=k_b * jnp.exp(-rb), k=k_b))

    for blk in blocks:
        sl = blk["sl"]
        for h in range(GLA_H):
            hs = slice(h * LANE, (h + 1) * LANE)
            a = jnp.where(low, _dot_nt(blk["qf"][:, hs], blk["kf"][:, hs]), 0.0)
            a = a + jnp.where(upp, _dot_nt(blk["qb"][:, hs], blk["kb"][:, hs]), 0.0)
            oacc[sl, h * GLA_DV:(h + 1) * GLA_DV] = _dot(a, v[sl, h * GLA_DV:(h + 1) * GLA_DV])

    for blk in blocks:
        sl = blk["sl"]
        r0 = sl.start
        for c in range(cpb):
            last = blk["bf"][(c + 1) * C - 1:(c + 1) * C, :]
            kdecT = (blk["k"] * jnp.exp(last - blk["bf"])).T
            in_c = (rowi >= c * C) & (rowi < (c + 1) * C)
            rs = slice(r0 + c * C, r0 + (c + 1) * C)
            for h in range(GLA_H):
                hs = slice(h * LANE, (h + 1) * LANE)
                vs = slice(h * GLA_DV, (h + 1) * GLA_DV)
                oacc[rs, vs] = oacc[rs, vs] + _dot(blk["qf"][c * C:(c + 1) * C, hs], s_f[h])
                kv = _dot(kdecT[hs, :], jnp.where(in_c, v[sl, vs], 0.0))
                dec = jnp.exp(blk["bfT"][hs, (c + 1) * C - 1:(c + 1) * C])
                s_f[h] = dec * s_f[h] + kv

    s_b = s_b0
    for blk in reversed(blocks):
        sl = blk["sl"]
        r0 = sl.start
        for c in range(cpb - 1, -1, -1):
            first = blk["rb"][c * C:c * C + 1, :]
            kdecT = (blk["k"] * jnp.exp(first - blk["rb"])).T
            in_c = (rowi >= c * C) & (rowi < (c + 1) * C)
            rs = slice(r0 + c * C, r0 + (c + 1) * C)
            for h in range(GLA_H):
                hs = slice(h * LANE, (h + 1) * LANE)
                vs = slice(h * GLA_DV, (h + 1) * GLA_DV)
                oacc[rs, vs] = oacc[rs, vs] + _dot(blk["qb"][c * C:(c + 1) * C, hs], s_b[h])
                kv = _dot(kdecT[hs, :], jnp.where(in_c, v[sl, vs], 0.0))
                dec = jnp.exp(blk["rbT"][hs, c * C:c * C + 1])
                s_b[h] = dec * s_b[h] + kv

    for h in range(GLA_H):
        vs = slice(h * GLA_DV, (h + 1) * GLA_DV)
        o_h = _rmsnorm(oacc[:, vs], nw_ref[...])
        y_ref[:, vs] = (o_h * _silu(gg[:, vs])).astype(BF16)
        if want_final:
            sfin_ref[0, 0, h] = s_f[h][:GLA_DK, :]
            sfin_ref[0, 1, h] = s_b[h][:GLA_DK, :]


def _gla_call(x, mod, ng, w, gkw, gkb, nw, s0, *, ctx):
    L = SEQ if ctx else DEC_SEQ
    nseq = BATCH if ctx else DEC_BATCH
    blk0 = 0 if ctx else N_CTX_TOK // L
    mod_of = (lambda i: (i, 0, 0)) if ctx else (lambda i: (N_CTX_BLK + LAT_BLK_PER_SEQ * i, 0, 0))
    in_specs = [pl.BlockSpec((L, D_MODEL), lambda i: (blk0 + i, 0)),
                pl.BlockSpec((1, N_MOD, D_MODEL), mod_of),
                _const_spec((1, D_MODEL)), _const_spec((D_MODEL, GLA_W)),
                _const_spec((2, LANE, GLA_QK)), _const_spec((2, GLA_QK)),
                _const_spec((1, GLA_DV))]
    args = [x, mod, ng, w, gkw, gkb, nw]
    st_spec = pl.BlockSpec((1, 2, GLA_H, GLA_DK, GLA_DV), lambda i: (i, 0, 0, 0, 0))
    out_specs = [pl.BlockSpec((L, D_MIX), lambda i: (i, 0))]
    out_shape = [jax.ShapeDtypeStruct((nseq * L, D_MIX), BF16)]
    if ctx:
        out_specs.append(st_spec)
        out_shape.append(jax.ShapeDtypeStruct((nseq, 2, GLA_H, GLA_DK, GLA_DV), F32))
    else:
        in_specs.append(st_spec)
        args.append(s0)
    return pl.pallas_call(
        functools.partial(_gla_kernel, L=L, has_init=not ctx, want_final=ctx),
        grid=(nseq,),
        in_specs=in_specs,
        out_specs=out_specs,
        out_shape=out_shape,
        scratch_shapes=[pltpu.VMEM((L, D_MIX), F32)],
        compiler_params=pltpu.CompilerParams(dimension_semantics=("arbitrary",),
                                             vmem_limit_bytes=VMEM_LIMIT),
        name="gla_ctx" if ctx else "gla_lat",
    )(*args)


def _rw_prep_kernel(x_ref, xp_ref, xn_ref, mod_ref, ng_ref, w_ref, mu_ref, a0_ref, a2_ref, g2_ref,
                    kkw_ref, ka_ref, rk_ref, w0_ref, w2_ref, bd_ref,
                    r_ref, wf_ref, wb_ref, k_ref, v_ref, kk_ref, nkka_ref, g_ref, bonus_ref):
    i = pl.program_id(0)
    j = (i - N_CTX_BLK) % LAT_BLK_PER_SEQ
    is_first = (i < N_CTX_BLK) | (j == 0)
    is_last = (i < N_CTX_BLK) | (j == LAT_BLK_PER_SEQ - 1)
    ng = ng_ref[...]
    shift, scale = mod_ref[0, 3:4, :], mod_ref[0, 4:5, :]
    w = w_ref[...]

    def proj(xr):
        return jnp.dot(_norm_mod(xr, ng, shift, scale).astype(BF16), w, preferred_element_type=F32)

    p = proj(x_ref[...])
    p_prev = jnp.where(is_first, 0.0, proj(xp_ref[...])[SUB - 1:SUB, :])
    p_next = jnp.where(is_last, 0.0, proj(xn_ref[...])[0:1, :])
    rows = _iota((RB, 1), 0)
    prev = jnp.where(rows == 0, p_prev, pltpu.roll(p, 1, 0))
    nxt = jnp.where(rows == RB - 1, p_next, pltpu.roll(p, RB - 1, 0))
    p = p + (0.5 * (prev + nxt) - p) * mu_ref[...]

    r = p[:, :D_MIX]
    k = p[:, D_MIX:2 * D_MIX]
    v = p[:, 2 * D_MIX:3 * D_MIX]
    wlr = p[:, 3 * D_MIX:3 * D_MIX + LANE]
    glr = p[:, 3 * D_MIX + LANE:3 * D_MIX + 2 * LANE]
    alr = p[:, 3 * D_MIX + 2 * LANE:]
    bd = bd_ref[...]
    a = _sigmoid(a0_ref[...] + _dot(alr, a2_ref[...]))
    g = _dot(_sigmoid(glr), g2_ref[...])
    kk = k * kkw_ref[...]
    kk = kk / jnp.maximum(jnp.sqrt(_dot_sel(kk * kk, bd)), 1e-12)
    k = k * (1.0 + (a - 1.0) * ka_ref[...])
    tw = jnp.tanh(wlr).astype(BF16)
    for d, o_ref in ((0, wf_ref), (1, wb_ref)):
        pre = w0_ref[d:d + 1, :] + jnp.dot(tw, w2_ref[d], preferred_element_type=F32)
        wl = -_softplus(-pre) - 0.5
        o_ref[...] = jnp.exp(-jnp.exp(wl))
    r_ref[...] = r
    k_ref[...] = k
    v_ref[...] = v
    kk_ref[...] = kk
    nkka_ref[...] = -(kk * a)
    g_ref[...] = g
    bonus_ref[...] = _dot_sel(r * k * rk_ref[...], bd) * v


def _rw_prep_call(x, mod, ng, w, mu, a0, a2, g2, kkw, ka, rk, w0, w2, bd):
    hb = RB // SUB
    row_spec = pl.BlockSpec((RB, D_MODEL), lambda i: (i, 0))
    out_spec = pl.BlockSpec((RB, D_MIX), lambda i: (i, 0))
    in_specs = [row_spec,
                pl.BlockSpec((SUB, D_MODEL), lambda i: (jnp.maximum(i * hb - 1, 0), 0)),
                pl.BlockSpec((SUB, D_MODEL), lambda i: (jnp.minimum((i + 1) * hb, N_TOK // SUB - 1), 0)),
                pl.BlockSpec((1, N_MOD, D_MODEL), lambda i: (i, 0, 0)),
                _const_spec((1, D_MODEL)), _const_spec((D_MODEL, RW_W)), _const_spec((1, RW_W)),
                _const_spec((1, D_MIX)), _const_spec((LANE, D_MIX)), _const_spec((LANE, D_MIX)),
                _const_spec((1, D_MIX)), _const_spec((1, D_MIX)), _const_spec((1, D_MIX)),
                _const_spec((2, D_MIX)), _const_spec((2, LANE, D_MIX)), _const_spec((D_MIX, D_MIX))]
    return pl.pallas_call(
        _rw_prep_kernel,
        grid=(N_BLK,),
        in_specs=in_specs,
        out_specs=[out_spec] * 9,
        out_shape=[jax.ShapeDtypeStruct((N_TOK, D_MIX), F32)] * 9,
        compiler_params=pltpu.CompilerParams(dimension_semantics=("arbitrary",),
                                             vmem_limit_bytes=VMEM_LIMIT),
        name="rw_prep",
    )(x, x, x, mod, ng, w, mu, a0, a2, g2, kkw, ka, rk, w0, w2, bd)


def _rw_scan_kernel(*refs, vb, has_init, want_final):
    it = iter(refs)
    r_ref, w_ref, k_ref, v_ref, kk_ref, nkka_ref = (next(it), next(it), next(it), next(it),
                                                    next(it), next(it))
    s0_ref = next(it) if has_init else None
    o_ref = next(it)
    sfin_ref = next(it) if want_final else None
    s_scr = next(it)
    tb = pl.program_id(1)

    @pl.when(tb == 0)
    def _():
        if has_init:
            s_scr[...] = s0_ref[0]
        else:
            s_scr[...] = jnp.zeros(s_scr.shape, F32)

    def bcast(ref, i, kx):
        return jnp.broadcast_to(ref[0, i, kx:kx + 1, :], (SUB, LANE))

    def step(i, carry):
        skk = [jnp.zeros((SUB, LANE), F32) for _ in range(vb)]
        for kx in range(RW_N):
            kkb = bcast(kk_ref, i, kx)
            for j in range(vb):
                skk[j] = skk[j] + s_scr[kx, j * SUB:(j + 1) * SUB, :] * kkb
        vv = [v_ref[0, i, j * SUB:(j + 1) * SUB, :] for j in range(vb)]
        oacc = [jnp.zeros((SUB, LANE), F32) for _ in range(vb)]
        for kx in range(RW_N):
            wb = bcast(w_ref, i, kx)
            ab = bcast(nkka_ref, i, kx)
            kb = bcast(k_ref, i, kx)
            rb = bcast(r_ref, i, kx)
            for j in range(vb):
                s = s_scr[kx, j * SUB:(j + 1) * SUB, :] * wb + skk[j] * ab + vv[j] * kb
                s_scr[kx, j * SUB:(j + 1) * SUB, :] = s
                oacc[j] = oacc[j] + s * rb
        for j in range(vb):
            o_ref[0, i, j * SUB:(j + 1) * SUB, :] = oacc[j]
        return carry

    lax.fori_loop(0, SCAN_TB, step, 0)

    if want_final:
        @pl.when(tb == pl.num_programs(1) - 1)
        def _():
            sfin_ref[0] = s_scr[...]


def _rw_scan_call(r, w, k, v, kk, nkka, s0, *, want_final):
    G, L = r.shape[0], r.shape[1]
    nv = v.shape[2]
    has_init = s0 is not None
    kspec = pl.BlockSpec((1, SCAN_TB, RW_N, LANE), lambda g, t: (g, t, 0, 0))
    vspec = pl.BlockSpec((1, SCAN_TB, nv, LANE), lambda g, t: (g, t, 0, 0))
    sspec = pl.BlockSpec((1, RW_N, nv, LANE), lambda g, t: (g, 0, 0, 0))
    in_specs = [kspec, kspec, kspec, vspec, kspec, kspec]
    args = [r, w, k, v, kk, nkka]
    if has_init:
        in_specs.append(sspec)
        args.append(s0)
    out_specs = [vspec]
    out_shape = [jax.ShapeDtypeStruct((G, L, nv, LANE), F32)]
    if want_final:
        out_specs.append(sspec)
        out_shape.append(jax.ShapeDtypeStruct((G, RW_N, nv, LANE), F32))
    return pl.pallas_call(
        functools.partial(_rw_scan_kernel, vb=nv // SUB, has_init=has_init, want_final=want_final),
        grid=(G, L // SCAN_TB),
        in_specs=in_specs,
        out_specs=out_specs,
        out_shape=out_shape,
        scratch_shapes=[pltpu.VMEM((RW_N, nv, LANE), F32)],
        compiler_params=pltpu.CompilerParams(dimension_semantics=("arbitrary", "arbitrary"),
                                             vmem_limit_bytes=VMEM_LIMIT),
        name="rw_scan",
    )(*args)


def _merge_kernel(x_ref, mod_ref, ng_ref, wgate_ref, yssd_ref, ygla_ref, orw_ref, bonus_ref, g_ref,
                  lnw_ref, lnb_ref, bd_ref, wso_ref, wgo_ref, wro_ref, wout_ref, o_ref):
    x = x_ref[...]
    u = _norm_mod(x, ng_ref[...], mod_ref[0, 3:4, :], mod_ref[0, 4:5, :]).astype(BF16)
    bd = bd_ref[...]
    o = orw_ref[...]
    mu = _dot_sel(o, bd) * (1.0 / RW_N)
    oc = o - mu
    var = _dot_sel(oc * oc, bd) * (1.0 / RW_N)
    o = oc * lax.rsqrt(var + RW_GN_EPS) * lnw_ref[...] + lnb_ref[...]
    y_rw = ((o + bonus_ref[...]) * g_ref[...]).astype(BF16)
    merged = jnp.zeros((RB, D_MODEL), F32)
    for b, (y, wo_ref) in enumerate(((yssd_ref[...], wso_ref), (ygla_ref[...], wgo_ref),
                                     (y_rw, wro_ref))):
        gate = _sigmoid(jnp.dot(u, wgate_ref[:, b * D_MODEL:(b + 1) * D_MODEL],
                                preferred_element_type=F32))
        merged = merged + gate * jnp.dot(y, wo_ref[...], preferred_element_type=F32)
    m = jnp.dot(merged.astype(BF16), wout_ref[...], preferred_element_type=F32)
    o_ref[...] = x + mod_ref[0, 5:6, :] * m


def _merge_call(x, mod, ng, wgate, yssd, ygla, orw, bonus, g, lnw, lnb, bd, wso, wgo, wro, wout):
    row_spec = pl.BlockSpec((RB, D_MODEL), lambda i: (i, 0))
    mix_spec = pl.BlockSpec((RB, D_MIX), lambda i: (i, 0))
    in_specs = [row_spec, pl.BlockSpec((1, N_MOD, D_MODEL), lambda i: (i, 0, 0)),
                _const_spec((1, D_MODEL)), _const_spec((D_MODEL, 3 * D_MODEL)),
                mix_spec, mix_spec, mix_spec, mix_spec, mix_spec,
                _const_spec((1, D_MIX)), _const_spec((1, D_MIX)), _const_spec((D_MIX, D_MIX)),
                _const_spec((D_MIX, D_MODEL)), _const_spec((D_MIX, D_MODEL)),
                _const_spec((D_MIX, D_MODEL)), _const_spec((D_MODEL, D_MODEL))]
    return pl.pallas_call(
        _merge_kernel,
        grid=(N_BLK,),
        in_specs=in_specs,
        out_specs=row_spec,
        out_shape=jax.ShapeDtypeStruct((N_TOK, D_MODEL), F32),
        compiler_params=pltpu.CompilerParams(dimension_semantics=("arbitrary",),
                                             vmem_limit_bytes=VMEM_LIMIT),
        name="merge",
    )(x, mod, ng, wgate, yssd, ygla, orw, bonus, g, lnw, lnb, bd, wso, wgo, wro, wout)


def _grid_pos_embed(rows, cols, dim):
    quarter = dim // 4
    omega = 1.0 / (10000.0 ** (jnp.arange(quarter, dtype=F32) / quarter))
    er = jnp.arange(rows, dtype=F32)[:, None] * omega
    ec = jnp.arange(cols, dtype=F32)[:, None] * omega
    er = jnp.concatenate([jnp.sin(er), jnp.cos(er)], axis=-1)
    ec = jnp.concatenate([jnp.sin(ec), jnp.cos(ec)], axis=-1)
    emb = jnp.concatenate([jnp.broadcast_to(er[:, None], (rows, cols, dim // 2)),
                           jnp.broadcast_to(ec[None], (rows, cols, dim // 2))], axis=-1)
    return emb.reshape(rows * cols, dim)


def _pad_cols(a, n):
    return jnp.pad(a, [(0, 0)] * (a.ndim - 1) + [(0, n - a.shape[-1])])


def _head_pad(a):
    a = a.reshape(a.shape[:-1] + (GLA_H, GLA_DK))
    return _pad_cols(a, LANE).reshape(a.shape[:-2] + (GLA_QK,))


def _rows_at(a, off, n):
    return jnp.pad(a, ((off, n - off - a.shape[0]), (0, 0)))


def _block_diag_ones(n, blk):
    i = np.arange(n)
    return jnp.asarray((i[:, None] // blk) == (i[None, :] // blk), BF16)


def _expand01(row0):
    m = np.zeros((LANE, D_MIX), np.float32)
    for h in range(SSD_H):
        m[row0 + h, h * SSD_P:(h + 1) * SSD_P] = 1.0
    return jnp.asarray(m, BF16)


def _to_scan_ctx(a, flip_bwd=True):
    t = a.reshape(BATCH, SEQ, RW_H, RW_N).transpose(1, 3, 0, 2).reshape(SEQ, RW_N, BATCH * RW_H)
    return jnp.stack([t, t[::-1]])


def _to_scan_lat_k(a_f, a_b):
    def tr(a):
        return a.reshape(DEC_BATCH, DEC_SEQ, RW_H, RW_N).transpose(1, 3, 0, 2).reshape(
            DEC_SEQ, RW_N, DEC_BATCH * RW_H)
    half = jnp.concatenate([tr(a_f), tr(a_b)[::-1]], axis=-1)
    return jnp.concatenate([half, half], axis=-1)[None]


def _to_scan_lat_v(a):
    t = a.reshape(DEC_BATCH, DEC_SEQ, RW_H, 2, RW_N // 2).transpose(1, 4, 3, 0, 2).reshape(
        DEC_SEQ, RW_N // 2, 2, 1, DEC_BATCH * RW_H)
    t = jnp.concatenate([t, t[::-1]], axis=3)
    return t.reshape(DEC_SEQ, RW_N // 2, LANE)[None]


def kernel(x_prompt, x_sample, state_ssd, state_gla, state_rwkv, c, c_ctx, norm_g, w_ada, b_ada,
           ffn_gate, ffn_up, ffn_down, w_in, ssd_conv_w, ssd_conv_b, ssd_dt_bias, ssd_A_log, ssd_D,
           ssd_norm, w_ssd_o, gla_gk_w, gla_gk_b, gla_norm, w_gla_o, rw_mu, rw_w0, rw_w2, rw_a0,
           rw_a2, rw_g2, rw_kk, rw_ka, rw_rk, rw_ln_w, rw_ln_b, w_rw_o, w_out, final_norm):
    rows = DEC_SEQ // GRID_W
    pos = jnp.concatenate([jnp.zeros((RB, D_MODEL), F32), _grid_pos_embed(rows, GRID_W, D_MODEL)])
    x = jnp.concatenate([x_prompt.reshape(N_CTX_TOK, D_MODEL), x_sample.reshape(-1, D_MODEL)])

    cond8 = jnp.concatenate([c_ctx[None], c, jnp.zeros((SUB - 1 - DEC_BATCH, D_MODEL), F32)])
    ada = _ada_call(cond8, w_ada, b_ada)
    cond_of_blk = np.concatenate([np.zeros(N_CTX_BLK, np.int32),
                                  1 + np.arange(N_BLK - N_CTX_BLK, dtype=np.int32) // LAT_BLK_PER_SEQ])

    bd = _block_diag_ones(D_MIX, RW_N)
    ef, eb = _expand01(0), _expand01(SSD_H)
    o_ssd = D_MIX + SSD_XBC + 2 * SSD_H
    o_gla = o_ssd + 2 * GLA_H * GLA_DK + 2 * D_MIX + 2 * GLA_LR
    o_rw = o_gla + 3 * D_MIX + 2 * RW_LW + RW_LA + RW_LG

    new_ssd, new_gla, new_rw = [], [], []
    for l in range(DEPTH):
        mod = ada[l][cond_of_blk].reshape(N_BLK, N_MOD, D_MODEL)
        ng = norm_g[l]
        wi = w_in[l]
        w_ssd = _pad_cols(wi[:, :o_ssd], SSD_W).astype(BF16)
        wg_ = wi[:, o_ssd:o_gla]
        w_gla = jnp.concatenate([_head_pad(wg_[:, :256]), _head_pad(wg_[:, 256:512]),
                                 wg_[:, 512:1536], _pad_cols(wg_[:, 1536:], LANE)], axis=1).astype(BF16)
        wr_ = wi[:, o_gla:o_rw]
        w_rw = jnp.concatenate([wr_[:, :1664], wr_[:, 1728:1856], _pad_cols(wr_[:, 1664:1728], LANE)],
                               axis=1).astype(BF16)
        mu_ = rw_mu[l]
        mu = jnp.concatenate([mu_[:1664], mu_[1728:1856], _pad_cols(mu_[1664:1728], LANE)])[None]
        w_gate = wi[:, o_rw:].astype(BF16)

        x = _ffn_call(x, mod, ng[0:1], ffn_gate[l, 0].astype(BF16), ffn_up[l, 0].astype(BF16),
                      ffn_down[l, 0].astype(BF16), mod_row=0, pos=pos if l == 0 else None)

        ssd_args = (x, mod, ng[1:2], w_ssd, ssd_conv_w[l], ssd_conv_b[l][None],
                    _pad_cols(ssd_dt_bias[l].reshape(1, -1), LANE),
                    _pad_cols(ssd_A_log[l].reshape(1, -1), LANE),
                    jnp.repeat(ssd_D[l], SSD_P, axis=1), ssd_norm[l][None], ef, eb)
        y_ssd_c, s_ssd = _ssd_call(*ssd_args, None, ctx=True)
        (y_ssd_l,) = _ssd_call(*ssd_args, state_ssd[:, l], ctx=False)
        y_ssd = jnp.concatenate([y_ssd_c, y_ssd_l])
        new_ssd.append(s_ssd)

        gkw = jnp.stack([_rows_at(_head_pad(gla_gk_w[l, d]), d * GLA_LR, LANE) for d in range(2)])
        gla_args = (x, mod, ng[1:2], w_gla, gkw.astype(BF16), _head_pad(gla_gk_b[l]),
                    gla_norm[l][None])
        y_gla_c, s_gla = _gla_call(*gla_args, None, ctx=True)
        (y_gla_l,) = _gla_call(*gla_args, state_gla[:, l], ctx=False)
        y_gla = jnp.concatenate([y_gla_c, y_gla_l])
        new_gla.append(s_gla)

        w2p = jnp.stack([_rows_at(rw_w2[l, d], d * RW_LW, LANE) for d in range(2)]).astype(BF16)
        r, wf, wb, k, v, kk, nkka, g, bonus = _rw_prep_call(
            x, mod, ng[1:2], w_rw, mu, rw_a0[l][None], _rows_at(rw_a2[l], 0, LANE).astype(BF16),
            rw_g2[l].astype(BF16), rw_kk[l][None], rw_ka[l][None], rw_rk[l].reshape(1, D_MIX),
            rw_w0[l], w2p, bd)
        nc = N_CTX_TOK
        w_ctx = jnp.stack([_to_scan_ctx(wf[:nc])[0], _to_scan_ctx(wb[:nc])[1]])
        o_c, s_rw = _rw_scan_call(_to_scan_ctx(r[:nc]), w_ctx, _to_scan_ctx(k[:nc]),
                                  _to_scan_ctx(v[:nc]), _to_scan_ctx(kk[:nc]),
                                  _to_scan_ctx(nkka[:nc]), None, want_final=True)
        o_c = o_c[0] + o_c[1][::-1]
        o_c = o_c.reshape(SEQ, RW_N, BATCH, RW_H).transpose(2, 0, 3, 1).reshape(nc, D_MIX)
        new_rw.append(s_rw.reshape(2, RW_N, RW_N, BATCH, RW_H).transpose(3, 0, 4, 2, 1))
        st = state_rwkv[:, l].reshape(DEC_BATCH, 2, RW_H, 2, RW_N // 2, RW_N)
        s0 = st.transpose(5, 4, 3, 1, 0, 2).reshape(1, RW_N, RW_N // 2, LANE)
        (o_l,) = _rw_scan_call(_to_scan_lat_k(r[nc:], r[nc:]), _to_scan_lat_k(wf[nc:], wb[nc:]),
                               _to_scan_lat_k(k[nc:], k[nc:]), _to_scan_lat_v(v[nc:]),
                               _to_scan_lat_k(kk[nc:], kk[nc:]), _to_scan_lat_k(nkka[nc:], nkka[nc:]),
                               s0, want_final=False)
        o_l = o_l[0].reshape(DEC_SEQ, RW_N // 2, 2, 2, DEC_BATCH, RW_H)
        o_l = o_l[:, :, :, 0] + o_l[::-1, :, :, 1]
        o_l = o_l.transpose(3, 0, 4, 2, 1).reshape(N_TOK - nc, D_MIX)
        o_rw_all = jnp.concatenate([o_c, o_l])

        x = _merge_call(x, mod, ng[1:2], w_gate, y_ssd, y_gla, o_rw_all, bonus, g,
                        rw_ln_w[l][None], rw_ln_b[l][None], bd, w_ssd_o[l].astype(BF16),
                        w_gla_o[l].astype(BF16), w_rw_o[l].astype(BF16), w_out[l].astype(BF16))

        x = _ffn_call(x, mod, ng[2:3], ffn_gate[l, 1].astype(BF16), ffn_up[l, 1].astype(BF16),
                      ffn_down[l, 1].astype(BF16), mod_row=6,
                      final_g=final_norm[None] if l == DEPTH - 1 else None)

    y_prompt = x[:N_CTX_TOK].reshape(BATCH, SEQ, D_MODEL)
    y_sample = x[N_CTX_TOK:].reshape(DEC_BATCH, DEC_SEQ, D_MODEL)
    return (y_prompt, y_sample, jnp.stack(new_ssd, axis=1), jnp.stack(new_gla, axis=1),
            jnp.stack(new_rw, axis=1))
```

```python
import functools
import math

import numpy as np
import jax
import jax.numpy as jnp
from jax import lax
from jax.experimental import pallas as pl
from jax.experimental.pallas import tpu as pltpu

F32 = jnp.float32
BF16 = jnp.bfloat16

D_MODEL = 1024
BATCH = 16
SEQ = 256
DEPTH = 2
DEC_BATCH = 4
DEC_SEQ = 1024
GRID_W = 64
D_MIX = 512
D_FF = 2816
N_MOD = 9
SSD_P = 64
SSD_H = 8
SSD_N = 64
SSD_G = 2
SSD_XBC = 768
GLA_H = 4
GLA_DK = 64
GLA_DV = 128
GLA_LR = 16
GLA_GATE_NORM = 16.0
GLA_CHUNK = 64
RW_N = 64
RW_H = 8
RW_LW = 64
RW_LA = 64
RW_LG = 128
RMS_EPS = 1e-6
RW_GN_EPS = 64e-5

LANE = 128
SUB = 8
RB = 256
N_CTX_TOK = BATCH * SEQ
N_TOK = N_CTX_TOK + DEC_BATCH * DEC_SEQ
N_BLK = N_TOK // RB
N_CTX_BLK = N_CTX_TOK // RB
LAT_BLK_PER_SEQ = DEC_SEQ // RB
SSD_W = D_MIX + SSD_XBC + LANE
GLA_QK = GLA_H * LANE
GLA_W = 2 * GLA_QK + 2 * D_MIX + LANE
RW_W = 3 * D_MIX + 3 * LANE
SCAN_TB = 32
VMEM_LIMIT = 56 * 1024 * 1024


def _dot(a, b):
    return jnp.dot(a.astype(BF16), b.astype(BF16), preferred_element_type=F32)


def _dot_nt(a, b):
    return lax.dot_general(a.astype(BF16), b.astype(BF16), (((1,), (1,)), ((), ())),
                           preferred_element_type=F32)


def _split3(x):
    hi = x.astype(BF16)
    r1 = x - hi.astype(F32)
    mid = r1.astype(BF16)
    lo = (r1 - mid.astype(F32)).astype(BF16)
    return hi, mid, lo


def _sel_dot(m01, x):
    hi, mid, lo = _split3(x)
    f = lambda p: jnp.dot(m01, p, preferred_element_type=F32)
    return f(hi) + f(mid) + f(lo)


def _dot_sel(x, m01):
    hi, mid, lo = _split3(x)
    f = lambda p: jnp.dot(p, m01, preferred_element_type=F32)
    return f(hi) + f(mid) + f(lo)


def _sigmoid(x):
    return jax.nn.sigmoid(x)


def _silu(x):
    return x * jax.nn.sigmoid(x)


def _softplus(x):
    return jnp.maximum(x, 0.0) + jnp.log1p(jnp.exp(-jnp.abs(x)))


def _rmsnorm(x, g):
    return x * lax.rsqrt(jnp.mean(x * x, axis=-1, keepdims=True) + RMS_EPS) * g


def _norm_mod(x, g, shift, scale):
    return _rmsnorm(x, g) * (1.0 + scale) + shift


def _iota(shape, dim):
    return lax.broadcasted_iota(jnp.int32, shape, dim)


def _tri01(n, lower, chunk=None):
    t = _iota((n, n), 0)
    s = _iota((n, n), 1)
    m = (s <= t) if lower else (s >= t)
    if chunk is not None:
        m = m & ((t // chunk) == (s // chunk))
    return m


def _ada_kernel(c_ref, w_ref, b_ref, o_ref):
    o_ref[0] = _dot(_silu(c_ref[...]), w_ref[0]) + b_ref[0]


def _ada_call(cond8, w_ada, b_ada):
    tn = 1024
    nj = (N_MOD * D_MODEL) // tn
    return pl.pallas_call(
        _ada_kernel,
        grid=(DEPTH, nj),
        in_specs=[pl.BlockSpec((SUB, D_MODEL), lambda l, j: (0, 0)),
                  pl.BlockSpec((1, D_MODEL, tn), lambda l, j: (l, 0, j)),
                  pl.BlockSpec((1, 1, tn), lambda l, j: (l, 0, j))],
        out_specs=pl.BlockSpec((1, SUB, tn), lambda l, j: (l, 0, j)),
        out_shape=jax.ShapeDtypeStruct((DEPTH, SUB, N_MOD * D_MODEL), F32),
        compiler_params=pltpu.CompilerParams(dimension_semantics=("arbitrary", "arbitrary"),
                                             vmem_limit_bytes=VMEM_LIMIT),
        name="ada",
    )(cond8, w_ada, b_ada.reshape(DEPTH, 1, N_MOD * D_MODEL))


def _ffn_kernel(*refs, mod_row, first, last):
    it = iter(refs)
    is_ctx = pl.program_id(0) < N_CTX_BLK
    if first:
        xc_ref, xl_ref, pos_ref = next(it), next(it), next(it)
        x = jnp.where(is_ctx, xc_ref[...], xl_ref[...] + pos_ref[...])
    else:
        x = next(it)[...]
    mod_ref, ng_ref, wg_ref, wu_ref, wd_ref = next(it), next(it), next(it), next(it), next(it)
    fin_ref = next(it) if last else None
    shift = mod_ref[0, mod_row:mod_row + 1, :]
    scale = mod_ref[0, mod_row + 1:mod_row + 2, :]
    gate = mod_ref[0, mod_row + 2:mod_row + 3, :]
    h = _norm_mod(x, ng_ref[...], shift, scale).astype(BF16)
    a = _silu(jnp.dot(h, wg_ref[...], preferred_element_type=F32))
    a = (a * jnp.dot(h, wu_ref[...], preferred_element_type=F32)).astype(BF16)
    y = x + 0.5 * gate * jnp.dot(a, wd_ref[...], preferred_element_type=F32)
    if last:
        y = _rmsnorm(y, fin_ref[...])
        oc_ref, ol_ref = next(it), next(it)

        @pl.when(is_ctx)
        def _():
            oc_ref[...] = y

        @pl.when(jnp.logical_not(is_ctx))
        def _():
            ol_ref[...] = y
    else:
        next(it)[...] = y


def _const_spec(shape):
    nd = len(shape)
    return pl.BlockSpec(shape, lambda i: (0,) * nd, pipeline_mode=pl.Buffered(1))


def _ctx_idx(i):
    return jnp.minimum(i, N_CTX_BLK - 1)


def _lat_idx(i):
    return jnp.maximum(i - N_CTX_BLK, 0)


def _ffn_call(x, mod, ng, wg, wu, wd, *, mod_row, pos=None, final_g=None):
    first = pos is not None
    last = final_g is not None
    row_spec = pl.BlockSpec((RB, D_MODEL), lambda i: (i, 0))
    ctx_spec = pl.BlockSpec((RB, D_MODEL), lambda i: (_ctx_idx(i), 0))
    lat_spec = pl.BlockSpec((RB, D_MODEL), lambda i: (_lat_idx(i), 0))
    if first:
        in_specs = [ctx_spec, lat_spec,
                    pl.BlockSpec((RB, D_MODEL), lambda i: (_lat_idx(i) % LAT_BLK_PER_SEQ, 0))]
        args = [x[0], x[1], pos]
    else:
        in_specs = [row_spec]
        args = [x]
    in_specs += [pl.BlockSpec((1, N_MOD, D_MODEL), lambda i: (i, 0, 0)),
                 _const_spec((1, D_MODEL)), _const_spec((D_MODEL, D_FF)),
                 _const_spec((D_MODEL, D_FF)), _const_spec((D_FF, D_MODEL))]
    args += [mod, ng, wg, wu, wd]
    if last:
        in_specs.append(_const_spec((1, D_MODEL)))
        args.append(final_g)
        half = jax.ShapeDtypeStruct((N_TOK // 2, D_MODEL), F32)
        out_specs, out_shape = [ctx_spec, lat_spec], [half, half]
    else:
        out_specs, out_shape = row_spec, jax.ShapeDtypeStruct((N_TOK, D_MODEL), F32)
    return pl.pallas_call(
        functools.partial(_ffn_kernel, mod_row=mod_row, first=first, last=last),
        grid=(N_BLK,),
        in_specs=in_specs,
        out_specs=out_specs,
        out_shape=out_shape,
        compiler_params=pltpu.CompilerParams(dimension_semantics=("arbitrary",),
                                             vmem_limit_bytes=VMEM_LIMIT),
        name="ffn",
    )(*args)


def _ssd_kernel(*refs, L, has_init, want_final):
    it = iter(refs)
    x_ref, mod_ref, ng_ref, w_ref = next(it), next(it), next(it), next(it)
    cw_ref, cb_ref, dtb_ref, alog_ref, d_ref, nw_ref = (next(it), next(it), next(it), next(it),
                                                        next(it), next(it))
    ef_ref, eb_ref = next(it), next(it)
    s0_ref = next(it) if has_init else None
    y_ref = next(it)
    sfin_ref = next(it) if want_final else None
    yacc = next(it)

    C = RB
    nc = L // C
    x = x_ref[...]
    u = _norm_mod(x, ng_ref[...], mod_ref[0, 3:4, :], mod_ref[0, 4:5, :]).astype(BF16)
    p = jnp.dot(u, w_ref[...], preferred_element_type=F32)
    z = p[:, :D_MIX]
    xbc = p[:, D_MIX:D_MIX + SSD_XBC]
    dtp = p[:, D_MIX + SSD_XBC:]
    rows = _iota((L, 1), 0)
    prev = jnp.where(rows == 0, 0.0, pltpu.roll(xbc, 1, 0))
    nxt = jnp.where(rows == L - 1, 0.0, pltpu.roll(xbc, L - 1, 0))
    xc = cb_ref[...] + prev * cw_ref[0:1, :] + xbc * cw_ref[1:2, :] + nxt * cw_ref[2:3, :]
    xc = _silu(xc)
    xs = xc[:, :D_MIX]
    bm = xc[:, D_MIX:D_MIX + SSD_G * SSD_N]
    cm = xc[:, D_MIX + SSD_G * SSD_N:]
    dt = _softplus(dtp + dtb_ref[...])
    adt = dt * (-jnp.exp(alog_ref[...]))

    tril = _tri01(C, True)
    triu = _tri01(C, False)
    tril_b = jnp.where(tril, 1.0, 0.0).astype(BF16)
    triu_b = jnp.where(triu, 1.0, 0.0).astype(BF16)
    ef = ef_ref[...]
    eb = eb_ref[...]

    cs, csT, rcs, rcsT, dtT, loc_f, loc_b = [], [], [], [], [], [], []
    for c in range(nc):
        a_c = adt[c * C:(c + 1) * C]
        a_cT = a_c.T
        cs.append(_sel_dot(tril_b, a_c))
        rcs.append(_sel_dot(triu_b, a_c))
        csT.append(_dot_sel(a_cT, triu_b))
        rcsT.append(_dot_sel(a_cT, tril_b))
        dtT.append(dt[c * C:(c + 1) * C].T)

    need_states = want_final or nc > 1
    if need_states:
        for c in range(nc):
            xs_c = xs[c * C:(c + 1) * C]
            dt_c = dt[c * C:(c + 1) * C]
            wf = jnp.exp(cs[c][C - 1:C, :] - cs[c]) * dt_c
            wb = jnp.exp(rcs[c][0:1, :] - rcs[c]) * dt_c
            xwf = (xs_c * _dot_sel(wf, ef)).T
            xwb = (xs_c * _dot_sel(wb, eb)).T
            lf, lb = [], []
            for h in range(SSD_H):
                g = h // (SSD_H // SSD_G)
                bm_g = bm[c * C:(c + 1) * C, g * SSD_N:(g + 1) * SSD_N]
                lf.append(_dot(xwf[h * SSD_P:(h + 1) * SSD_P, :], bm_g))
                lb.append(_dot(xwb[h * SSD_P:(h + 1) * SSD_P, :], bm_g))
            loc_f.append(lf)
            loc_b.append(lb)

    zero_s = jnp.zeros((SSD_P, SSD_N), F32)
    sin_f = [[None] * SSD_H for _ in range(nc + 1)]
    sin_b = [[None] * SSD_H for _ in range(nc + 1)]
    for h in range(SSD_H):
        sin_f[0][h] = s0_ref[0, 0, h] if has_init else zero_s
        sin_b[nc][h] = s0_ref[0, 1, h] if has_init else zero_s
    if need_states:
        for c in range(nc):
            dec = jnp.exp(cs[c][C - 1:C, :])
            for h in range(SSD_H):
                sin_f[c + 1][h] = dec[:, h:h + 1] * sin_f[c][h] + loc_f[c][h]
        for c in range(nc - 1, -1, -1):
            dec = jnp.exp(rcs[c][0:1, :])
            for h in range(SSD_H):
                sin_b[c][h] = dec[:, SSD_H + h:SSD_H + h + 1] * sin_b[c + 1][h] + loc_b[c][h]

    for c in range(nc):
        sl = slice(c * C, (c + 1) * C)
        ecs = jnp.exp(cs[c])
        ercs = jnp.exp(rcs[c])
        for h in range(SSD_H):
            g = h // (SSD_H // SSD_G)
            cm_g = cm[sl, g * SSD_N:(g + 1) * SSD_N]
            bm_g = bm[sl, g * SSD_N:(g + 1) * SSD_N]
            cb = _dot_nt(cm_g, bm_g)
            lf = jnp.exp(jnp.where(tril, cs[c][:, h:h + 1] - csT[c][h:h + 1, :], -jnp.inf))
            lb = jnp.exp(jnp.where(triu, rcs[c][:, SSD_H + h:SSD_H + h + 1]
                                   - rcsT[c][SSD_H + h:SSD_H + h + 1, :], -jnp.inf))
            m = cb * (lf * dtT[c][h:h + 1, :] + lb * dtT[c][SSD_H + h:SSD_H + h + 1, :])
            y_h = _dot(m, xs[sl, h * SSD_P:(h + 1) * SSD_P])
            if has_init or nc > 1:
                y_h = y_h + ecs[:, h:h + 1] * _dot_nt(cm_g, sin_f[c][h])
                y_h = y_h + ercs[:, SSD_H + h:SSD_H + h + 1] * _dot_nt(cm_g, sin_b[c + 1][h])
            yacc[sl, h * SSD_P:(h + 1) * SSD_P] = y_h

    y = yacc[...] + xs * (d_ref[0:1, :] + d_ref[1:2, :])
    y = _rmsnorm(y * _silu(z), nw_ref[...])
    y_ref[...] = y.astype(BF16)
    if want_final:
        for h in range(SSD_H):
            sfin_ref[0, 0, h] = sin_f[nc][h]
            sfin_ref[0, 1, h] = sin_b[0][h]


def _ssd_call(x, mod, ng, w, cw, cb, dtb, alog, dexp, nw, ef, eb, s0, *, ctx):
    L = SEQ if ctx else DEC_SEQ
    nseq = BATCH if ctx else DEC_BATCH
    blk0 = 0 if ctx else N_CTX_TOK // L
    mod_of = (lambda i: (i, 0, 0)) if ctx else (lambda i: (N_CTX_BLK + LAT_BLK_PER_SEQ * i, 0, 0))
    in_specs = [pl.BlockSpec((L, D_MODEL), lambda i: (blk0 + i, 0)),
                pl.BlockSpec((1, N_MOD, D_MODEL), mod_of),
                _const_spec((1, D_MODEL)), _const_spec((D_MODEL, SSD_W)),
                _const_spec((3, SSD_XBC)), _const_spec((1, SSD_XBC)),
                _const_spec((1, LANE)), _const_spec((1, LANE)),
                _const_spec((2, D_MIX)), _const_spec((1, D_MIX)),
                _const_spec((LANE, D_MIX)), _const_spec((LANE, D_MIX))]
    args = [x, mod, ng, w, cw, cb, dtb, alog, dexp, nw, ef, eb]
    st_spec = pl.BlockSpec((1, 2, SSD_H, SSD_P, SSD_N), lambda i: (i, 0, 0, 0, 0))
    out_specs = [pl.BlockSpec((L, D_MIX), lambda i: (i, 0))]
    out_shape = [jax.ShapeDtypeStruct((nseq * L, D_MIX), BF16)]
    if ctx:
        out_specs.append(st_spec)
        out_shape.append(jax.ShapeDtypeStruct((nseq, 2, SSD_H, SSD_P, SSD_N), F32))
    else:
        in_specs.append(st_spec)
        args.append(s0)
    return pl.pallas_call(
        functools.partial(_ssd_kernel, L=L, has_init=not ctx, want_final=ctx),
        grid=(nseq,),
        in_specs=in_specs,
        out_specs=out_specs,
        out_shape=out_shape,
        scratch_shapes=[pltpu.VMEM((L, D_MIX), F32)],
        compiler_params=pltpu.CompilerParams(dimension_semantics=("arbitrary",),
                                             vmem_limit_bytes=VMEM_LIMIT),
        name="ssd_ctx" if ctx else "ssd_lat",
    )(*args)


def _gla_kernel(*refs, L, has_init, want_final):
    it = iter(refs)
    x_ref, mod_ref, ng_ref, w_ref = next(it), next(it), next(it), next(it)
    gkw_ref, gkb_ref, nw_ref = next(it), next(it), next(it)
    s0_ref = next(it) if has_init else None
    y_ref = next(it)
    sfin_ref = next(it) if want_final else None
    oacc = next(it)

    C = GLA_CHUNK
    B = RB
    cpb = B // C
    nb = L // B
    x = x_ref[...]
    u = _norm_mod(x, ng_ref[...], mod_ref[0, 3:4, :], mod_ref[0, 4:5, :]).astype(BF16)
    p = jnp.dot(u, w_ref[...], preferred_element_type=F32)
    q = p[:, :GLA_QK] * (GLA_DK ** -0.5)
    k = p[:, GLA_QK:2 * GLA_QK]
    v = p[:, 2 * GLA_QK:2 * GLA_QK + D_MIX]
    gg = p[:, 2 * GLA_QK + D_MIX:2 * GLA_QK + 2 * D_MIX]
    lr = p[:, 2 * GLA_QK + 2 * D_MIX:].astype(BF16)
    lg = []
    for d in range(2):
        pre = jnp.dot(lr, gkw_ref[d], preferred_element_type=F32) + gkb_ref[d:d + 1, :]
        lg.append(-_softplus(-pre) / GLA_GATE_NORM)

    low = _tri01(B, True, C)
    upp = _tri01(B, False, C)
    low_b = jnp.where(low, 1.0, 0.0).astype(BF16)
    upp_b = jnp.where(upp, 1.0, 0.0).astype(BF16)
    rowi = _iota((B, 1), 0)

    zero_s = jnp.zeros((LANE, GLA_DV), F32)
    if has_init:
        pad = jnp.zeros((LANE - GLA_DK, GLA_DV), F32)
        s_f = [jnp.concatenate([s0_ref[0, 0, h], pad], axis=0) for h in range(GLA_H)]
        s_b0 = [jnp.concatenate([s0_ref[0, 1, h], pad], axis=0) for h in range(GLA_H)]
    else:
        s_f = [zero_s] * GLA_H
        s_b0 = [zero_s] * GLA_H

    blocks = []
    for b in range(nb):
        sl = slice(b * B, (b + 1) * B)
        bf = _sel_dot(low_b, lg[0][sl])
        rb = _sel_dot(upp_b, lg[1][sl])
        q_b, k_b = q[sl], k[sl]
        blocks.append(dict(sl=sl, bf=bf, rb=rb, bfT=bf.T, rbT=rb.T,
                           qf=q_b * jnp.exp(bf), kf=k_b * jnp.exp(-bf),
                           qb=q_b * jnp.exp(rb), kb=k_b * jnp.exp(-rb), k=k_b))

    for blk in blocks:
        sl = blk["sl"]
        for h in range(GLA_H):
            hs = slice(h * LANE, (h + 1) * LANE)
            a = jnp.where(low, _dot_nt(blk["qf"][:, hs], blk["kf"][:, hs]), 0.0)
            a = a + jnp.where(upp, _dot_nt(blk["qb"][:, hs], blk["kb"][:, hs]), 0.0)
            oacc[sl, h * GLA_DV:(h + 1) * GLA_DV] = _dot(a, v[sl, h * GLA_DV:(h + 1) * GLA_DV])

    for blk in blocks:
        sl = blk["sl"]
        r0 = sl.start
        for c in range(cpb):
            last = blk["bf"][(c + 1) * C - 1:(c + 1) * C, :]
            kdecT = (blk["k"] * jnp.exp(last - blk["bf"])).T
            in_c = (rowi >= c * C) & (rowi < (c + 1) * C)
            rs = slice(r0 + c * C, r0 + (c + 1) * C)
            for h in range(GLA_H):
                hs = slice(h * LANE, (h + 1) * LANE)
                vs = slice(h * GLA_DV, (h + 1) * GLA_DV)
                oacc[rs, vs] = oacc[rs, vs] + _dot(blk["qf"][c * C:(c + 1) * C, hs], s_f[h])
                kv = _dot(kdecT[hs, :], jnp.where(in_c, v[sl, vs], 0.0))
                dec = jnp.exp(blk["bfT"][hs, (c + 1) * C - 1:(c + 1) * C])
                s_f[h] = dec * s_f[h] + kv

    s_b = s_b0
    for blk in reversed(blocks):
        sl = blk["sl"]
        r0 = sl.start
        for c in range(cpb - 1, -1, -1):
            first = blk["rb"][c * C:c * C + 1, :]
            kdecT = (blk["k"] * jnp.exp(first - blk["rb"])).T
            in_c = (rowi >= c * C) & (rowi < (c + 1) * C)
            rs = slice(r0 + c * C, r0 + (c + 1) * C)
            for h in range(GLA_H):
                hs = slice(h * LANE, (h + 1) * LANE)
                vs = slice(h * GLA_DV, (h + 1) * GLA_DV)
                oacc[rs, vs] = oacc[rs, vs] + _dot(blk["qb"][c * C:(c + 1) * C, hs], s_b[h])
                kv = _dot(kdecT[hs, :], jnp.where(in_c, v[sl, vs], 0.0))
                dec = jnp.exp(blk["rbT"][hs, c * C:c * C + 1])
                s_b[h] = dec * s_b[h] + kv

    for h in range(GLA_H):
        vs = slice(h * GLA_DV, (h + 1) * GLA_DV)
        o_h = _rmsnorm(oacc[:, vs], nw_ref[...])
        y_ref[:, vs] = (o_h * _silu(gg[:, vs])).astype(BF16)
        if want_final:
            sfin_ref[0, 0, h] = s_f[h][:GLA_DK, :]
            sfin_ref[0, 1, h] = s_b[h][:GLA_DK, :]


def _gla_call(x, mod, ng, w, gkw, gkb, nw, s0, *, ctx):
    L = SEQ if ctx else DEC_SEQ
    nseq = BATCH if ctx else DEC_BATCH
    blk0 = 0 if ctx else N_CTX_TOK // L
    mod_of = (lambda i: (i, 0, 0)) if ctx else (lambda i: (N_CTX_BLK + LAT_BLK_PER_SEQ * i, 0, 0))
    in_specs = [pl.BlockSpec((L, D_MODEL), lambda i: (blk0 + i, 0)),
                pl.BlockSpec((1, N_MOD, D_MODEL), mod_of),
                _const_spec((1, D_MODEL)), _const_spec((D_MODEL, GLA_W)),
                _const_spec((2, LANE, GLA_QK)), _const_spec((2, GLA_QK)),
                _const_spec((1, GLA_DV))]
    args = [x, mod, ng, w, gkw, gkb, nw]
    st_spec = pl.BlockSpec((1, 2, GLA_H, GLA_DK, GLA_DV), lambda i: (i, 0, 0, 0, 0))
    out_specs = [pl.BlockSpec((L, D_MIX), lambda i: (i, 0))]
    out_shape = [jax.ShapeDtypeStruct((nseq * L, D_MIX), BF16)]
    if ctx:
        out_specs.append(st_spec)
        out_shape.append(jax.ShapeDtypeStruct((nseq, 2, GLA_H, GLA_DK, GLA_DV), F32))
    else:
        in_specs.append(st_spec)
        args.append(s0)
    return pl.pallas_call(
        functools.partial(_gla_kernel, L=L, has_init=not ctx, want_final=ctx),
        grid=(nseq,),
        in_specs=in_specs,
        out_specs=out_specs,
        out_shape=out_shape,
        scratch_shapes=[pltpu.VMEM((L, D_MIX), F32)],
        compiler_params=pltpu.CompilerParams(dimension_semantics=("arbitrary",),
                                             vmem_limit_bytes=VMEM_LIMIT),
        name="gla_ctx" if ctx else "gla_lat",
    )(*args)


def _rw_prep_kernel(x_ref, xp_ref, xn_ref, mod_ref, ng_ref, w_ref, mu_ref, a0_ref, a2_ref, g2_ref,
                    kkw_ref, ka_ref, rk_ref, w0_ref, w2_ref, bd_ref,
                    r_ref, wf_ref, wb_ref, k_ref, v_ref, kk_ref, nkka_ref, g_ref, bonus_ref):
    i = pl.program_id(0)
    j = (i - N_CTX_BLK) % LAT_BLK_PER_SEQ
    is_first = (i < N_CTX_BLK) | (j == 0)
    is_last = (i < N_CTX_BLK) | (j == LAT_BLK_PER_SEQ - 1)
    ng = ng_ref[...]
    shift, scale = mod_ref[0, 3:4, :], mod_ref[0, 4:5, :]
    w = w_ref[...]

    def proj(xr):
        return jnp.dot(_norm_mod(xr, ng, shift, scale).astype(BF16), w, preferred_element_type=F32)

    p = proj(x_ref[...])
    p_prev = jnp.where(is_first, 0.0, proj(xp_ref[...])[SUB - 1:SUB, :])
    p_next = jnp.where(is_last, 0.0, proj(xn_ref[...])[0:1, :])
    rows = _iota((RB, 1), 0)
    prev = jnp.where(rows == 0, p_prev, pltpu.roll(p, 1, 0))
    nxt = jnp.where(rows == RB - 1, p_next, pltpu.roll(p, RB - 1, 0))
    p = p + (0.5 * (prev + nxt) - p) * mu_ref[...]

    r = p[:, :D_MIX]
    k = p[:, D_MIX:2 * D_MIX]
    v = p[:, 2 * D_MIX:3 * D_MIX]
    wlr = p[:, 3 * D_MIX:3 * D_MIX + LANE]
    glr = p[:, 3 * D_MIX + LANE:3 * D_MIX + 2 * LANE]
    alr = p[:, 3 * D_MIX + 2 * LANE:]
    bd = bd_ref[...]
    a = _sigmoid(a0_ref[...] + _dot(alr, a2_ref[...]))
    g = _dot(_sigmoid(glr), g2_ref[...])
    kk = k * kkw_ref[...]
    kk = kk / jnp.maximum(jnp.sqrt(_dot_sel(kk * kk, bd)), 1e-12)
    k = k * (1.0 + (a - 1.0) * ka_ref[...])
    tw = jnp.tanh(wlr).astype(BF16)
    for d, o_ref in ((0, wf_ref), (1, wb_ref)):
        pre = w0_ref[d:d + 1, :] + jnp.dot(tw, w2_ref[d], preferred_element_type=F32)
        wl = -_softplus(-pre) - 0.5
        o_ref[0] = jnp.exp(-jnp.exp(wl)).T
    r_ref[0] = r.T
    k_ref[0] = k.T
    v_ref[0] = v.T
    kk_ref[0] = kk.T
    nkka_ref[0] = (-(kk * a)).T
    g_ref[...] = g
    bonus_ref[...] = _dot_sel(r * k * rk_ref[...], bd) * v


def _rw_prep_call(x, mod, ng, w, mu, a0, a2, g2, kkw, ka, rk, w0, w2, bd):
    hb = RB // SUB
    row_spec = pl.BlockSpec((RB, D_MODEL), lambda i: (i, 0))
    out_spec = pl.BlockSpec((RB, D_MIX), lambda i: (i, 0))
    in_specs = [row_spec,
                pl.BlockSpec((SUB, D_MODEL), lambda i: (jnp.maximum(i * hb - 1, 0), 0)),
                pl.BlockSpec((SUB, D_MODEL), lambda i: (jnp.minimum((i + 1) * hb, N_TOK // SUB - 1), 0)),
                pl.BlockSpec((1, N_MOD, D_MODEL), lambda i: (i, 0, 0)),
                _const_spec((1, D_MODEL)), _const_spec((D_MODEL, RW_W)), _const_spec((1, RW_W)),
                _const_spec((1, D_MIX)), _const_spec((LANE, D_MIX)), _const_spec((LANE, D_MIX)),
                _const_spec((1, D_MIX)), _const_spec((1, D_MIX)), _const_spec((1, D_MIX)),
                _const_spec((2, D_MIX)), _const_spec((2, LANE, D_MIX)), _const_spec((D_MIX, D_MIX))]
    t_spec = pl.BlockSpec((1, D_MIX, RB), lambda i: (i, 0, 0))
    t_shape = jax.ShapeDtypeStruct((N_BLK, D_MIX, RB), F32)
    tok_shape = jax.ShapeDtypeStruct((N_TOK, D_MIX), F32)
    return pl.pallas_call(
        _rw_prep_kernel,
        grid=(N_BLK,),
        in_specs=in_specs,
        out_specs=[t_spec] * 7 + [out_spec] * 2,
        out_shape=[t_shape] * 7 + [tok_shape] * 2,
        compiler_params=pltpu.CompilerParams(dimension_semantics=("arbitrary",),
                                             vmem_limit_bytes=VMEM_LIMIT),
        name="rw_prep",
    )(x, x, x, mod, ng, w, mu, a0, a2, g2, kkw, ka, rk, w0, w2, bd)


RW_VQ = LANE // (DEC_BATCH * RW_H)
RW_NV_LAT = RW_N // RW_VQ


def _rl_rows(ref, nl, lat):
    x = ref[:, 0, :, nl, :] if lat else ref[:, :, nl, :]
    return x.reshape(-1, RB)


def _rl_k_kernel(r_ref, k_ref, kk_ref, a_ref, wf_ref, wb_ref, ro, ko, kko, ao, wo, *, lat):
    def lanes(ref, nl):
        x = _rl_rows(ref, nl, lat)
        if lat:
            x = jnp.concatenate([x] * RW_VQ, axis=0)
        return x.T

    for nl in range(SUB):
        ro[nl] = lanes(r_ref, nl)
        ko[nl] = lanes(k_ref, nl)
        kko[nl] = lanes(kk_ref, nl)
        ao[nl] = lanes(a_ref, nl)
        wo[0, nl] = lanes(wf_ref, nl)
        wo[1, nl] = lanes(wb_ref, nl)


def _rl_k_call(r, k, kk, nkka, wf, wb, *, lat):
    nb = RW_N // SUB
    if lat:
        L = DEC_SEQ
        view = (N_BLK // LAT_BLK_PER_SEQ, LAT_BLK_PER_SEQ, RW_H, RW_N, RB)
        grid = (LAT_BLK_PER_SEQ, nb)
        ispec = pl.BlockSpec((DEC_BATCH, 1, RW_H, SUB, RB), lambda j, n: (1, j, 0, n, 0))
        ospec = pl.BlockSpec((SUB, RB, LANE), lambda j, n: (n, j, 0))
        wspec = pl.BlockSpec((2, SUB, RB, LANE), lambda j, n: (0, n, j, 0))
    else:
        L = SEQ
        view = (N_BLK, RW_H, RW_N, RB)
        grid = (nb,)
        ispec = pl.BlockSpec((BATCH, RW_H, SUB, RB), lambda n: (0, 0, n, 0))
        ospec = pl.BlockSpec((SUB, RB, LANE), lambda n: (n, 0, 0))
        wspec = pl.BlockSpec((2, SUB, RB, LANE), lambda n: (0, n, 0, 0))
    kshape = jax.ShapeDtypeStruct((RW_N, L, LANE), F32)
    return pl.pallas_call(
        functools.partial(_rl_k_kernel, lat=lat),
        grid=grid,
        in_specs=[ispec] * 6,
        out_specs=[ospec] * 4 + [wspec],
        out_shape=[kshape] * 4 + [jax.ShapeDtypeStruct((2, RW_N, L, LANE), F32)],
        compiler_params=pltpu.CompilerParams(dimension_semantics=("arbitrary",) * len(grid),
                                             vmem_limit_bytes=VMEM_LIMIT),
        name="rl_k_lat" if lat else "rl_k_ctx",
    )(*[a.reshape(view) for a in (r, k, kk, nkka, wf, wb)])


def _rl_v_kernel(*refs, lat):
    o_ref = refs[-1]
    for nl in range(SUB):
        x = jnp.concatenate([_rl_rows(ref, nl, lat) for ref in refs[:-1]], axis=0) if lat \
            else _rl_rows(refs[0], nl, lat)
        o_ref[:, nl, :] = x.T


def _rl_v_call(v, *, lat):
    if lat:
        view = (N_BLK // LAT_BLK_PER_SEQ, LAT_BLK_PER_SEQ, RW_H, RW_N, RB)
        nvb = RW_NV_LAT // SUB
        grid = (LAT_BLK_PER_SEQ, nvb)
        ispecs = [pl.BlockSpec((DEC_BATCH, 1, RW_H, SUB, RB),
                               functools.partial(lambda j, n, q: (1, j, 0, q * nvb + n, 0), q=q))
                  for q in range(RW_VQ)]
        ospec = pl.BlockSpec((RB, SUB, LANE), lambda j, n: (j, n, 0))
        oshape = jax.ShapeDtypeStruct((DEC_SEQ, RW_NV_LAT, LANE), F32)
    else:
        view = (N_BLK, RW_H, RW_N, RB)
        grid = (RW_N // SUB,)
        ispecs = [pl.BlockSpec((BATCH, RW_H, SUB, RB), lambda n: (0, 0, n, 0))]
        ospec = pl.BlockSpec((RB, SUB, LANE), lambda n: (0, n, 0))
        oshape = jax.ShapeDtypeStruct((SEQ, RW_N, LANE), F32)
    vv = v.reshape(view)
    return pl.pallas_call(
        functools.partial(_rl_v_kernel, lat=lat),
        grid=grid,
        in_specs=ispecs,
        out_specs=ospec,
        out_shape=oshape,
        compiler_params=pltpu.CompilerParams(dimension_semantics=("arbitrary",) * len(grid),
                                             vmem_limit_bytes=VMEM_LIMIT),
        name="rl_v_lat" if lat else "rl_v_ctx",
    )(*([vv] * len(ispecs)))


def _rl_out_kernel(o_ref, ot_ref, *, lat):
    for nl in range(SUB):
        x = (o_ref[0, :, nl, :] + o_ref[1, :, nl, :]).T
        if lat:
            rows = DEC_BATCH * RW_H
            for q in range(RW_VQ):
                ot_ref[:, 0, :, q, nl, :] = x[q * rows:(q + 1) * rows].reshape(DEC_BATCH, RW_H, RB)
        else:
            ot_ref[:, :, nl, :] = x.reshape(BATCH, RW_H, RB)


def _rl_out_call(o, *, lat):
    if lat:
        grid = (LAT_BLK_PER_SEQ, RW_NV_LAT // SUB)
        ispec = pl.BlockSpec((2, RB, SUB, LANE), lambda j, n: (0, j, n, 0))
        ospec = pl.BlockSpec((DEC_BATCH, 1, RW_H, RW_VQ, SUB, RB), lambda j, n: (0, j, 0, 0, n, 0))
        oshape = (DEC_BATCH, LAT_BLK_PER_SEQ, RW_H, RW_VQ, RW_NV_LAT, RB)
    else:
        grid = (RW_N // SUB,)
        ispec = pl.BlockSpec((2, RB, SUB, LANE), lambda n: (0, 0, n, 0))
        ospec = pl.BlockSpec((BATCH, RW_H, SUB, RB), lambda n: (0, 0, n, 0))
        oshape = (BATCH, RW_H, RW_N, RB)
    out = pl.pallas_call(
        functools.partial(_rl_out_kernel, lat=lat),
        grid=grid,
        in_specs=[ispec],
        out_specs=ospec,
        out_shape=jax.ShapeDtypeStruct(oshape, F32),
        compiler_params=pltpu.CompilerParams(dimension_semantics=("arbitrary",) * len(grid),
                                             vmem_limit_bytes=VMEM_LIMIT),
        name="rl_out_lat" if lat else "rl_out_ctx",
    )(o)
    return out.reshape(N_CTX_BLK, D_MIX, RB)


def _rw_scan_kernel(*refs, vb, has_init, want_final):
    it = iter(refs)
    r_ref, w_ref, k_ref, v_ref, kk_ref, nkka_ref = (next(it), next(it), next(it), next(it),
                                                    next(it), next(it))
    s0_ref = next(it) if has_init else None
    o_ref = next(it)
    sfin_ref = next(it) if want_final else None
    s_scr = next(it)
    g = pl.program_id(0)
    tb = pl.program_id(1)

    @pl.when(tb == 0)
    def _():
        if has_init:
            s_scr[...] = s0_ref[0]
        else:
            s_scr[...] = jnp.zeros(s_scr.shape, F32)

    def bcast(ref, t, kx):
        return jnp.broadcast_to(ref[kx, pl.ds(t, 1), :], (SUB, LANE))

    def step(i, carry):
        t = jnp.where(g == 0, i, SCAN_TB - 1 - i)
        skk = [jnp.zeros((SUB, LANE), F32) for _ in range(vb)]
        for kx in range(RW_N):
            kkb = bcast(kk_ref, t, kx)
            for j in range(vb):
                skk[j] = skk[j] + s_scr[kx, j * SUB:(j + 1) * SUB, :] * kkb
        vv = [v_ref[t, j * SUB:(j + 1) * SUB, :] for j in range(vb)]
        oacc = [jnp.zeros((SUB, LANE), F32) for _ in range(vb)]
        for kx in range(RW_N):
            wb = jnp.broadcast_to(w_ref[0, kx, pl.ds(t, 1), :], (SUB, LANE))
            ab = bcast(nkka_ref, t, kx)
            kb = bcast(k_ref, t, kx)
            rb = bcast(r_ref, t, kx)
            for j in range(vb):
                s = s_scr[kx, j * SUB:(j + 1) * SUB, :] * wb + skk[j] * ab + vv[j] * kb
                s_scr[kx, j * SUB:(j + 1) * SUB, :] = s
                oacc[j] = oacc[j] + s * rb
        for j in range(vb):
            o_ref[0, t, j * SUB:(j + 1) * SUB, :] = oacc[j]
        return carry

    lax.fori_loop(0, SCAN_TB, step, 0)

    if want_final:
        @pl.when(tb == pl.num_programs(1) - 1)
        def _():
            sfin_ref[0] = s_scr[...]


def _rw_scan_call(r, w, k, v, kk, nkka, s0, *, want_final):
    L = r.shape[1]
    nv = v.shape[1]
    ntb = L // SCAN_TB
    has_init = s0 is not None
    tmap = lambda g, t: t + g * (ntb - 1 - 2 * t)
    kspec = pl.BlockSpec((RW_N, SCAN_TB, LANE), lambda g, t: (0, tmap(g, t), 0))
    wspec = pl.BlockSpec((1, RW_N, SCAN_TB, LANE), lambda g, t: (g, 0, tmap(g, t), 0))
    vspec = pl.BlockSpec((SCAN_TB, nv, LANE), lambda g, t: (tmap(g, t), 0, 0))
    ospec = pl.BlockSpec((1, SCAN_TB, nv, LANE), lambda g, t: (g, tmap(g, t), 0, 0))
    sspec = pl.BlockSpec((1, RW_N, nv, LANE), lambda g, t: (g, 0, 0, 0))
    in_specs = [kspec, wspec, kspec, vspec, kspec, kspec]
    args = [r, w, k, v, kk, nkka]
    if has_init:
        in_specs.append(sspec)
        args.append(s0)
    out_specs = [ospec]
    out_shape = [jax.ShapeDtypeStruct((2, L, nv, LANE), F32)]
    if want_final:
        out_specs.append(sspec)
        out_shape.append(jax.ShapeDtypeStruct((2, RW_N, nv, LANE), F32))
    return pl.pallas_call(
        functools.partial(_rw_scan_kernel, vb=nv // SUB, has_init=has_init, want_final=want_final),
        grid=(2, ntb),
        in_specs=in_specs,
        out_specs=out_specs,
        out_shape=out_shape,
        scratch_shapes=[pltpu.VMEM((RW_N, nv, LANE), F32)],
        compiler_params=pltpu.CompilerParams(dimension_semantics=("arbitrary", "arbitrary"),
                                             vmem_limit_bytes=VMEM_LIMIT),
        name="rw_scan",
    )(*args)


def _merge_kernel(x_ref, mod_ref, ng_ref, wgate_ref, yssd_c, yssd_l, ygla_c, ygla_l, orw_c, orw_l,
                  bonus_ref, g_ref, lnw_ref, lnb_ref, bd_ref, wso_ref, wgo_ref, wro_ref, wout_ref,
                  o_ref):
    is_ctx = pl.program_id(0) < N_CTX_BLK
    x = x_ref[...]
    u = _norm_mod(x, ng_ref[...], mod_ref[0, 3:4, :], mod_ref[0, 4:5, :]).astype(BF16)
    bd = bd_ref[...]
    yssd = jnp.where(is_ctx, yssd_c[...], yssd_l[...])
    ygla = jnp.where(is_ctx, ygla_c[...], ygla_l[...])
    o = jnp.where(is_ctx, orw_c[0], orw_l[0]).T
    mu = _dot_sel(o, bd) * (1.0 / RW_N)
    oc = o - mu
    var = _dot_sel(oc * oc, bd) * (1.0 / RW_N)
    o = oc * lax.rsqrt(var + RW_GN_EPS) * lnw_ref[...] + lnb_ref[...]
    y_rw = ((o + bonus_ref[...]) * g_ref[...]).astype(BF16)
    merged = jnp.zeros((RB, D_MODEL), F32)
    for b, (y, wo_ref) in enumerate(((yssd, wso_ref), (ygla, wgo_ref), (y_rw, wro_ref))):
        gate = _sigmoid(jnp.dot(u, wgate_ref[:, b * D_MODEL:(b + 1) * D_MODEL],
                                preferred_element_type=F32))
        merged = merged + gate * jnp.dot(y, wo_ref[...], preferred_element_type=F32)
    m = jnp.dot(merged.astype(BF16), wout_ref[...], preferred_element_type=F32)
    o_ref[...] = x + mod_ref[0, 5:6, :] * m


def _merge_call(x, mod, ng, wgate, yssd_c, yssd_l, ygla_c, ygla_l, orw_c, orw_l, bonus, g, lnw, lnb,
                bd, wso, wgo, wro, wout):
    row_spec = pl.BlockSpec((RB, D_MODEL), lambda i: (i, 0))
    mix_spec = pl.BlockSpec((RB, D_MIX), lambda i: (i, 0))
    mix_c = pl.BlockSpec((RB, D_MIX), lambda i: (_ctx_idx(i), 0))
    mix_l = pl.BlockSpec((RB, D_MIX), lambda i: (_lat_idx(i), 0))
    t_c = pl.BlockSpec((1, D_MIX, RB), lambda i: (_ctx_idx(i), 0, 0))
    t_l = pl.BlockSpec((1, D_MIX, RB), lambda i: (_lat_idx(i), 0, 0))
    in_specs = [row_spec, pl.BlockSpec((1, N_MOD, D_MODEL), lambda i: (i, 0, 0)),
                _const_spec((1, D_MODEL)), _const_spec((D_MODEL, 3 * D_MODEL)),
                mix_c, mix_l, mix_c, mix_l, t_c, t_l, mix_spec, mix_spec,
                _const_spec((1, D_MIX)), _const_spec((1, D_MIX)), _const_spec((D_MIX, D_MIX)),
                _const_spec((D_MIX, D_MODEL)), _const_spec((D_MIX, D_MODEL)),
                _const_spec((D_MIX, D_MODEL)), _const_spec((D_MODEL, D_MODEL))]
    return pl.pallas_call(
        _merge_kernel,
        grid=(N_BLK,),
        in_specs=in_specs,
        out_specs=row_spec,
        out_shape=jax.ShapeDtypeStruct((N_TOK, D_MODEL), F32),
        compiler_params=pltpu.CompilerParams(dimension_semantics=("arbitrary",),
                                             vmem_limit_bytes=VMEM_LIMIT),
        name="merge",
    )(x, mod, ng, wgate, yssd_c, yssd_l, ygla_c, ygla_l, orw_c, orw_l, bonus, g, lnw, lnb, bd,
      wso, wgo, wro, wout)


def _grid_pos_embed(rows, cols, dim):
    quarter = dim // 4
    omega = 1.0 / (10000.0 ** (jnp.arange(quarter, dtype=F32) / quarter))
    er = jnp.arange(rows, dtype=F32)[:, None] * omega
    ec = jnp.arange(cols, dtype=F32)[:, None] * omega
    er = jnp.concatenate([jnp.sin(er), jnp.cos(er)], axis=-1)
    ec = jnp.concatenate([jnp.sin(ec), jnp.cos(ec)], axis=-1)
    emb = jnp.concatenate([jnp.broadcast_to(er[:, None], (rows, cols, dim // 2)),
                           jnp.broadcast_to(ec[None], (rows, cols, dim // 2))], axis=-1)
    return emb.reshape(rows * cols, dim)


def _pad_cols(a, n):
    return jnp.pad(a, [(0, 0)] * (a.ndim - 1) + [(0, n - a.shape[-1])])


def _head_pad(a):
    a = a.reshape(a.shape[:-1] + (GLA_H, GLA_DK))
    return _pad_cols(a, LANE).reshape(a.shape[:-2] + (GLA_QK,))


def _rows_at(a, off, n):
    return jnp.pad(a, ((off, n - off - a.shape[0]), (0, 0)))


def _block_diag_ones(n, blk):
    i = np.arange(n)
    return jnp.asarray((i[:, None] // blk) == (i[None, :] // blk), BF16)


def _expand01(row0):
    m = np.zeros((LANE, D_MIX), np.float32)
    for h in range(SSD_H):
        m[row0 + h, h * SSD_P:(h + 1) * SSD_P] = 1.0
    return jnp.asarray(m, BF16)


def kernel(x_prompt, x_sample, state_ssd, state_gla, state_rwkv, c, c_ctx, norm_g, w_ada, b_ada,
           ffn_gate, ffn_up, ffn_down, w_in, ssd_conv_w, ssd_conv_b, ssd_dt_bias, ssd_A_log, ssd_D,
           ssd_norm, w_ssd_o, gla_gk_w, gla_gk_b, gla_norm, w_gla_o, rw_mu, rw_w0, rw_w2, rw_a0,
           rw_a2, rw_g2, rw_kk, rw_ka, rw_rk, rw_ln_w, rw_ln_b, w_rw_o, w_out, final_norm):
    pos = _grid_pos_embed(DEC_SEQ // GRID_W, GRID_W, D_MODEL)
    x = (x_prompt.reshape(N_CTX_TOK, D_MODEL), x_sample.reshape(-1, D_MODEL))
    s0_rw = state_rwkv.reshape(DEC_BATCH, DEPTH, 2, RW_H, RW_VQ, RW_NV_LAT, RW_N).transpose(
        1, 2, 6, 5, 4, 0, 3).reshape(DEPTH, 2, RW_N, RW_NV_LAT, LANE)

    cond8 = jnp.concatenate([c_ctx[None], c, jnp.zeros((SUB - 1 - DEC_BATCH, D_MODEL), F32)])
    ada = _ada_call(cond8, w_ada, b_ada)
    cond_of_blk = np.concatenate([np.zeros(N_CTX_BLK, np.int32),
                                  1 + np.arange(N_BLK - N_CTX_BLK, dtype=np.int32) // LAT_BLK_PER_SEQ])

    bd = _block_diag_ones(D_MIX, RW_N)
    ef, eb = _expand01(0), _expand01(SSD_H)
    o_ssd = D_MIX + SSD_XBC + 2 * SSD_H
    o_gla = o_ssd + 2 * GLA_H * GLA_DK + 2 * D_MIX + 2 * GLA_LR
    o_rw = o_gla + 3 * D_MIX + 2 * RW_LW + RW_LA + RW_LG

    new_ssd, new_gla, new_rw = [], [], []
    for l in range(DEPTH):
        mod = ada[l][cond_of_blk].reshape(N_BLK, N_MOD, D_MODEL)
        ng = norm_g[l]
        wi = w_in[l]
        w_ssd = _pad_cols(wi[:, :o_ssd], SSD_W).astype(BF16)
        wg_ = wi[:, o_ssd:o_gla]
        w_gla = jnp.concatenate([_head_pad(wg_[:, :256]), _head_pad(wg_[:, 256:512]),
                                 wg_[:, 512:1536], _pad_cols(wg_[:, 1536:], LANE)], axis=1).astype(BF16)
        wr_ = wi[:, o_gla:o_rw]
        w_rw = jnp.concatenate([wr_[:, :1664], wr_[:, 1728:1856], _pad_cols(wr_[:, 1664:1728], LANE)],
                               axis=1).astype(BF16)
        mu_ = rw_mu[l]
        mu = jnp.concatenate([mu_[:1664], mu_[1728:1856], _pad_cols(mu_[1664:1728], LANE)])[None]
        w_gate = wi[:, o_rw:].astype(BF16)

        x = _ffn_call(x, mod, ng[0:1], ffn_gate[l, 0].astype(BF16), ffn_up[l, 0].astype(BF16),
                      ffn_down[l, 0].astype(BF16), mod_row=0, pos=pos if l == 0 else None)

        ssd_args = (x, mod, ng[1:2], w_ssd, ssd_conv_w[l], ssd_conv_b[l][None],
                    _pad_cols(ssd_dt_bias[l].reshape(1, -1), LANE),
                    _pad_cols(ssd_A_log[l].reshape(1, -1), LANE),
                    jnp.repeat(ssd_D[l], SSD_P, axis=1), ssd_norm[l][None], ef, eb)
        y_ssd_c, s_ssd = _ssd_call(*ssd_args, None, ctx=True)
        (y_ssd_l,) = _ssd_call(*ssd_args, state_ssd[:, l], ctx=False)
        new_ssd.append(s_ssd)

        gkw = jnp.stack([_rows_at(_head_pad(gla_gk_w[l, d]), d * GLA_LR, LANE) for d in range(2)])
        gla_args = (x, mod, ng[1:2], w_gla, gkw.astype(BF16), _head_pad(gla_gk_b[l]),
                    gla_norm[l][None])
        y_gla_c, s_gla = _gla_call(*gla_args, None, ctx=True)
        (y_gla_l,) = _gla_call(*gla_args, state_gla[:, l], ctx=False)
        new_gla.append(s_gla)

        w2p = jnp.stack([_rows_at(rw_w2[l, d], d * RW_LW, LANE) for d in range(2)]).astype(BF16)
        r, wf, wb, k, v, kk, nkka, g, bonus = _rw_prep_call(
            x, mod, ng[1:2], w_rw, mu, rw_a0[l][None], _rows_at(rw_a2[l], 0, LANE).astype(BF16),
            rw_g2[l].astype(BF16), rw_kk[l][None], rw_ka[l][None], rw_rk[l].reshape(1, D_MIX),
            rw_w0[l], w2p, bd)
        rc, kc, kkc, ac, wc = _rl_k_call(r, k, kk, nkka, wf, wb, lat=False)
        o_c, s_rw = _rw_scan_call(rc, wc, kc, _rl_v_call(v, lat=False), kkc, ac, None,
                                  want_final=True)
        new_rw.append(s_rw.reshape(2, RW_N, RW_N, BATCH, RW_H).transpose(3, 0, 4, 2, 1))
        rl, kl, kkl, al, wl = _rl_k_call(r, k, kk, nkka, wf, wb, lat=True)
        (o_l,) = _rw_scan_call(rl, wl, kl, _rl_v_call(v, lat=True), kkl, al, s0_rw[l],
                               want_final=False)

        x = _merge_call(x, mod, ng[1:2], w_gate, y_ssd_c, y_ssd_l, y_gla_c, y_gla_l,
                        _rl_out_call(o_c, lat=False), _rl_out_call(o_l, lat=True), bonus, g,
                        rw_ln_w[l][None], rw_ln_b[l][None], bd, w_ssd_o[l].astype(BF16),
                        w_gla_o[l].astype(BF16), w_rw_o[l].astype(BF16), w_out[l].astype(BF16))

        x = _ffn_call(x, mod, ng[2:3], ffn_gate[l, 1].astype(BF16), ffn_up[l, 1].astype(BF16),
                      ffn_down[l, 1].astype(BF16), mod_row=6,
                      final_g=final_norm[None] if l == DEPTH - 1 else None)

    y_prompt = x[0].reshape(BATCH, SEQ, D_MODEL)
    y_sample = x[1].reshape(DEC_BATCH, DEC_SEQ, D_MODEL)
    return (y_prompt, y_sample, jnp.stack(new_ssd, axis=1), jnp.stack(new_gla, axis=1),
            jnp.stack(new_rw, axis=1))
```

```python
import functools
import math

import numpy as np
import jax
import jax.numpy as jnp
from jax import lax
from jax.experimental import pallas as pl
from jax.experimental.pallas import tpu as pltpu

F32 = jnp.float32
BF16 = jnp.bfloat16

D_MODEL = 1024
BATCH = 16
SEQ = 256
DEPTH = 2
DEC_BATCH = 4
DEC_SEQ = 1024
GRID_W = 64
D_MIX = 512
D_FF = 2816
N_MOD = 9
SSD_P = 64
SSD_H = 8
SSD_N = 64
SSD_G = 2
SSD_XBC = 768
GLA_H = 4
GLA_DK = 64
GLA_DV = 128
GLA_LR = 16
GLA_GATE_NORM = 16.0
GLA_CHUNK = 64
RW_N = 64
RW_H = 8
RW_LW = 64
RW_LA = 64
RW_LG = 128
RMS_EPS = 1e-6
RW_GN_EPS = 64e-5

LANE = 128
SUB = 8
RB = 256
N_CTX_TOK = BATCH * SEQ
N_TOK = N_CTX_TOK + DEC_BATCH * DEC_SEQ
N_BLK = N_TOK // RB
N_CTX_BLK = N_CTX_TOK // RB
LAT_BLK_PER_SEQ = DEC_SEQ // RB
SSD_W = D_MIX + SSD_XBC + LANE
GLA_QK = GLA_H * LANE
GLA_W = 2 * GLA_QK + 2 * D_MIX + LANE
RW_W = 3 * D_MIX + 3 * LANE
SCAN_TB = 32
VMEM_LIMIT = 56 * 1024 * 1024


def _dot(a, b):
    return jnp.dot(a.astype(BF16), b.astype(BF16), preferred_element_type=F32)


def _dot_nt(a, b):
    return lax.dot_general(a.astype(BF16), b.astype(BF16), (((1,), (1,)), ((), ())),
                           preferred_element_type=F32)


def _split3(x):
    hi = x.astype(BF16)
    r1 = x - hi.astype(F32)
    mid = r1.astype(BF16)
    lo = (r1 - mid.astype(F32)).astype(BF16)
    return hi, mid, lo


def _sel_dot(m01, x):
    hi, mid, lo = _split3(x)
    f = lambda p: jnp.dot(m01, p, preferred_element_type=F32)
    return f(hi) + f(mid) + f(lo)


def _dot_sel(x, m01):
    hi, mid, lo = _split3(x)
    f = lambda p: jnp.dot(p, m01, preferred_element_type=F32)
    return f(hi) + f(mid) + f(lo)


def _sigmoid(x):
    return jax.nn.sigmoid(x)


def _silu(x):
    return x * jax.nn.sigmoid(x)


def _softplus(x):
    return jnp.maximum(x, 0.0) + jnp.log1p(jnp.exp(-jnp.abs(x)))


def _rmsnorm(x, g):
    return x * lax.rsqrt(jnp.mean(x * x, axis=-1, keepdims=True) + RMS_EPS) * g


def _norm_mod(x, g, shift, scale):
    return _rmsnorm(x, g) * (1.0 + scale) + shift


def _iota(shape, dim):
    return lax.broadcasted_iota(jnp.int32, shape, dim)


def _tri01(n, lower, chunk=None):
    t = _iota((n, n), 0)
    s = _iota((n, n), 1)
    m = (s <= t) if lower else (s >= t)
    if chunk is not None:
        m = m & ((t // chunk) == (s // chunk))
    return m


def _ada_kernel(c_ref, w_ref, b_ref, o_ref):
    o_ref[0] = _dot(_silu(c_ref[...]), w_ref[0]) + b_ref[0]


def _ada_call(cond8, w_ada, b_ada):
    tn = 1024
    nj = (N_MOD * D_MODEL) // tn
    return pl.pallas_call(
        _ada_kernel,
        grid=(DEPTH, nj),
        in_specs=[pl.BlockSpec((SUB, D_MODEL), lambda l, j: (0, 0)),
                  pl.BlockSpec((1, D_MODEL, tn), lambda l, j: (l, 0, j)),
                  pl.BlockSpec((1, 1, tn), lambda l, j: (l, 0, j))],
        out_specs=pl.BlockSpec((1, SUB, tn), lambda l, j: (l, 0, j)),
        out_shape=jax.ShapeDtypeStruct((DEPTH, SUB, N_MOD * D_MODEL), F32),
        compiler_params=pltpu.CompilerParams(dimension_semantics=("arbitrary", "arbitrary"),
                                             vmem_limit_bytes=VMEM_LIMIT),
        name="ada",
    )(cond8, w_ada, b_ada.reshape(DEPTH, 1, N_MOD * D_MODEL))


def _ffn_kernel(*refs, mod_row, first, last):
    it = iter(refs)
    is_ctx = pl.program_id(0) < N_CTX_BLK
    if first:
        xc_ref, xl_ref, pos_ref = next(it), next(it), next(it)
        x = jnp.where(is_ctx, xc_ref[...], xl_ref[...] + pos_ref[...])
    else:
        x = next(it)[...]
    mod_ref, ng_ref, wg_ref, wu_ref, wd_ref = next(it), next(it), next(it), next(it), next(it)
    fin_ref = next(it) if last else None
    shift = mod_ref[0, mod_row:mod_row + 1, :]
    scale = mod_ref[0, mod_row + 1:mod_row + 2, :]
    gate = mod_ref[0, mod_row + 2:mod_row + 3, :]
    h = _norm_mod(x, ng_ref[...], shift, scale).astype(BF16)
    a = _silu(jnp.dot(h, wg_ref[...], preferred_element_type=F32))
    a = (a * jnp.dot(h, wu_ref[...], preferred_element_type=F32)).astype(BF16)
    y = x + 0.5 * gate * jnp.dot(a, wd_ref[...], preferred_element_type=F32)
    if last:
        y = _rmsnorm(y, fin_ref[...])
        oc_ref, ol_ref = next(it), next(it)

        @pl.when(is_ctx)
        def _():
            oc_ref[...] = y

        @pl.when(jnp.logical_not(is_ctx))
        def _():
            ol_ref[...] = y
    else:
        next(it)[...] = y


def _const_spec(shape):
    nd = len(shape)
    return pl.BlockSpec(shape, lambda i: (0,) * nd, pipeline_mode=pl.Buffered(1))


def _ctx_idx(i):
    return jnp.minimum(i, N_CTX_BLK - 1)


def _lat_idx(i):
    return jnp.maximum(i - N_CTX_BLK, 0)


def _ffn_call(x, mod, ng, wg, wu, wd, *, mod_row, pos=None, final_g=None):
    first = pos is not None
    last = final_g is not None
    row_spec = pl.BlockSpec((RB, D_MODEL), lambda i: (i, 0))
    ctx_spec = pl.BlockSpec((RB, D_MODEL), lambda i: (_ctx_idx(i), 0))
    lat_spec = pl.BlockSpec((RB, D_MODEL), lambda i: (_lat_idx(i), 0))
    if first:
        in_specs = [ctx_spec, lat_spec,
                    pl.BlockSpec((RB, D_MODEL), lambda i: (_lat_idx(i) % LAT_BLK_PER_SEQ, 0))]
        args = [x[0], x[1], pos]
    else:
        in_specs = [row_spec]
        args = [x]
    in_specs += [pl.BlockSpec((1, N_MOD, D_MODEL), lambda i: (i, 0, 0)),
                 _const_spec((1, D_MODEL)), _const_spec((D_MODEL, D_FF)),
                 _const_spec((D_MODEL, D_FF)), _const_spec((D_FF, D_MODEL))]
    args += [mod, ng, wg, wu, wd]
    if last:
        in_specs.append(_const_spec((1, D_MODEL)))
        args.append(final_g)
        half = jax.ShapeDtypeStruct((N_TOK // 2, D_MODEL), F32)
        out_specs, out_shape = [ctx_spec, lat_spec], [half, half]
    else:
        out_specs, out_shape = row_spec, jax.ShapeDtypeStruct((N_TOK, D_MODEL), F32)
    return pl.pallas_call(
        functools.partial(_ffn_kernel, mod_row=mod_row, first=first, last=last),
        grid=(N_BLK,),
        in_specs=in_specs,
        out_specs=out_specs,
        out_shape=out_shape,
        compiler_params=pltpu.CompilerParams(dimension_semantics=("arbitrary",),
                                             vmem_limit_bytes=VMEM_LIMIT),
        name="ffn",
    )(*args)


def _ssd_kernel(*refs, L, has_init, want_final):
    it = iter(refs)
    x_ref, mod_ref, ng_ref, w_ref = next(it), next(it), next(it), next(it)
    cw_ref, cb_ref, dtb_ref, alog_ref, d_ref, nw_ref = (next(it), next(it), next(it), next(it),
                                                        next(it), next(it))
    ef_ref, eb_ref = next(it), next(it)
    s0_ref = next(it) if has_init else None
    y_ref = next(it)
    sfin_ref = next(it) if want_final else None
    yacc = next(it)

    C = RB
    nc = L // C
    x = x_ref[...]
    u = _norm_mod(x, ng_ref[...], mod_ref[0, 3:4, :], mod_ref[0, 4:5, :]).astype(BF16)
    p = jnp.dot(u, w_ref[...], preferred_element_type=F32)
    z = p[:, :D_MIX]
    xbc = p[:, D_MIX:D_MIX + SSD_XBC]
    dtp = p[:, D_MIX + SSD_XBC:]
    rows = _iota((L, 1), 0)
    prev = jnp.where(rows == 0, 0.0, pltpu.roll(xbc, 1, 0))
    nxt = jnp.where(rows == L - 1, 0.0, pltpu.roll(xbc, L - 1, 0))
    xc = cb_ref[...] + prev * cw_ref[0:1, :] + xbc * cw_ref[1:2, :] + nxt * cw_ref[2:3, :]
    xc = _silu(xc)
    xs = xc[:, :D_MIX]
    bm = xc[:, D_MIX:D_MIX + SSD_G * SSD_N]
    cm = xc[:, D_MIX + SSD_G * SSD_N:]
    dt = _softplus(dtp + dtb_ref[...])
    adt = dt * (-jnp.exp(alog_ref[...]))

    tril = _tri01(C, True)
    triu = _tri01(C, False)
    tril_b = jnp.where(tril, 1.0, 0.0).astype(BF16)
    triu_b = jnp.where(triu, 1.0, 0.0).astype(BF16)
    ef = ef_ref[...]
    eb = eb_ref[...]

    cs, csT, rcs, rcsT, dtT, loc_f, loc_b = [], [], [], [], [], [], []
    for c in range(nc):
        a_c = adt[c * C:(c + 1) * C]
        a_cT = a_c.T
        cs.append(_sel_dot(tril_b, a_c))
        rcs.append(_sel_dot(triu_b, a_c))
        csT.append(_dot_sel(a_cT, triu_b))
        rcsT.append(_dot_sel(a_cT, tril_b))
        dtT.append(dt[c * C:(c + 1) * C].T)

    need_states = want_final or nc > 1
    if need_states:
        for c in range(nc):
            xs_c = xs[c * C:(c + 1) * C]
            dt_c = dt[c * C:(c + 1) * C]
            wf = jnp.exp(cs[c][C - 1:C, :] - cs[c]) * dt_c
            wb = jnp.exp(rcs[c][0:1, :] - rcs[c]) * dt_c
            xwf = (xs_c * _dot_sel(wf, ef)).T
            xwb = (xs_c * _dot_sel(wb, eb)).T
            lf, lb = [], []
            for h in range(SSD_H):
                g = h // (SSD_H // SSD_G)
                bm_g = bm[c * C:(c + 1) * C, g * SSD_N:(g + 1) * SSD_N]
                lf.append(_dot(xwf[h * SSD_P:(h + 1) * SSD_P, :], bm_g))
                lb.append(_dot(xwb[h * SSD_P:(h + 1) * SSD_P, :], bm_g))
            loc_f.append(lf)
            loc_b.append(lb)

    zero_s = jnp.zeros((SSD_P, SSD_N), F32)
    sin_f = [[None] * SSD_H for _ in range(nc + 1)]
    sin_b = [[None] * SSD_H for _ in range(nc + 1)]
    for h in range(SSD_H):
        sin_f[0][h] = s0_ref[0, 0, h] if has_init else zero_s
        sin_b[nc][h] = s0_ref[0, 1, h] if has_init else zero_s
    if need_states:
        for c in range(nc):
            dec = jnp.exp(cs[c][C - 1:C, :])
            for h in range(SSD_H):
                sin_f[c + 1][h] = dec[:, h:h + 1] * sin_f[c][h] + loc_f[c][h]
        for c in range(nc - 1, -1, -1):
            dec = jnp.exp(rcs[c][0:1, :])
            for h in range(SSD_H):
                sin_b[c][h] = dec[:, SSD_H + h:SSD_H + h + 1] * sin_b[c + 1][h] + loc_b[c][h]

    for c in range(nc):
        sl = slice(c * C, (c + 1) * C)
        ecs = jnp.exp(cs[c])
        ercs = jnp.exp(rcs[c])
        for h in range(SSD_H):
            g = h // (SSD_H // SSD_G)
            cm_g = cm[sl, g * SSD_N:(g + 1) * SSD_N]
            bm_g = bm[sl, g * SSD_N:(g + 1) * SSD_N]
            cb = _dot_nt(cm_g, bm_g)
            lf = jnp.exp(jnp.where(tril, cs[c][:, h:h + 1] - csT[c][h:h + 1, :], -jnp.inf))
            lb = jnp.exp(jnp.where(triu, rcs[c][:, SSD_H + h:SSD_H + h + 1]
                                   - rcsT[c][SSD_H + h:SSD_H + h + 1, :], -jnp.inf))
            m = cb * (lf * dtT[c][h:h + 1, :] + lb * dtT[c][SSD_H + h:SSD_H + h + 1, :])
            y_h = _dot(m, xs[sl, h * SSD_P:(h + 1) * SSD_P])
            if has_init or nc > 1:
                y_h = y_h + ecs[:, h:h + 1] * _dot_nt(cm_g, sin_f[c][h])
                y_h = y_h + ercs[:, SSD_H + h:SSD_H + h + 1] * _dot_nt(cm_g, sin_b[c + 1][h])
            yacc[sl, h * SSD_P:(h + 1) * SSD_P] = y_h

    y = yacc[...] + xs * (d_ref[0:1, :] + d_ref[1:2, :])
    y = _rmsnorm(y * _silu(z), nw_ref[...])
    y_ref[...] = y.astype(BF16)
    if want_final:
        for h in range(SSD_H):
            sfin_ref[0, 0, h] = sin_f[nc][h]
            sfin_ref[0, 1, h] = sin_b[0][h]


def _ssd_call(x, mod, ng, w, cw, cb, dtb, alog, dexp, nw, ef, eb, s0, *, ctx):
    L = SEQ if ctx else DEC_SEQ
    nseq = BATCH if ctx else DEC_BATCH
    blk0 = 0 if ctx else N_CTX_TOK // L
    mod_of = (lambda i: (i, 0, 0)) if ctx else (lambda i: (N_CTX_BLK + LAT_BLK_PER_SEQ * i, 0, 0))
    in_specs = [pl.BlockSpec((L, D_MODEL), lambda i: (blk0 + i, 0)),
                pl.BlockSpec((1, N_MOD, D_MODEL), mod_of),
                _const_spec((1, D_MODEL)), _const_spec((D_MODEL, SSD_W)),
                _const_spec((3, SSD_XBC)), _const_spec((1, SSD_XBC)),
                _const_spec((1, LANE)), _const_spec((1, LANE)),
                _const_spec((2, D_MIX)), _const_spec((1, D_MIX)),
                _const_spec((LANE, D_MIX)), _const_spec((LANE, D_MIX))]
    args = [x, mod, ng, w, cw, cb, dtb, alog, dexp, nw, ef, eb]
    st_spec = pl.BlockSpec((1, 2, SSD_H, SSD_P, SSD_N), lambda i: (i, 0, 0, 0, 0))
    out_specs = [pl.BlockSpec((L, D_MIX), lambda i: (i, 0))]
    out_shape = [jax.ShapeDtypeStruct((nseq * L, D_MIX), BF16)]
    if ctx:
        out_specs.append(st_spec)
        out_shape.append(jax.ShapeDtypeStruct((nseq, 2, SSD_H, SSD_P, SSD_N), F32))
    else:
        in_specs.append(st_spec)
        args.append(s0)
    return pl.pallas_call(
        functools.partial(_ssd_kernel, L=L, has_init=not ctx, want_final=ctx),
        grid=(nseq,),
        in_specs=in_specs,
        out_specs=out_specs,
        out_shape=out_shape,
        scratch_shapes=[pltpu.VMEM((L, D_MIX), F32)],
        compiler_params=pltpu.CompilerParams(dimension_semantics=("arbitrary",),
                                             vmem_limit_bytes=VMEM_LIMIT),
        name="ssd_ctx" if ctx else "ssd_lat",
    )(*args)


def _gla_kernel(*refs, L, has_init, want_final):
    it = iter(refs)
    x_ref, mod_ref, ng_ref, w_ref = next(it), next(it), next(it), next(it)
    gkw_ref, gkb_ref, nw_ref = next(it), next(it), next(it)
    s0_ref = next(it) if has_init else None
    y_ref = next(it)
    sfin_ref = next(it) if want_final else None
    oacc = next(it)

    C = GLA_CHUNK
    B = RB
    cpb = B // C
    nb = L // B
    x = x_ref[...]
    u = _norm_mod(x, ng_ref[...], mod_ref[0, 3:4, :], mod_ref[0, 4:5, :]).astype(BF16)
    p = jnp.dot(u, w_ref[...], preferred_element_type=F32)
    q = p[:, :GLA_QK] * (GLA_DK ** -0.5)
    k = p[:, GLA_QK:2 * GLA_QK]
    v = p[:, 2 * GLA_QK:2 * GLA_QK + D_MIX]
    gg = p[:, 2 * GLA_QK + D_MIX:2 * GLA_QK + 2 * D_MIX]
    lr = p[:, 2 * GLA_QK + 2 * D_MIX:].astype(BF16)
    lg = []
    for d in range(2):
        pre = jnp.dot(lr, gkw_ref[d], preferred_element_type=F32) + gkb_ref[d:d + 1, :]
        lg.append(-_softplus(-pre) / GLA_GATE_NORM)

    low = _tri01(B, True, C)
    upp = _tri01(B, False, C)
    low_b = jnp.where(low, 1.0, 0.0).astype(BF16)
    upp_b = jnp.where(upp, 1.0, 0.0).astype(BF16)
    rowi = _iota((B, 1), 0)

    zero_s = jnp.zeros((LANE, GLA_DV), F32)
    if has_init:
        pad = jnp.zeros((LANE - GLA_DK, GLA_DV), F32)
        s_f = [jnp.concatenate([s0_ref[0, 0, h], pad], axis=0) for h in range(GLA_H)]
        s_b0 = [jnp.concatenate([s0_ref[0, 1, h], pad], axis=0) for h in range(GLA_H)]
    else:
        s_f = [zero_s] * GLA_H
        s_b0 = [zero_s] * GLA_H

    blocks = []
    for b in range(nb):
        sl = slice(b * B, (b + 1) * B)
        bf = _sel_dot(low_b, lg[0][sl])
        rb = _sel_dot(upp_b, lg[1][sl])
        q_b, k_b = q[sl], k[sl]
        blocks.append(dict(sl=sl, bf=bf, rb=rb, bfT=bf.T, rbT=rb.T,
                           qf=q_b * jnp.exp(bf), kf=k_b * jnp.exp(-bf),
                           qb=q_b * jnp.exp(rb), kb=k_b * jnp.exp(-rb), k=k_b))

    for blk in blocks:
        sl = blk["sl"]
        for h in range(GLA_H):
            hs = slice(h * LANE, (h + 1) * LANE)
            a = jnp.where(low, _dot_nt(blk["qf"][:, hs], blk["kf"][:, hs]), 0.0)
            a = a + jnp.where(upp, _dot_nt(blk["qb"][:, hs], blk["kb"][:, hs]), 0.0)
            oacc[sl, h * GLA_DV:(h + 1) * GLA_DV] = _dot(a, v[sl, h * GLA_DV:(h + 1) * GLA_DV])

    for blk in blocks:
        sl = blk["sl"]
        r0 = sl.start
        for c in range(cpb):
            last = blk["bf"][(c + 1) * C - 1:(c + 1) * C, :]
            kdecT = (blk["k"] * jnp.exp(last - blk["bf"])).T
            in_c = (rowi >= c * C) & (rowi < (c + 1) * C)
            rs = slice(r0 + c * C, r0 + (c + 1) * C)
            for h in range(GLA_H):
                hs = slice(h * LANE, (h + 1) * LANE)
                vs = slice(h * GLA_DV, (h + 1) * GLA_DV)
                oacc[rs, vs] = oacc[rs, vs] + _dot(blk["qf"][c * C:(c + 1) * C, hs], s_f[h])
                kv = _dot(kdecT[hs, :], jnp.where(in_c, v[sl, vs], 0.0))
                dec = jnp.exp(blk["bfT"][hs, (c + 1) * C - 1:(c + 1) * C])
                s_f[h] = dec * s_f[h] + kv

    s_b = s_b0
    for blk in reversed(blocks):
        sl = blk["sl"]
        r0 = sl.start
        for c in range(cpb - 1, -1, -1):
            first = blk["rb"][c * C:c * C + 1, :]
            kdecT = (blk["k"] * jnp.exp(first - blk["rb"])).T
            in_c = (rowi >= c * C) & (rowi < (c + 1) * C)
            rs = slice(r0 + c * C, r0 + (c + 1) * C)
            for h in range(GLA_H):
                hs = slice(h * LANE, (h + 1) * LANE)
                vs = slice(h * GLA_DV, (h + 1) * GLA_DV)
                oacc[rs, vs] = oacc[rs, vs] + _dot(blk["qb"][c * C:(c + 1) * C, hs], s_b[h])
                kv = _dot(kdecT[hs, :], jnp.where(in_c, v[sl, vs], 0.0))
                dec = jnp.exp(blk["rbT"][hs, c * C:c * C + 1])
                s_b[h] = dec * s_b[h] + kv

    for h in range(GLA_H):
        vs = slice(h * GLA_DV, (h + 1) * GLA_DV)
        o_h = _rmsnorm(oacc[:, vs], nw_ref[...])
        y_ref[:, vs] = (o_h * _silu(gg[:, vs])).astype(BF16)
        if want_final:
            sfin_ref[0, 0, h] = s_f[h][:GLA_DK, :]
            sfin_ref[0, 1, h] = s_b[h][:GLA_DK, :]


def _gla_call(x, mod, ng, w, gkw, gkb, nw, s0, *, ctx):
    L = SEQ if ctx else DEC_SEQ
    nseq = BATCH if ctx else DEC_BATCH
    blk0 = 0 if ctx else N_CTX_TOK // L
    mod_of = (lambda i: (i, 0, 0)) if ctx else (lambda i: (N_CTX_BLK + LAT_BLK_PER_SEQ * i, 0, 0))
    in_specs = [pl.BlockSpec((L, D_MODEL), lambda i: (blk0 + i, 0)),
                pl.BlockSpec((1, N_MOD, D_MODEL), mod_of),
                _const_spec((1, D_MODEL)), _const_spec((D_MODEL, GLA_W)),
                _const_spec((2, LANE, GLA_QK)), _const_spec((2, GLA_QK)),
                _const_spec((1, GLA_DV))]
    args = [x, mod, ng, w, gkw, gkb, nw]
    st_spec = pl.BlockSpec((1, 2, GLA_H, GLA_DK, GLA_DV), lambda i: (i, 0, 0, 0, 0))
    out_specs = [pl.BlockSpec((L, D_MIX), lambda i: (i, 0))]
    out_shape = [jax.ShapeDtypeStruct((nseq * L, D_MIX), BF16)]
    if ctx:
        out_specs.append(st_spec)
        out_shape.append(jax.ShapeDtypeStruct((nseq, 2, GLA_H, GLA_DK, GLA_DV), F32))
    else:
        in_specs.append(st_spec)
        args.append(s0)
    return pl.pallas_call(
        functools.partial(_gla_kernel, L=L, has_init=not ctx, want_final=ctx),
        grid=(nseq,),
        in_specs=in_specs,
        out_specs=out_specs,
        out_shape=out_shape,
        scratch_shapes=[pltpu.VMEM((L, D_MIX), F32)],
        compiler_params=pltpu.CompilerParams(dimension_semantics=("arbitrary",),
                                             vmem_limit_bytes=VMEM_LIMIT),
        name="gla_ctx" if ctx else "gla_lat",
    )(*args)


def _rw_prep_kernel(x_ref, xp_ref, xn_ref, mod_ref, ng_ref, w_ref, mu_ref, a0_ref, a2_ref, g2_ref,
                    kkw_ref, ka_ref, rk_ref, w0_ref, w2_ref, bd_ref,
                    r_ref, wf_ref, wb_ref, k_ref, v_ref, kk_ref, nkka_ref, g_ref, bonus_ref):
    i = pl.program_id(0)
    j = (i - N_CTX_BLK) % LAT_BLK_PER_SEQ
    is_first = (i < N_CTX_BLK) | (j == 0)
    is_last = (i < N_CTX_BLK) | (j == LAT_BLK_PER_SEQ - 1)
    x_all = jnp.concatenate([x_ref[...], xp_ref[...], xn_ref[...]], axis=0)
    u_all = _norm_mod(x_all, ng_ref[...], mod_ref[0, 3:4, :], mod_ref[0, 4:5, :]).astype(BF16)
    p_all = jnp.dot(u_all, w_ref[...], preferred_element_type=F32)
    p = p_all[:RB]
    p_prev = jnp.where(is_first, 0.0, p_all[RB + SUB - 1:RB + SUB, :])
    p_next = jnp.where(is_last, 0.0, p_all[RB + SUB:RB + SUB + 1, :])
    rows = _iota((RB, 1), 0)
    prev = jnp.where(rows == 0, p_prev, pltpu.roll(p, 1, 0))
    nxt = jnp.where(rows == RB - 1, p_next, pltpu.roll(p, RB - 1, 0))
    p = p + (0.5 * (prev + nxt) - p) * mu_ref[...]

    r = p[:, :D_MIX]
    k = p[:, D_MIX:2 * D_MIX]
    v = p[:, 2 * D_MIX:3 * D_MIX]
    wlr = p[:, 3 * D_MIX:3 * D_MIX + LANE]
    glr = p[:, 3 * D_MIX + LANE:3 * D_MIX + 2 * LANE]
    alr = p[:, 3 * D_MIX + 2 * LANE:]
    bd = bd_ref[...]
    a = _sigmoid(a0_ref[...] + _dot(alr, a2_ref[...]))
    g = _dot(_sigmoid(glr), g2_ref[...])
    kk = k * kkw_ref[...]
    kk = kk / jnp.maximum(jnp.sqrt(_dot_sel(kk * kk, bd)), 1e-12)
    k = k * (1.0 + (a - 1.0) * ka_ref[...])
    tw = jnp.tanh(wlr).astype(BF16)
    for d, o_ref in ((0, wf_ref), (1, wb_ref)):
        pre = w0_ref[d:d + 1, :] + jnp.dot(tw, w2_ref[d], preferred_element_type=F32)
        wl = -_softplus(-pre) - 0.5
        o_ref[0] = jnp.exp(-jnp.exp(wl)).T
    r_ref[0] = r.T
    k_ref[0] = k.T
    v_ref[0] = v.T
    kk_ref[0] = kk.T
    nkka_ref[0] = (-(kk * a)).T
    g_ref[...] = g
    bonus_ref[...] = _dot_sel(r * k * rk_ref[...], bd) * v


def _rw_prep_call(x, mod, ng, w, mu, a0, a2, g2, kkw, ka, rk, w0, w2, bd):
    hb = RB // SUB
    row_spec = pl.BlockSpec((RB, D_MODEL), lambda i: (i, 0))
    out_spec = pl.BlockSpec((RB, D_MIX), lambda i: (i, 0))
    in_specs = [row_spec,
                pl.BlockSpec((SUB, D_MODEL), lambda i: (jnp.maximum(i * hb - 1, 0), 0)),
                pl.BlockSpec((SUB, D_MODEL), lambda i: (jnp.minimum((i + 1) * hb, N_TOK // SUB - 1), 0)),
                pl.BlockSpec((1, N_MOD, D_MODEL), lambda i: (i, 0, 0)),
                _const_spec((1, D_MODEL)), _const_spec((D_MODEL, RW_W)), _const_spec((1, RW_W)),
                _const_spec((1, D_MIX)), _const_spec((LANE, D_MIX)), _const_spec((LANE, D_MIX)),
                _const_spec((1, D_MIX)), _const_spec((1, D_MIX)), _const_spec((1, D_MIX)),
                _const_spec((2, D_MIX)), _const_spec((2, LANE, D_MIX)), _const_spec((D_MIX, D_MIX))]
    t_spec = pl.BlockSpec((1, D_MIX, RB), lambda i: (i, 0, 0))
    t_shape = jax.ShapeDtypeStruct((N_BLK, D_MIX, RB), F32)
    tok_shape = jax.ShapeDtypeStruct((N_TOK, D_MIX), F32)
    return pl.pallas_call(
        _rw_prep_kernel,
        grid=(N_BLK,),
        in_specs=in_specs,
        out_specs=[t_spec] * 7 + [out_spec] * 2,
        out_shape=[t_shape] * 7 + [tok_shape] * 2,
        compiler_params=pltpu.CompilerParams(dimension_semantics=("arbitrary",),
                                             vmem_limit_bytes=VMEM_LIMIT),
        name="rw_prep",
    )(x, x, x, mod, ng, w, mu, a0, a2, g2, kkw, ka, rk, w0, w2, bd)


RW_VQ = LANE // (DEC_BATCH * RW_H)
RW_NV_LAT = RW_N // RW_VQ


def _rl_rows(ref, nl, lat):
    x = ref[:, 0, :, nl, :] if lat else ref[:, :, nl, :]
    return x.reshape(-1, RB)


def _rl_k_kernel(r_ref, k_ref, kk_ref, a_ref, wf_ref, wb_ref, ro, ko, kko, ao, wo, *, lat):
    def lanes(ref, nl):
        x = _rl_rows(ref, nl, lat)
        if lat:
            x = jnp.concatenate([x] * RW_VQ, axis=0)
        return x.T

    for nl in range(SUB):
        ro[nl] = lanes(r_ref, nl)
        ko[nl] = lanes(k_ref, nl)
        kko[nl] = lanes(kk_ref, nl)
        ao[nl] = lanes(a_ref, nl)
        wo[0, nl] = lanes(wf_ref, nl)
        wo[1, nl] = lanes(wb_ref, nl)


def _rl_k_call(r, k, kk, nkka, wf, wb, *, lat):
    nb = RW_N // SUB
    if lat:
        L = DEC_SEQ
        view = (N_BLK // LAT_BLK_PER_SEQ, LAT_BLK_PER_SEQ, RW_H, RW_N, RB)
        grid = (LAT_BLK_PER_SEQ, nb)
        ispec = pl.BlockSpec((DEC_BATCH, 1, RW_H, SUB, RB), lambda j, n: (1, j, 0, n, 0))
        ospec = pl.BlockSpec((SUB, RB, LANE), lambda j, n: (n, j, 0))
        wspec = pl.BlockSpec((2, SUB, RB, LANE), lambda j, n: (0, n, j, 0))
    else:
        L = SEQ
        view = (N_BLK, RW_H, RW_N, RB)
        grid = (nb,)
        ispec = pl.BlockSpec((BATCH, RW_H, SUB, RB), lambda n: (0, 0, n, 0))
        ospec = pl.BlockSpec((SUB, RB, LANE), lambda n: (n, 0, 0))
        wspec = pl.BlockSpec((2, SUB, RB, LANE), lambda n: (0, n, 0, 0))
    kshape = jax.ShapeDtypeStruct((RW_N, L, LANE), F32)
    return pl.pallas_call(
        functools.partial(_rl_k_kernel, lat=lat),
        grid=grid,
        in_specs=[ispec] * 6,
        out_specs=[ospec] * 4 + [wspec],
        out_shape=[kshape] * 4 + [jax.ShapeDtypeStruct((2, RW_N, L, LANE), F32)],
        compiler_params=pltpu.CompilerParams(dimension_semantics=("arbitrary",) * len(grid),
                                             vmem_limit_bytes=VMEM_LIMIT),
        name="rl_k_lat" if lat else "rl_k_ctx",
    )(*[a.reshape(view) for a in (r, k, kk, nkka, wf, wb)])


def _rl_v_kernel(*refs, lat):
    o_ref = refs[-1]
    for nl in range(SUB):
        x = jnp.concatenate([_rl_rows(ref, nl, lat) for ref in refs[:-1]], axis=0) if lat \
            else _rl_rows(refs[0], nl, lat)
        o_ref[:, nl, :] = x.T


def _rl_v_call(v, *, lat):
    if lat:
        view = (N_BLK // LAT_BLK_PER_SEQ, LAT_BLK_PER_SEQ, RW_H, RW_N, RB)
        nvb = RW_NV_LAT // SUB
        grid = (LAT_BLK_PER_SEQ, nvb)
        ispecs = [pl.BlockSpec((DEC_BATCH, 1, RW_H, SUB, RB),
                               functools.partial(lambda j, n, q: (1, j, 0, q * nvb + n, 0), q=q))
                  for q in range(RW_VQ)]
        ospec = pl.BlockSpec((RB, SUB, LANE), lambda j, n: (j, n, 0))
        oshape = jax.ShapeDtypeStruct((DEC_SEQ, RW_NV_LAT, LANE), F32)
    else:
        view = (N_BLK, RW_H, RW_N, RB)
        grid = (RW_N // SUB,)
        ispecs = [pl.BlockSpec((BATCH, RW_H, SUB, RB), lambda n: (0, 0, n, 0))]
        ospec = pl.BlockSpec((RB, SUB, LANE), lambda n: (0, n, 0))
        oshape = jax.ShapeDtypeStruct((SEQ, RW_N, LANE), F32)
    vv = v.reshape(view)
    return pl.pallas_call(
        functools.partial(_rl_v_kernel, lat=lat),
        grid=grid,
        in_specs=ispecs,
        out_specs=ospec,
        out_shape=oshape,
        compiler_params=pltpu.CompilerParams(dimension_semantics=("arbitrary",) * len(grid),
                                             vmem_limit_bytes=VMEM_LIMIT),
        name="rl_v_lat" if lat else "rl_v_ctx",
    )(*([vv] * len(ispecs)))


def _rl_out_kernel(o_ref, ot_ref, *, lat):
    for nl in range(SUB):
        x = (o_ref[0, :, nl, :] + o_ref[1, :, nl, :]).T
        if lat:
            rows = DEC_BATCH * RW_H
            for q in range(RW_VQ):
                ot_ref[:, 0, :, q, nl, :] = x[q * rows:(q + 1) * rows].reshape(DEC_BATCH, RW_H, RB)
        else:
            ot_ref[:, :, nl, :] = x.reshape(BATCH, RW_H, RB)


def _rl_out_call(o, *, lat):
    if lat:
        grid = (LAT_BLK_PER_SEQ, RW_NV_LAT // SUB)
        ispec = pl.BlockSpec((2, RB, SUB, LANE), lambda j, n: (0, j, n, 0))
        ospec = pl.BlockSpec((DEC_BATCH, 1, RW_H, RW_VQ, SUB, RB), lambda j, n: (0, j, 0, 0, n, 0))
        oshape = (DEC_BATCH, LAT_BLK_PER_SEQ, RW_H, RW_VQ, RW_NV_LAT, RB)
    else:
        grid = (RW_N // SUB,)
        ispec = pl.BlockSpec((2, RB, SUB, LANE), lambda n: (0, 0, n, 0))
        ospec = pl.BlockSpec((BATCH, RW_H, SUB, RB), lambda n: (0, 0, n, 0))
        oshape = (BATCH, RW_H, RW_N, RB)
    out = pl.pallas_call(
        functools.partial(_rl_out_kernel, lat=lat),
        grid=grid,
        in_specs=[ispec],
        out_specs=ospec,
        out_shape=jax.ShapeDtypeStruct(oshape, F32),
        compiler_params=pltpu.CompilerParams(dimension_semantics=("arbitrary",) * len(grid),
                                             vmem_limit_bytes=VMEM_LIMIT),
        name="rl_out_lat" if lat else "rl_out_ctx",
    )(o)
    return out.reshape(N_CTX_BLK, D_MIX, RB)


def _rw_scan_kernel(*refs, vb, npart, has_init, want_final):
    it = iter(refs)
    r_ref, w_ref, k_ref, v_ref, kk_ref, nkka_ref = (next(it), next(it), next(it), next(it),
                                                    next(it), next(it))
    s0_ref = next(it) if has_init else None
    o_ref = next(it)
    sfin_ref = next(it) if want_final else None
    s_scr = next(it)
    g = pl.program_id(0)
    tb = pl.program_id(1)

    @pl.when(tb == 0)
    def _():
        if has_init:
            s_scr[...] = s0_ref[0]
        else:
            s_scr[...] = jnp.zeros(s_scr.shape, F32)

    def bcast(ref, t, kx):
        return jnp.broadcast_to(ref[kx, pl.ds(t, 1), :], (SUB, LANE))

    def t_of(i):
        return jnp.where(g == 0, i, SCAN_TB - 1 - i)

    def zeros():
        return [[jnp.zeros((SUB, LANE), F32) for _ in range(vb)] for _ in range(npart)]

    def total(parts):
        return [functools.reduce(lambda a, b: a + b, [p[j] for p in parts]) for j in range(vb)]

    t0 = t_of(0)
    acc = zeros()
    for kx in range(RW_N):
        kkb = bcast(kk_ref, t0, kx)
        for j in range(vb):
            acc[kx % npart][j] = acc[kx % npart][j] + s_scr[kx, j * SUB:(j + 1) * SUB, :] * kkb

    def step(i, skk):
        t = t_of(i)
        tn = t_of(jnp.minimum(i + 1, SCAN_TB - 1))
        vv = [v_ref[t, j * SUB:(j + 1) * SUB, :] for j in range(vb)]
        oacc, nacc = zeros(), zeros()
        for kx in range(RW_N):
            wb = jnp.broadcast_to(w_ref[0, kx, pl.ds(t, 1), :], (SUB, LANE))
            ab = bcast(nkka_ref, t, kx)
            kb = bcast(k_ref, t, kx)
            rb = bcast(r_ref, t, kx)
            kkn = bcast(kk_ref, tn, kx)
            p = kx % npart
            for j in range(vb):
                s = s_scr[kx, j * SUB:(j + 1) * SUB, :] * wb + skk[j] * ab + vv[j] * kb
                s_scr[kx, j * SUB:(j + 1) * SUB, :] = s
                oacc[p][j] = oacc[p][j] + s * rb
                nacc[p][j] = nacc[p][j] + s * kkn
        for j, o in enumerate(total(oacc)):
            o_ref[0, t, j * SUB:(j + 1) * SUB, :] = o
        return tuple(total(nacc))

    lax.fori_loop(0, SCAN_TB, step, tuple(total(acc)))

    if want_final:
        @pl.when(tb == pl.num_programs(1) - 1)
        def _():
            sfin_ref[0] = s_scr[...]


def _rw_scan_call(r, w, k, v, kk, nkka, s0, *, want_final):
    L = r.shape[1]
    nv = v.shape[1]
    ntb = L // SCAN_TB
    has_init = s0 is not None
    tmap = lambda g, t: t + g * (ntb - 1 - 2 * t)
    kspec = pl.BlockSpec((RW_N, SCAN_TB, LANE), lambda g, t: (0, tmap(g, t), 0))
    wspec = pl.BlockSpec((1, RW_N, SCAN_TB, LANE), lambda g, t: (g, 0, tmap(g, t), 0))
    vspec = pl.BlockSpec((SCAN_TB, nv, LANE), lambda g, t: (tmap(g, t), 0, 0))
    ospec = pl.BlockSpec((1, SCAN_TB, nv, LANE), lambda g, t: (g, tmap(g, t), 0, 0))
    sspec = pl.BlockSpec((1, RW_N, nv, LANE), lambda g, t: (g, 0, 0, 0))
    in_specs = [kspec, wspec, kspec, vspec, kspec, kspec]
    args = [r, w, k, v, kk, nkka]
    if has_init:
        in_specs.append(sspec)
        args.append(s0)
    out_specs = [ospec]
    out_shape = [jax.ShapeDtypeStruct((2, L, nv, LANE), F32)]
    if want_final:
        out_specs.append(sspec)
        out_shape.append(jax.ShapeDtypeStruct((2, RW_N, nv, LANE), F32))
    return pl.pallas_call(
        functools.partial(_rw_scan_kernel, vb=nv // SUB, npart=max(1, 4 * SUB // nv),
                          has_init=has_init, want_final=want_final),
        grid=(2, ntb),
        in_specs=in_specs,
        out_specs=out_specs,
        out_shape=out_shape,
        scratch_shapes=[pltpu.VMEM((RW_N, nv, LANE), F32)],
        compiler_params=pltpu.CompilerParams(dimension_semantics=("arbitrary", "arbitrary"),
                                             vmem_limit_bytes=VMEM_LIMIT),
        name="rw_scan",
    )(*args)


def _merge_kernel(x_ref, mod_ref, ng_ref, wgate_ref, yssd_c, yssd_l, ygla_c, ygla_l, orw_c, orw_l,
                  bonus_ref, g_ref, lnw_ref, lnb_ref, bd_ref, wso_ref, wgo_ref, wro_ref, wout_ref,
                  o_ref):
    is_ctx = pl.program_id(0) < N_CTX_BLK
    x = x_ref[...]
    u = _norm_mod(x, ng_ref[...], mod_ref[0, 3:4, :], mod_ref[0, 4:5, :]).astype(BF16)
    bd = bd_ref[...]
    yssd = jnp.where(is_ctx, yssd_c[...], yssd_l[...])
    ygla = jnp.where(is_ctx, ygla_c[...], ygla_l[...])
    o = jnp.where(is_ctx, orw_c[0], orw_l[0]).T
    mu = _dot_sel(o, bd) * (1.0 / RW_N)
    oc = o - mu
    var = _dot_sel(oc * oc, bd) * (1.0 / RW_N)
    o = oc * lax.rsqrt(var + RW_GN_EPS) * lnw_ref[...] + lnb_ref[...]
    y_rw = ((o + bonus_ref[...]) * g_ref[...]).astype(BF16)
    merged = jnp.zeros((RB, D_MODEL), F32)
    for b, (y, wo_ref) in enumerate(((yssd, wso_ref), (ygla, wgo_ref), (y_rw, wro_ref))):
        gate = _sigmoid(jnp.dot(u, wgate_ref[:, b * D_MODEL:(b + 1) * D_MODEL],
                                preferred_element_type=F32))
        merged = merged + gate * jnp.dot(y, wo_ref[...], preferred_element_type=F32)
    m = jnp.dot(merged.astype(BF16), wout_ref[...], preferred_element_type=F32)
    o_ref[...] = x + mod_ref[0, 5:6, :] * m


def _merge_call(x, mod, ng, wgate, yssd_c, yssd_l, ygla_c, ygla_l, orw_c, orw_l, bonus, g, lnw, lnb,
                bd, wso, wgo, wro, wout):
    row_spec = pl.BlockSpec((RB, D_MODEL), lambda i: (i, 0))
    mix_spec = pl.BlockSpec((RB, D_MIX), lambda i: (i, 0))
    mix_c = pl.BlockSpec((RB, D_MIX), lambda i: (_ctx_idx(i), 0))
    mix_l = pl.BlockSpec((RB, D_MIX), lambda i: (_lat_idx(i), 0))
    t_c = pl.BlockSpec((1, D_MIX, RB), lambda i: (_ctx_idx(i), 0, 0))
    t_l = pl.BlockSpec((1, D_MIX, RB), lambda i: (_lat_idx(i), 0, 0))
    in_specs = [row_spec, pl.BlockSpec((1, N_MOD, D_MODEL), lambda i: (i, 0, 0)),
                _const_spec((1, D_MODEL)), _const_spec((D_MODEL, 3 * D_MODEL)),
                mix_c, mix_l, mix_c, mix_l, t_c, t_l, mix_spec, mix_spec,
                _const_spec((1, D_MIX)), _const_spec((1, D_MIX)), _const_spec((D_MIX, D_MIX)),
                _const_spec((D_MIX, D_MODEL)), _const_spec((D_MIX, D_MODEL)),
                _const_spec((D_MIX, D_MODEL)), _const_spec((D_MODEL, D_MODEL))]
    return pl.pallas_call(
        _merge_kernel,
        grid=(N_BLK,),
        in_specs=in_specs,
        out_specs=row_spec,
        out_shape=jax.ShapeDtypeStruct((N_TOK, D_MODEL), F32),
        compiler_params=pltpu.CompilerParams(dimension_semantics=("arbitrary",),
                                             vmem_limit_bytes=VMEM_LIMIT),
        name="merge",
    )(x, mod, ng, wgate, yssd_c, yssd_l, ygla_c, ygla_l, orw_c, orw_l, bonus, g, lnw, lnb, bd,
      wso, wgo, wro, wout)


def _grid_pos_embed(rows, cols, dim):
    quarter = dim // 4
    omega = 1.0 / (10000.0 ** (jnp.arange(quarter, dtype=F32) / quarter))
    er = jnp.arange(rows, dtype=F32)[:, None] * omega
    ec = jnp.arange(cols, dtype=F32)[:, None] * omega
    er = jnp.concatenate([jnp.sin(er), jnp.cos(er)], axis=-1)
    ec = jnp.concatenate([jnp.sin(ec), jnp.cos(ec)], axis=-1)
    emb = jnp.concatenate([jnp.broadcast_to(er[:, None], (rows, cols, dim // 2)),
                           jnp.broadcast_to(ec[None], (rows, cols, dim // 2))], axis=-1)
    return emb.reshape(rows * cols, dim)


def _pad_cols(a, n):
    return jnp.pad(a, [(0, 0)] * (a.ndim - 1) + [(0, n - a.shape[-1])])


def _head_pad(a):
    a = a.reshape(a.shape[:-1] + (GLA_H, GLA_DK))
    return _pad_cols(a, LANE).reshape(a.shape[:-2] + (GLA_QK,))


def _rows_at(a, off, n):
    return jnp.pad(a, ((off, n - off - a.shape[0]), (0, 0)))


def _block_diag_ones(n, blk):
    i = np.arange(n)
    return jnp.asarray((i[:, None] // blk) == (i[None, :] // blk), BF16)


def _expand01(row0):
    m = np.zeros((LANE, D_MIX), np.float32)
    for h in range(SSD_H):
        m[row0 + h, h * SSD_P:(h + 1) * SSD_P] = 1.0
    return jnp.asarray(m, BF16)


def kernel(x_prompt, x_sample, state_ssd, state_gla, state_rwkv, c, c_ctx, norm_g, w_ada, b_ada,
           ffn_gate, ffn_up, ffn_down, w_in, ssd_conv_w, ssd_conv_b, ssd_dt_bias, ssd_A_log, ssd_D,
           ssd_norm, w_ssd_o, gla_gk_w, gla_gk_b, gla_norm, w_gla_o, rw_mu, rw_w0, rw_w2, rw_a0,
           rw_a2, rw_g2, rw_kk, rw_ka, rw_rk, rw_ln_w, rw_ln_b, w_rw_o, w_out, final_norm):
    pos = _grid_pos_embed(DEC_SEQ // GRID_W, GRID_W, D_MODEL)
    x = (x_prompt.reshape(N_CTX_TOK, D_MODEL), x_sample.reshape(-1, D_MODEL))
    s0_rw = state_rwkv.reshape(DEC_BATCH, DEPTH, 2, RW_H, RW_VQ, RW_NV_LAT, RW_N).transpose(
        1, 2, 6, 5, 4, 0, 3).reshape(DEPTH, 2, RW_N, RW_NV_LAT, LANE)

    cond8 = jnp.concatenate([c_ctx[None], c, jnp.zeros((SUB - 1 - DEC_BATCH, D_MODEL), F32)])
    ada = _ada_call(cond8, w_ada, b_ada)
    cond_of_blk = np.concatenate([np.zeros(N_CTX_BLK, np.int32),
                                  1 + np.arange(N_BLK - N_CTX_BLK, dtype=np.int32) // LAT_BLK_PER_SEQ])

    bd = _block_diag_ones(D_MIX, RW_N)
    ef, eb = _expand01(0), _expand01(SSD_H)
    o_ssd = D_MIX + SSD_XBC + 2 * SSD_H
    o_gla = o_ssd + 2 * GLA_H * GLA_DK + 2 * D_MIX + 2 * GLA_LR
    o_rw = o_gla + 3 * D_MIX + 2 * RW_LW + RW_LA + RW_LG

    new_ssd, new_gla, new_rw = [], [], []
    for l in range(DEPTH):
        mod = ada[l][cond_of_blk].reshape(N_BLK, N_MOD, D_MODEL)
        ng = norm_g[l]
        wi = w_in[l]
        w_ssd = _pad_cols(wi[:, :o_ssd], SSD_W).astype(BF16)
        wg_ = wi[:, o_ssd:o_gla]
        w_gla = jnp.concatenate([_head_pad(wg_[:, :256]), _head_pad(wg_[:, 256:512]),
                                 wg_[:, 512:1536], _pad_cols(wg_[:, 1536:], LANE)], axis=1).astype(BF16)
        wr_ = wi[:, o_gla:o_rw]
        w_rw = jnp.concatenate([wr_[:, :1664], wr_[:, 1728:1856], _pad_cols(wr_[:, 1664:1728], LANE)],
                               axis=1).astype(BF16)
        mu_ = rw_mu[l]
        mu = jnp.concatenate([mu_[:1664], mu_[1728:1856], _pad_cols(mu_[1664:1728], LANE)])[None]
        w_gate = wi[:, o_rw:].astype(BF16)

        x = _ffn_call(x, mod, ng[0:1], ffn_gate[l, 0].astype(BF16), ffn_up[l, 0].astype(BF16),
                      ffn_down[l, 0].astype(BF16), mod_row=0, pos=pos if l == 0 else None)

        ssd_args = (x, mod, ng[1:2], w_ssd, ssd_conv_w[l], ssd_conv_b[l][None],
                    _pad_cols(ssd_dt_bias[l].reshape(1, -1), LANE),
                    _pad_cols(ssd_A_log[l].reshape(1, -1), LANE),
                    jnp.repeat(ssd_D[l], SSD_P, axis=1), ssd_norm[l][None], ef, eb)
        y_ssd_c, s_ssd = _ssd_call(*ssd_args, None, ctx=True)
        (y_ssd_l,) = _ssd_call(*ssd_args, state_ssd[:, l], ctx=False)
        new_ssd.append(s_ssd)

        gkw = jnp.stack([_rows_at(_head_pad(gla_gk_w[l, d]), d * GLA_LR, LANE) for d in range(2)])
        gla_args = (x, mod, ng[1:2], w_gla, gkw.astype(BF16), _head_pad(gla_gk_b[l]),
                    gla_norm[l][None])
        y_gla_c, s_gla = _gla_call(*gla_args, None, ctx=True)
        (y_gla_l,) = _gla_call(*gla_args, state_gla[:, l], ctx=False)
        new_gla.append(s_gla)

        w2p = jnp.stack([_rows_at(rw_w2[l, d], d * RW_LW, LANE) for d in range(2)]).astype(BF16)
        r, wf, wb, k, v, kk, nkka, g, bonus = _rw_prep_call(
            x, mod, ng[1:2], w_rw, mu, rw_a0[l][None], _rows_at(rw_a2[l], 0, LANE).astype(BF16),
            rw_g2[l].astype(BF16), rw_kk[l][None], rw_ka[l][None], rw_rk[l].reshape(1, D_MIX),
            rw_w0[l], w2p, bd)
        rc, kc, kkc, ac, wc = _rl_k_call(r, k, kk, nkka, wf, wb, lat=False)
        o_c, s_rw = _rw_scan_call(rc, wc, kc, _rl_v_call(v, lat=False), kkc, ac, None,
                                  want_final=True)
        new_rw.append(s_rw.reshape(2, RW_N, RW_N, BATCH, RW_H).transpose(3, 0, 4, 2, 1))
        rl, kl, kkl, al, wl = _rl_k_call(r, k, kk, nkka, wf, wb, lat=True)
        (o_l,) = _rw_scan_call(rl, wl, kl, _rl_v_call(v, lat=True), kkl, al, s0_rw[l],
                               want_final=False)

        x = _merge_call(x, mod, ng[1:2], w_gate, y_ssd_c, y_ssd_l, y_gla_c, y_gla_l,
                        _rl_out_call(o_c, lat=False), _rl_out_call(o_l, lat=True), bonus, g,
                        rw_ln_w[l][None], rw_ln_b[l][None], bd, w_ssd_o[l].astype(BF16),
                        w_gla_o[l].astype(BF16), w_rw_o[l].astype(BF16), w_out[l].astype(BF16))

        x = _ffn_call(x, mod, ng[2:3], ffn_gate[l, 1].astype(BF16), ffn_up[l, 1].astype(BF16),
                      ffn_down[l, 1].astype(BF16), mod_row=6,
                      final_g=final_norm[None] if l == DEPTH - 1 else None)

    y_prompt = x[0].reshape(BATCH, SEQ, D_MODEL)
    y_sample = x[1].reshape(DEC_BATCH, DEC_SEQ, D_MODEL)
    return (y_prompt, y_sample, jnp.stack(new_ssd, axis=1), jnp.stack(new_gla, axis=1),
            jnp.stack(new_rw, axis=1))
```

```python
import functools
import math

import numpy as np
import jax
import jax.numpy as jnp
from jax import lax
from jax.experimental import pallas as pl
from jax.experimental.pallas import tpu as pltpu

F32 = jnp.float32
BF16 = jnp.bfloat16

D_MODEL = 1024
BATCH = 16
SEQ = 256
DEPTH = 2
DEC_BATCH = 4
DEC_SEQ = 1024
GRID_W = 64
D_MIX = 512
D_FF = 2816
N_MOD = 9
SSD_P = 64
SSD_H = 8
SSD_N = 64
SSD_G = 2
SSD_XBC = 768
GLA_H = 4
GLA_DK = 64
GLA_DV = 128
GLA_LR = 16
GLA_GATE_NORM = 16.0
GLA_CHUNK = 64
RW_N = 64
RW_H = 8
RW_LW = 64
RW_LA = 64
RW_LG = 128
RMS_EPS = 1e-6
RW_GN_EPS = 64e-5

LANE = 128
SUB = 8
RB = 256
N_CTX_TOK = BATCH * SEQ
N_TOK = N_CTX_TOK + DEC_BATCH * DEC_SEQ
N_BLK = N_TOK // RB
N_CTX_BLK = N_CTX_TOK // RB
LAT_BLK_PER_SEQ = DEC_SEQ // RB
SSD_W = D_MIX + SSD_XBC + LANE
GLA_QK = GLA_H * GLA_DK
GLA_W = 2 * GLA_QK + 2 * D_MIX + LANE
RW_W = 3 * D_MIX + 3 * LANE
SCAN_TB = 32
FFN_SUB = 1
VMEM_LIMIT = 56 * 1024 * 1024


def _dot(a, b):
    return jnp.dot(a.astype(BF16), b.astype(BF16), preferred_element_type=F32)


def _dot_nt(a, b):
    return lax.dot_general(a.astype(BF16), b.astype(BF16), (((1,), (1,)), ((), ())),
                           preferred_element_type=F32)


def _split3(x):
    hi = x.astype(BF16)
    r1 = x - hi.astype(F32)
    mid = r1.astype(BF16)
    lo = (r1 - mid.astype(F32)).astype(BF16)
    return hi, mid, lo


def _sel_dot(m01, x):
    hi, mid, lo = _split3(x)
    f = lambda p: jnp.dot(m01, p, preferred_element_type=F32)
    return f(hi) + f(mid) + f(lo)


def _dot_sel(x, m01):
    hi, mid, lo = _split3(x)
    f = lambda p: jnp.dot(p, m01, preferred_element_type=F32)
    return f(hi) + f(mid) + f(lo)


def _sigmoid(x):
    return 0.5 * jnp.tanh(0.5 * x) + 0.5


def _silu(x):
    return x * _sigmoid(x)


def _softplus(x):
    return jnp.maximum(x, 0.0) + jnp.log(1.0 + jnp.exp(-jnp.abs(x)))


def _rmsnorm(x, g):
    return x * lax.rsqrt(jnp.mean(x * x, axis=-1, keepdims=True) + RMS_EPS) * g


def _norm_mod(x, g, shift, scale):
    return _rmsnorm(x, g) * (1.0 + scale) + shift


def _iota(shape, dim):
    return lax.broadcasted_iota(jnp.int32, shape, dim)


def _tri01(n, lower, chunk=None):
    t = _iota((n, n), 0)
    s = _iota((n, n), 1)
    m = (s <= t) if lower else (s >= t)
    if chunk is not None:
        m = m & ((t // chunk) == (s // chunk))
    return m


def _ada_kernel(c_ref, w_ref, b_ref, o_ref):
    o_ref[0] = _dot(_silu(c_ref[...]), w_ref[0]) + b_ref[0]


def _ada_call(cond8, w_ada, b_ada):
    tn = 1024
    nj = (N_MOD * D_MODEL) // tn
    return pl.pallas_call(
        _ada_kernel,
        grid=(DEPTH, nj),
        in_specs=[pl.BlockSpec((SUB, D_MODEL), lambda l, j: (0, 0)),
                  pl.BlockSpec((1, D_MODEL, tn), lambda l, j: (l, 0, j)),
                  pl.BlockSpec((1, 1, tn), lambda l, j: (l, 0, j))],
        out_specs=pl.BlockSpec((1, SUB, tn), lambda l, j: (l, 0, j)),
        out_shape=jax.ShapeDtypeStruct((DEPTH, SUB, N_MOD * D_MODEL), F32),
        compiler_params=pltpu.CompilerParams(dimension_semantics=("arbitrary", "arbitrary"),
                                             vmem_limit_bytes=VMEM_LIMIT),
        name="ada",
    )(cond8, w_ada, b_ada.reshape(DEPTH, 1, N_MOD * D_MODEL))


def _ffn_kernel(*refs, mod_row, first, last):
    it = iter(refs)
    is_ctx = pl.program_id(0) < N_CTX_BLK // FFN_SUB
    if first:
        xc_ref, xl_ref, pos_ref = next(it), next(it), next(it)
        x = jnp.where(is_ctx, xc_ref[...], xl_ref[...] + pos_ref[...])
    else:
        x = next(it)[...]
    mod_ref, ng_ref, wg_ref, wu_ref, wd_ref = next(it), next(it), next(it), next(it), next(it)
    fin_ref = next(it) if last else None
    shift = mod_ref[0, mod_row:mod_row + 1, :]
    scale = mod_ref[0, mod_row + 1:mod_row + 2, :]
    gate = mod_ref[0, mod_row + 2:mod_row + 3, :]
    h = _norm_mod(x, ng_ref[...], shift, scale).astype(BF16)
    a = _silu(jnp.dot(h, wg_ref[...], preferred_element_type=F32))
    a = (a * jnp.dot(h, wu_ref[...], preferred_element_type=F32)).astype(BF16)
    y = x + 0.5 * gate * jnp.dot(a, wd_ref[...], preferred_element_type=F32)
    if last:
        y = _rmsnorm(y, fin_ref[...])
        oc_ref, ol_ref = next(it), next(it)

        @pl.when(is_ctx)
        def _():
            oc_ref[...] = y

        @pl.when(jnp.logical_not(is_ctx))
        def _():
            ol_ref[...] = y
    else:
        next(it)[...] = y


def _const_spec(shape):
    nd = len(shape)
    return pl.BlockSpec(shape, lambda i: (0,) * nd, pipeline_mode=pl.Buffered(1))


def _ctx_idx(i):
    return jnp.minimum(i, N_CTX_BLK - 1)


def _lat_idx(i):
    return jnp.maximum(i - N_CTX_BLK, 0)


def _ffn_call(x, mod, ng, wg, wu, wd, *, mod_row, pos=None, final_g=None):
    first = pos is not None
    last = final_g is not None
    fb = FFN_SUB * RB
    nctx = N_CTX_BLK // FFN_SUB
    row_spec = pl.BlockSpec((fb, D_MODEL), lambda i: (i, 0))
    ctx_spec = pl.BlockSpec((fb, D_MODEL), lambda i: (jnp.minimum(i, nctx - 1), 0))
    lat_spec = pl.BlockSpec((fb, D_MODEL), lambda i: (jnp.maximum(i - nctx, 0), 0))
    if first:
        in_specs = [ctx_spec, lat_spec,
                    pl.BlockSpec((fb, D_MODEL),
                                 lambda i: (jnp.maximum(i - nctx, 0) % (DEC_SEQ // fb), 0))]
        args = [x[0], x[1], pos]
    else:
        in_specs = [row_spec]
        args = [x]
    in_specs += [pl.BlockSpec((1, N_MOD, D_MODEL), lambda i: (FFN_SUB * i, 0, 0)),
                 _const_spec((1, D_MODEL)), _const_spec((D_MODEL, D_FF)),
                 _const_spec((D_MODEL, D_FF)), _const_spec((D_FF, D_MODEL))]
    args += [mod, ng, wg, wu, wd]
    if last:
        in_specs.append(_const_spec((1, D_MODEL)))
        args.append(final_g)
        half = jax.ShapeDtypeStruct((N_TOK // 2, D_MODEL), F32)
        out_specs, out_shape = [ctx_spec, lat_spec], [half, half]
    else:
        out_specs, out_shape = row_spec, jax.ShapeDtypeStruct((N_TOK, D_MODEL), F32)
    return pl.pallas_call(
        functools.partial(_ffn_kernel, mod_row=mod_row, first=first, last=last),
        grid=(N_BLK // FFN_SUB,),
        in_specs=in_specs,
        out_specs=out_specs,
        out_shape=out_shape,
        compiler_params=pltpu.CompilerParams(dimension_semantics=("arbitrary",),
                                             vmem_limit_bytes=VMEM_LIMIT),
        name="ffn",
    )(*args)


def _ssd_kernel(*refs, L, has_init, want_final):
    it = iter(refs)
    x_ref, mod_ref, ng_ref, w_ref = next(it), next(it), next(it), next(it)
    cw_ref, cb_ref, dtb_ref, alog_ref, d_ref, nw_ref = (next(it), next(it), next(it), next(it),
                                                        next(it), next(it))
    ef_ref, eb_ref = next(it), next(it)
    s0_ref = next(it) if has_init else None
    y_ref = next(it)
    sfin_ref = next(it) if want_final else None
    yacc = next(it)

    C = RB
    nc = L // C
    x = x_ref[...]
    u = _norm_mod(x, ng_ref[...], mod_ref[0, 3:4, :], mod_ref[0, 4:5, :]).astype(BF16)
    p = jnp.dot(u, w_ref[...], preferred_element_type=F32)
    z = p[:, :D_MIX]
    xbc = p[:, D_MIX:D_MIX + SSD_XBC]
    dtp = p[:, D_MIX + SSD_XBC:]
    rows = _iota((L, 1), 0)
    prev = jnp.where(rows == 0, 0.0, pltpu.roll(xbc, 1, 0))
    nxt = jnp.where(rows == L - 1, 0.0, pltpu.roll(xbc, L - 1, 0))
    xc = cb_ref[...] + prev * cw_ref[0:1, :] + xbc * cw_ref[1:2, :] + nxt * cw_ref[2:3, :]
    xc = _silu(xc)
    xs = xc[:, :D_MIX]
    bm = xc[:, D_MIX:D_MIX + SSD_G * SSD_N]
    cm = xc[:, D_MIX + SSD_G * SSD_N:]
    dt = _softplus(dtp + dtb_ref[...])
    adt = dt * (-jnp.exp(alog_ref[...]))

    tril = _tri01(C, True)
    triu = _tri01(C, False)
    tril_b = jnp.where(tril, 1.0, 0.0).astype(BF16)
    triu_b = jnp.where(triu, 1.0, 0.0).astype(BF16)
    ef = ef_ref[...]
    eb = eb_ref[...]

    cs, csT, rcs, rcsT, dtT, loc_f, loc_b = [], [], [], [], [], [], []
    for c in range(nc):
        a_c = adt[c * C:(c + 1) * C]
        a_cT = a_c.T
        cs.append(_sel_dot(tril_b, a_c))
        rcs.append(_sel_dot(triu_b, a_c))
        csT.append(_dot_sel(a_cT, triu_b))
        rcsT.append(_dot_sel(a_cT, tril_b))
        dtT.append(dt[c * C:(c + 1) * C].T)

    need_states = want_final or nc > 1
    if need_states:
        for c in range(nc):
            xs_c = xs[c * C:(c + 1) * C]
            dt_c = dt[c * C:(c + 1) * C]
            wf = jnp.exp(cs[c][C - 1:C, :] - cs[c]) * dt_c
            wb = jnp.exp(rcs[c][0:1, :] - rcs[c]) * dt_c
            xwf = (xs_c * _dot_sel(wf, ef)).T
            xwb = (xs_c * _dot_sel(wb, eb)).T
            lf, lb = [], []
            for h in range(SSD_H):
                g = h // (SSD_H // SSD_G)
                bm_g = bm[c * C:(c + 1) * C, g * SSD_N:(g + 1) * SSD_N]
                lf.append(_dot(xwf[h * SSD_P:(h + 1) * SSD_P, :], bm_g))
                lb.append(_dot(xwb[h * SSD_P:(h + 1) * SSD_P, :], bm_g))
            loc_f.append(lf)
            loc_b.append(lb)

    zero_s = jnp.zeros((SSD_P, SSD_N), F32)
    sin_f = [[None] * SSD_H for _ in range(nc + 1)]
    sin_b = [[None] * SSD_H for _ in range(nc + 1)]
    for h in range(SSD_H):
        sin_f[0][h] = s0_ref[0, 0, h] if has_init else zero_s
        sin_b[nc][h] = s0_ref[0, 1, h] if has_init else zero_s
    if need_states:
        for c in range(nc):
            dec = jnp.exp(cs[c][C - 1:C, :])
            for h in range(SSD_H):
                sin_f[c + 1][h] = dec[:, h:h + 1] * sin_f[c][h] + loc_f[c][h]
        for c in range(nc - 1, -1, -1):
            dec = jnp.exp(rcs[c][0:1, :])
            for h in range(SSD_H):
                sin_b[c][h] = dec[:, SSD_H + h:SSD_H + h + 1] * sin_b[c + 1][h] + loc_b[c][h]

    for c in range(nc):
        sl = slice(c * C, (c + 1) * C)
        ecs = jnp.exp(cs[c])
        ercs = jnp.exp(rcs[c])
        for h in range(SSD_H):
            g = h // (SSD_H // SSD_G)
            cm_g = cm[sl, g * SSD_N:(g + 1) * SSD_N]
            bm_g = bm[sl, g * SSD_N:(g + 1) * SSD_N]
            cb = _dot_nt(cm_g, bm_g)
            lf = jnp.exp(jnp.where(tril, cs[c][:, h:h + 1] - csT[c][h:h + 1, :], -jnp.inf))
            lb = jnp.exp(jnp.where(triu, rcs[c][:, SSD_H + h:SSD_H + h + 1]
                                   - rcsT[c][SSD_H + h:SSD_H + h + 1, :], -jnp.inf))
            m = cb * (lf * dtT[c][h:h + 1, :] + lb * dtT[c][SSD_H + h:SSD_H + h + 1, :])
            y_h = _dot(m, xs[sl, h * SSD_P:(h + 1) * SSD_P])
            if has_init or nc > 1:
                y_h = y_h + ecs[:, h:h + 1] * _dot_nt(cm_g, sin_f[c][h])
                y_h = y_h + ercs[:, SSD_H + h:SSD_H + h + 1] * _dot_nt(cm_g, sin_b[c + 1][h])
            yacc[sl, h * SSD_P:(h + 1) * SSD_P] = y_h

    y = yacc[...] + xs * (d_ref[0:1, :] + d_ref[1:2, :])
    y = _rmsnorm(y * _silu(z), nw_ref[...])
    y_ref[...] = y.astype(BF16)
    if want_final:
        for h in range(SSD_H):
            sfin_ref[0, 0, h] = sin_f[nc][h]
            sfin_ref[0, 1, h] = sin_b[0][h]


def _ssd_call(x, mod, ng, w, cw, cb, dtb, alog, dexp, nw, ef, eb, s0, *, ctx):
    L = SEQ if ctx else DEC_SEQ
    nseq = BATCH if ctx else DEC_BATCH
    blk0 = 0 if ctx else N_CTX_TOK // L
    mod_of = (lambda i: (i, 0, 0)) if ctx else (lambda i: (N_CTX_BLK + LAT_BLK_PER_SEQ * i, 0, 0))
    in_specs = [pl.BlockSpec((L, D_MODEL), lambda i: (blk0 + i, 0)),
                pl.BlockSpec((1, N_MOD, D_MODEL), mod_of),
                _const_spec((1, D_MODEL)), _const_spec((D_MODEL, SSD_W)),
                _const_spec((3, SSD_XBC)), _const_spec((1, SSD_XBC)),
                _const_spec((1, LANE)), _const_spec((1, LANE)),
                _const_spec((2, D_MIX)), _const_spec((1, D_MIX)),
                _const_spec((LANE, D_MIX)), _const_spec((LANE, D_MIX))]
    args = [x, mod, ng, w, cw, cb, dtb, alog, dexp, nw, ef, eb]
    st_spec = pl.BlockSpec((1, 2, SSD_H, SSD_P, SSD_N), lambda i: (i, 0, 0, 0, 0))
    out_specs = [pl.BlockSpec((L, D_MIX), lambda i: (i, 0))]
    out_shape = [jax.ShapeDtypeStruct((nseq * L, D_MIX), BF16)]
    if ctx:
        out_specs.append(st_spec)
        out_shape.append(jax.ShapeDtypeStruct((nseq, 2, SSD_H, SSD_P, SSD_N), F32))
    else:
        in_specs.append(st_spec)
        args.append(s0)
    return pl.pallas_call(
        functools.partial(_ssd_kernel, L=L, has_init=not ctx, want_final=ctx),
        grid=(nseq,),
        in_specs=in_specs,
        out_specs=out_specs,
        out_shape=out_shape,
        scratch_shapes=[pltpu.VMEM((L, D_MIX), F32)],
        compiler_params=pltpu.CompilerParams(dimension_semantics=("arbitrary",),
                                             vmem_limit_bytes=VMEM_LIMIT),
        name="ssd_ctx" if ctx else "ssd_lat",
    )(*args)


def _gla_kernel(*refs, L, has_init, want_final):
    it = iter(refs)
    x_ref, mod_ref, ng_ref, w_ref = next(it), next(it), next(it), next(it)
    gkw_ref, gkb_ref, nw_ref = next(it), next(it), next(it)
    s0_ref = next(it) if has_init else None
    y_ref = next(it)
    sfin_ref = next(it) if want_final else None
    oacc = next(it)

    C = GLA_CHUNK
    B = RB
    cpb = B // C
    nb = L // B
    x = x_ref[...]
    u = _norm_mod(x, ng_ref[...], mod_ref[0, 3:4, :], mod_ref[0, 4:5, :]).astype(BF16)
    p = jnp.dot(u, w_ref[...], preferred_element_type=F32)
    q = p[:, :GLA_QK] * (GLA_DK ** -0.5)
    k = p[:, GLA_QK:2 * GLA_QK]
    v = p[:, 2 * GLA_QK:2 * GLA_QK + D_MIX]
    gg = p[:, 2 * GLA_QK + D_MIX:2 * GLA_QK + 2 * D_MIX]
    lr = p[:, 2 * GLA_QK + 2 * D_MIX:].astype(BF16)
    lg = []
    for d in range(2):
        pre = jnp.dot(lr, gkw_ref[d], preferred_element_type=F32) + gkb_ref[d:d + 1, :]
        lg.append(-_softplus(-pre) / GLA_GATE_NORM)

    low = _tri01(B, True, C)
    upp = _tri01(B, False, C)
    low_b = jnp.where(low, 1.0, 0.0).astype(BF16)
    upp_b = jnp.where(upp, 1.0, 0.0).astype(BF16)
    rowi = _iota((B, 1), 0)

    if has_init:
        s_f = [s0_ref[0, 0, h] for h in range(GLA_H)]
        s_b = [s0_ref[0, 1, h] for h in range(GLA_H)]
    else:
        s_f = [jnp.zeros((GLA_DK, GLA_DV), F32)] * GLA_H
        s_b = list(s_f)

    hpp = LANE // GLA_DK
    lane_head = _iota((1, LANE), 1) // GLA_DK

    def pair(h):
        return slice((h // hpp) * LANE, (h // hpp + 1) * LANE)

    def only(h, a):
        return jnp.where(lane_head == h % hpp, a[:, pair(h)], 0.0)

    def pair_state(s, h):
        j = h // hpp
        return jnp.concatenate(s[j * hpp:(j + 1) * hpp], axis=0)

    blocks = []
    for b in range(nb):
        sl = slice(b * B, (b + 1) * B)
        bf = _sel_dot(low_b, lg[0][sl])
        rb = _sel_dot(upp_b, lg[1][sl])
        q_b, k_b = q[sl], k[sl]
        qf, qb = q_b * jnp.exp(bf), q_b * jnp.exp(rb)
        blocks.append(dict(sl=sl, bf=bf, rb=rb, bfT=bf.T, rbT=rb.T, k=k_b,
                           kf=k_b * jnp.exp(-bf), kb=k_b * jnp.exp(-rb),
                           qf=[only(h, qf) for h in range(GLA_H)],
                           qb=[only(h, qb) for h in range(GLA_H)]))

    for blk in blocks:
        sl = blk["sl"]
        for h in range(GLA_H):
            a = jnp.where(low, _dot_nt(blk["qf"][h], blk["kf"][:, pair(h)]), 0.0)
            a = a + jnp.where(upp, _dot_nt(blk["qb"][h], blk["kb"][:, pair(h)]), 0.0)
            oacc[sl, h * GLA_DV:(h + 1) * GLA_DV] = _dot(a, v[sl, h * GLA_DV:(h + 1) * GLA_DV])

    def inter(blk, c, qd, cum, cumT, edge, s):
        sl = blk["sl"]
        rs = slice(sl.start + c * C, sl.start + (c + 1) * C)
        kdecT = (blk["k"] * jnp.exp(cum[edge:edge + 1, :] - cum)).T
        in_c = (rowi >= c * C) & (rowi < (c + 1) * C)
        s_in = [pair_state(s, h) for h in range(0, GLA_H, hpp)]
        for h in range(GLA_H):
            ks = slice(h * GLA_DK, (h + 1) * GLA_DK)
            vs = slice(h * GLA_DV, (h + 1) * GLA_DV)
            oacc[rs, vs] = oacc[rs, vs] + _dot(qd[h][c * C:(c + 1) * C], s_in[h // hpp])
            kv = _dot(kdecT[ks, :], jnp.where(in_c, v[sl, vs], 0.0))
            s[h] = jnp.exp(cumT[ks, edge:edge + 1]) * s[h] + kv

    for blk in blocks:
        for c in range(cpb):
            inter(blk, c, blk["qf"], blk["bf"], blk["bfT"], (c + 1) * C - 1, s_f)
    for blk in reversed(blocks):
        for c in range(cpb - 1, -1, -1):
            inter(blk, c, blk["qb"], blk["rb"], blk["rbT"], c * C, s_b)

    for h in range(GLA_H):
        vs = slice(h * GLA_DV, (h + 1) * GLA_DV)
        o_h = _rmsnorm(oacc[:, vs], nw_ref[...])
        y_ref[:, vs] = (o_h * _silu(gg[:, vs])).astype(BF16)
        if want_final:
            sfin_ref[0, 0, h] = s_f[h]
            sfin_ref[0, 1, h] = s_b[h]


def _gla_call(x, mod, ng, w, gkw, gkb, nw, s0, *, ctx):
    L = SEQ if ctx else DEC_SEQ
    nseq = BATCH if ctx else DEC_BATCH
    blk0 = 0 if ctx else N_CTX_TOK // L
    mod_of = (lambda i: (i, 0, 0)) if ctx else (lambda i: (N_CTX_BLK + LAT_BLK_PER_SEQ * i, 0, 0))
    in_specs = [pl.BlockSpec((L, D_MODEL), lambda i: (blk0 + i, 0)),
                pl.BlockSpec((1, N_MOD, D_MODEL), mod_of),
                _const_spec((1, D_MODEL)), _const_spec((D_MODEL, GLA_W)),
                _const_spec((2, LANE, GLA_QK)), _const_spec((2, GLA_QK)),
                _const_spec((1, GLA_DV))]
    args = [x, mod, ng, w, gkw, gkb, nw]
    st_spec = pl.BlockSpec((1, 2, GLA_H, GLA_DK, GLA_DV), lambda i: (i, 0, 0, 0, 0))
    out_specs = [pl.BlockSpec((L, D_MIX), lambda i: (i, 0))]
    out_shape = [jax.ShapeDtypeStruct((nseq * L, D_MIX), BF16)]
    if ctx:
        out_specs.append(st_spec)
        out_shape.append(jax.ShapeDtypeStruct((nseq, 2, GLA_H, GLA_DK, GLA_DV), F32))
    else:
        in_specs.append(st_spec)
        args.append(s0)
    return pl.pallas_call(
        functools.partial(_gla_kernel, L=L, has_init=not ctx, want_final=ctx),
        grid=(nseq,),
        in_specs=in_specs,
        out_specs=out_specs,
        out_shape=out_shape,
        scratch_shapes=[pltpu.VMEM((L, D_MIX), F32)],
        compiler_params=pltpu.CompilerParams(dimension_semantics=("arbitrary",),
                                             vmem_limit_bytes=VMEM_LIMIT),
        name="gla_ctx" if ctx else "gla_lat",
    )(*args)


def _rw_prep_kernel(x_ref, xp_ref, xn_ref, mod_ref, ng_ref, w_ref, mu_ref, a0_ref, a2_ref, g2_ref,
                    kkw_ref, ka_ref, rk_ref, w0_ref, w2_ref, bd_ref,
                    r_ref, wf_ref, wb_ref, k_ref, v_ref, kk_ref, nkka_ref, g_ref, bonus_ref):
    i = pl.program_id(0)
    j = (i - N_CTX_BLK) % LAT_BLK_PER_SEQ
    is_first = (i < N_CTX_BLK) | (j == 0)
    is_last = (i < N_CTX_BLK) | (j == LAT_BLK_PER_SEQ - 1)
    x_all = jnp.concatenate([x_ref[...], xp_ref[...], xn_ref[...]], axis=0)
    u_all = _norm_mod(x_all, ng_ref[...], mod_ref[0, 3:4, :], mod_ref[0, 4:5, :]).astype(BF16)
    p_all = jnp.dot(u_all, w_ref[...], preferred_element_type=F32)
    p = p_all[:RB]
    p_prev = jnp.where(is_first, 0.0, p_all[RB + SUB - 1:RB + SUB, :])
    p_next = jnp.where(is_last, 0.0, p_all[RB + SUB:RB + SUB + 1, :])
    rows = _iota((RB, 1), 0)
    prev = jnp.where(rows == 0, p_prev, pltpu.roll(p, 1, 0))
    nxt = jnp.where(rows == RB - 1, p_next, pltpu.roll(p, RB - 1, 0))
    p = p + (0.5 * (prev + nxt) - p) * mu_ref[...]

    r = p[:, :D_MIX]
    k = p[:, D_MIX:2 * D_MIX]
    v = p[:, 2 * D_MIX:3 * D_MIX]
    wlr = p[:, 3 * D_MIX:3 * D_MIX + LANE]
    glr = p[:, 3 * D_MIX + LANE:3 * D_MIX + 2 * LANE]
    alr = p[:, 3 * D_MIX + 2 * LANE:]
    bd = bd_ref[...]
    a = _sigmoid(a0_ref[...] + _dot(alr, a2_ref[...]))
    g = _dot(_sigmoid(glr), g2_ref[...])
    kk = k * kkw_ref[...]
    kk = kk / jnp.maximum(jnp.sqrt(_dot_sel(kk * kk, bd)), 1e-12)
    k = k * (1.0 + (a - 1.0) * ka_ref[...])
    tw = jnp.tanh(wlr).astype(BF16)
    for d, o_ref in ((0, wf_ref), (1, wb_ref)):
        pre = w0_ref[d:d + 1, :] + jnp.dot(tw, w2_ref[d], preferred_element_type=F32)
        o_ref[0] = jnp.exp(-math.exp(-0.5) * _sigmoid(pre)).T
    r_ref[0] = r.T
    k_ref[0] = k.T
    v_ref[0] = v.T
    kk_ref[0] = kk.T
    nkka_ref[0] = (-(kk * a)).T
    g_ref[...] = g
    bonus_ref[...] = _dot_sel(r * k * rk_ref[...], bd) * v


def _rw_prep_call(x, mod, ng, w, mu, a0, a2, g2, kkw, ka, rk, w0, w2, bd):
    hb = RB // SUB
    row_spec = pl.BlockSpec((RB, D_MODEL), lambda i: (i, 0))
    out_spec = pl.BlockSpec((RB, D_MIX), lambda i: (i, 0))
    in_specs = [row_spec,
                pl.BlockSpec((SUB, D_MODEL), lambda i: (jnp.maximum(i * hb - 1, 0), 0)),
                pl.BlockSpec((SUB, D_MODEL), lambda i: (jnp.minimum((i + 1) * hb, N_TOK // SUB - 1), 0)),
                pl.BlockSpec((1, N_MOD, D_MODEL), lambda i: (i, 0, 0)),
                _const_spec((1, D_MODEL)), _const_spec((D_MODEL, RW_W)), _const_spec((1, RW_W)),
                _const_spec((1, D_MIX)), _const_spec((LANE, D_MIX)), _const_spec((LANE, D_MIX)),
                _const_spec((1, D_MIX)), _const_spec((1, D_MIX)), _const_spec((1, D_MIX)),
                _const_spec((2, D_MIX)), _const_spec((2, LANE, D_MIX)), _const_spec((D_MIX, D_MIX))]
    t_spec = pl.BlockSpec((1, D_MIX, RB), lambda i: (i, 0, 0))
    t_shape = jax.ShapeDtypeStruct((N_BLK, D_MIX, RB), F32)
    tok_shape = jax.ShapeDtypeStruct((N_TOK, D_MIX), F32)
    return pl.pallas_call(
        _rw_prep_kernel,
        grid=(N_BLK,),
        in_specs=in_specs,
        out_specs=[t_spec] * 7 + [out_spec] * 2,
        out_shape=[t_shape] * 7 + [tok_shape] * 2,
        compiler_params=pltpu.CompilerParams(dimension_semantics=("arbitrary",),
                                             vmem_limit_bytes=VMEM_LIMIT),
        name="rw_prep",
    )(x, x, x, mod, ng, w, mu, a0, a2, g2, kkw, ka, rk, w0, w2, bd)


RW_VQ = LANE // (DEC_BATCH * RW_H)
RW_NV_LAT = RW_N // RW_VQ


def _rl_rows(ref, nl, lat):
    x = ref[:, 0, :, nl, :] if lat else ref[:, :, nl, :]
    return x.reshape(-1, RB)


def _rl_k_kernel(r_ref, k_ref, kk_ref, a_ref, wf_ref, wb_ref, ro, ko, kko, ao, wo, *, lat):
    def lanes(ref, nl):
        x = _rl_rows(ref, nl, lat)
        if lat:
            x = jnp.concatenate([x] * RW_VQ, axis=0)
        return x.T

    for nl in range(SUB):
        ro[nl] = lanes(r_ref, nl)
        ko[nl] = lanes(k_ref, nl)
        kko[nl] = lanes(kk_ref, nl)
        ao[nl] = lanes(a_ref, nl)
        wo[0, nl] = lanes(wf_ref, nl)
        wo[1, nl] = lanes(wb_ref, nl)


def _rl_k_call(r, k, kk, nkka, wf, wb, *, lat):
    nb = RW_N // SUB
    if lat:
        L = DEC_SEQ
        view = (N_BLK // LAT_BLK_PER_SEQ, LAT_BLK_PER_SEQ, RW_H, RW_N, RB)
        grid = (LAT_BLK_PER_SEQ, nb)
        ispec = pl.BlockSpec((DEC_BATCH, 1, RW_H, SUB, RB), lambda j, n: (1, j, 0, n, 0))
        ospec = pl.BlockSpec((SUB, RB, LANE), lambda j, n: (n, j, 0))
        wspec = pl.BlockSpec((2, SUB, RB, LANE), lambda j, n: (0, n, j, 0))
    else:
        L = SEQ
        view = (N_BLK, RW_H, RW_N, RB)
        grid = (nb,)
        ispec = pl.BlockSpec((BATCH, RW_H, SUB, RB), lambda n: (0, 0, n, 0))
        ospec = pl.BlockSpec((SUB, RB, LANE), lambda n: (n, 0, 0))
        wspec = pl.BlockSpec((2, SUB, RB, LANE), lambda n: (0, n, 0, 0))
    kshape = jax.ShapeDtypeStruct((RW_N, L, LANE), F32)
    return pl.pallas_call(
        functools.partial(_rl_k_kernel, lat=lat),
        grid=grid,
        in_specs=[ispec] * 6,
        out_specs=[ospec] * 4 + [wspec],
        out_shape=[kshape] * 4 + [jax.ShapeDtypeStruct((2, RW_N, L, LANE), F32)],
        compiler_params=pltpu.CompilerParams(dimension_semantics=("arbitrary",) * len(grid),
                                             vmem_limit_bytes=VMEM_LIMIT),
        name="rl_k_lat" if lat else "rl_k_ctx",
    )(*[a.reshape(view) for a in (r, k, kk, nkka, wf, wb)])


def _rl_v_kernel(*refs, lat):
    o_ref = refs[-1]
    for nl in range(SUB):
        x = jnp.concatenate([_rl_rows(ref, nl, lat) for ref in refs[:-1]], axis=0) if lat \
            else _rl_rows(refs[0], nl, lat)
        o_ref[:, nl, :] = x.T


def _rl_v_call(v, *, lat):
    if lat:
        view = (N_BLK // LAT_BLK_PER_SEQ, LAT_BLK_PER_SEQ, RW_H, RW_N, RB)
        nvb = RW_NV_LAT // SUB
        grid = (LAT_BLK_PER_SEQ, nvb)
        ispecs = [pl.BlockSpec((DEC_BATCH, 1, RW_H, SUB, RB),
                               functools.partial(lambda j, n, q: (1, j, 0, q * nvb + n, 0), q=q))
                  for q in range(RW_VQ)]
        ospec = pl.BlockSpec((RB, SUB, LANE), lambda j, n: (j, n, 0))
        oshape = jax.ShapeDtypeStruct((DEC_SEQ, RW_NV_LAT, LANE), F32)
    else:
        view = (N_BLK, RW_H, RW_N, RB)
        grid = (RW_N // SUB,)
        ispecs = [pl.BlockSpec((BATCH, RW_H, SUB, RB), lambda n: (0, 0, n, 0))]
        ospec = pl.BlockSpec((RB, SUB, LANE), lambda n: (0, n, 0))
        oshape = jax.ShapeDtypeStruct((SEQ, RW_N, LANE), F32)
    vv = v.reshape(view)
    return pl.pallas_call(
        functools.partial(_rl_v_kernel, lat=lat),
        grid=grid,
        in_specs=ispecs,
        out_specs=ospec,
        out_shape=oshape,
        compiler_params=pltpu.CompilerParams(dimension_semantics=("arbitrary",) * len(grid),
                                             vmem_limit_bytes=VMEM_LIMIT),
        name="rl_v_lat" if lat else "rl_v_ctx",
    )(*([vv] * len(ispecs)))


def _rl_out_kernel(o_ref, ot_ref, *, lat):
    for nl in range(SUB):
        x = (o_ref[0, :, nl, :] + o_ref[1, :, nl, :]).T
        if lat:
            rows = DEC_BATCH * RW_H
            for q in range(RW_VQ):
                ot_ref[:, 0, :, q, nl, :] = x[q * rows:(q + 1) * rows].reshape(DEC_BATCH, RW_H, RB)
        else:
            ot_ref[:, :, nl, :] = x.reshape(BATCH, RW_H, RB)


def _rl_out_call(o, *, lat):
    if lat:
        grid = (LAT_BLK_PER_SEQ, RW_NV_LAT // SUB)
        ispec = pl.BlockSpec((2, RB, SUB, LANE), lambda j, n: (0, j, n, 0))
        ospec = pl.BlockSpec((DEC_BATCH, 1, RW_H, RW_VQ, SUB, RB), lambda j, n: (0, j, 0, 0, n, 0))
        oshape = (DEC_BATCH, LAT_BLK_PER_SEQ, RW_H, RW_VQ, RW_NV_LAT, RB)
    else:
        grid = (RW_N // SUB,)
        ispec = pl.BlockSpec((2, RB, SUB, LANE), lambda n: (0, 0, n, 0))
        ospec = pl.BlockSpec((BATCH, RW_H, SUB, RB), lambda n: (0, 0, n, 0))
        oshape = (BATCH, RW_H, RW_N, RB)
    out = pl.pallas_call(
        functools.partial(_rl_out_kernel, lat=lat),
        grid=grid,
        in_specs=[ispec],
        out_specs=ospec,
        out_shape=jax.ShapeDtypeStruct(oshape, F32),
        compiler_params=pltpu.CompilerParams(dimension_semantics=("arbitrary",) * len(grid),
                                             vmem_limit_bytes=VMEM_LIMIT),
        name="rl_out_lat" if lat else "rl_out_ctx",
    )(o)
    return out.reshape(N_CTX_BLK, D_MIX, RB)


def _rw_scan_kernel(*refs, vb, npart, has_init, want_final):
    it = iter(refs)
    r_ref, w_ref, k_ref, v_ref, kk_ref, nkka_ref = (next(it), next(it), next(it), next(it),
                                                    next(it), next(it))
    s0_ref = next(it) if has_init else None
    o_ref = next(it)
    sfin_ref = next(it) if want_final else None
    s_scr = next(it)
    g = pl.program_id(0)
    tb = pl.program_id(1)

    @pl.when(tb == 0)
    def _():
        if has_init:
            s_scr[...] = s0_ref[0]
        else:
            s_scr[...] = jnp.zeros(s_scr.shape, F32)

    def bcast(ref, t, kx):
        return jnp.broadcast_to(ref[kx, pl.ds(t, 1), :], (SUB, LANE))

    def t_of(i):
        return jnp.where(g == 0, i, SCAN_TB - 1 - i)

    def zeros():
        return [[jnp.zeros((SUB, LANE), F32) for _ in range(vb)] for _ in range(npart)]

    def total(parts):
        return [functools.reduce(lambda a, b: a + b, [p[j] for p in parts]) for j in range(vb)]

    t0 = t_of(0)
    acc = zeros()
    for kx in range(RW_N):
        kkb = bcast(kk_ref, t0, kx)
        for j in range(vb):
            acc[kx % npart][j] = acc[kx % npart][j] + s_scr[kx, j * SUB:(j + 1) * SUB, :] * kkb

    def step(i, skk):
        t = t_of(i)
        tn = t_of(jnp.minimum(i + 1, SCAN_TB - 1))
        vv = [v_ref[t, j * SUB:(j + 1) * SUB, :] for j in range(vb)]
        oacc, nacc = zeros(), zeros()
        for kx in range(RW_N):
            wb = jnp.broadcast_to(w_ref[0, kx, pl.ds(t, 1), :], (SUB, LANE))
            ab = bcast(nkka_ref, t, kx)
            kb = bcast(k_ref, t, kx)
            rb = bcast(r_ref, t, kx)
            kkn = bcast(kk_ref, tn, kx)
            p = kx % npart
            for j in range(vb):
                s = s_scr[kx, j * SUB:(j + 1) * SUB, :] * wb + skk[j] * ab + vv[j] * kb
                s_scr[kx, j * SUB:(j + 1) * SUB, :] = s
                oacc[p][j] = oacc[p][j] + s * rb
                nacc[p][j] = nacc[p][j] + s * kkn
        for j, o in enumerate(total(oacc)):
            o_ref[0, t, j * SUB:(j + 1) * SUB, :] = o
        return tuple(total(nacc))

    lax.fori_loop(0, SCAN_TB, step, tuple(total(acc)))

    if want_final:
        @pl.when(tb == pl.num_programs(1) - 1)
        def _():
            sfin_ref[0] = s_scr[...]


def _rw_scan_call(r, w, k, v, kk, nkka, s0, *, want_final):
    L = r.shape[1]
    nv = v.shape[1]
    ntb = L // SCAN_TB
    has_init = s0 is not None
    tmap = lambda g, t: t + g * (ntb - 1 - 2 * t)
    kspec = pl.BlockSpec((RW_N, SCAN_TB, LANE), lambda g, t: (0, tmap(g, t), 0))
    wspec = pl.BlockSpec((1, RW_N, SCAN_TB, LANE), lambda g, t: (g, 0, tmap(g, t), 0))
    vspec = pl.BlockSpec((SCAN_TB, nv, LANE), lambda g, t: (tmap(g, t), 0, 0))
    ospec = pl.BlockSpec((1, SCAN_TB, nv, LANE), lambda g, t: (g, tmap(g, t), 0, 0))
    sspec = pl.BlockSpec((1, RW_N, nv, LANE), lambda g, t: (g, 0, 0, 0))
    in_specs = [kspec, wspec, kspec, vspec, kspec, kspec]
    args = [r, w, k, v, kk, nkka]
    if has_init:
        in_specs.append(sspec)
        args.append(s0)
    out_specs = [ospec]
    out_shape = [jax.ShapeDtypeStruct((2, L, nv, LANE), F32)]
    if want_final:
        out_specs.append(sspec)
        out_shape.append(jax.ShapeDtypeStruct((2, RW_N, nv, LANE), F32))
    return pl.pallas_call(
        functools.partial(_rw_scan_kernel, vb=nv // SUB, npart=max(1, 4 * SUB // nv),
                          has_init=has_init, want_final=want_final),
        grid=(2, ntb),
        in_specs=in_specs,
        out_specs=out_specs,
        out_shape=out_shape,
        scratch_shapes=[pltpu.VMEM((RW_N, nv, LANE), F32)],
        compiler_params=pltpu.CompilerParams(dimension_semantics=("arbitrary", "arbitrary"),
                                             vmem_limit_bytes=VMEM_LIMIT),
        name="rw_scan",
    )(*args)


def _merge_kernel(x_ref, mod_ref, ng_ref, wgate_ref, yssd_c, yssd_l, ygla_c, ygla_l, orw_c, orw_l,
                  bonus_ref, g_ref, lnw_ref, lnb_ref, bd_ref, wso_ref, wgo_ref, wro_ref, wout_ref,
                  o_ref):
    is_ctx = pl.program_id(0) < N_CTX_BLK
    x = x_ref[...]
    u = _norm_mod(x, ng_ref[...], mod_ref[0, 3:4, :], mod_ref[0, 4:5, :]).astype(BF16)
    bd = bd_ref[...]
    yssd = jnp.where(is_ctx, yssd_c[...], yssd_l[...])
    ygla = jnp.where(is_ctx, ygla_c[...], ygla_l[...])
    o = jnp.where(is_ctx, orw_c[0], orw_l[0]).T
    mu = _dot_sel(o, bd) * (1.0 / RW_N)
    oc = o - mu
    var = _dot_sel(oc * oc, bd) * (1.0 / RW_N)
    o = oc * lax.rsqrt(var + RW_GN_EPS) * lnw_ref[...] + lnb_ref[...]
    y_rw = ((o + bonus_ref[...]) * g_ref[...]).astype(BF16)
    merged = jnp.zeros((RB, D_MODEL), F32)
    for b, (y, wo_ref) in enumerate(((yssd, wso_ref), (ygla, wgo_ref), (y_rw, wro_ref))):
        gate = _sigmoid(jnp.dot(u, wgate_ref[:, b * D_MODEL:(b + 1) * D_MODEL],
                                preferred_element_type=F32))
        merged = merged + gate * jnp.dot(y, wo_ref[...], preferred_element_type=F32)
    m = jnp.dot(merged.astype(BF16), wout_ref[...], preferred_element_type=F32)
    o_ref[...] = x + mod_ref[0, 5:6, :] * m


def _merge_call(x, mod, ng, wgate, yssd_c, yssd_l, ygla_c, ygla_l, orw_c, orw_l, bonus, g, lnw, lnb,
                bd, wso, wgo, wro, wout):
    row_spec = pl.BlockSpec((RB, D_MODEL), lambda i: (i, 0))
    mix_spec = pl.BlockSpec((RB, D_MIX), lambda i: (i, 0))
    mix_c = pl.BlockSpec((RB, D_MIX), lambda i: (_ctx_idx(i), 0))
    mix_l = pl.BlockSpec((RB, D_MIX), lambda i: (_lat_idx(i), 0))
    t_c = pl.BlockSpec((1, D_MIX, RB), lambda i: (_ctx_idx(i), 0, 0))
    t_l = pl.BlockSpec((1, D_MIX, RB), lambda i: (_lat_idx(i), 0, 0))
    in_specs = [row_spec, pl.BlockSpec((1, N_MOD, D_MODEL), lambda i: (i, 0, 0)),
                _const_spec((1, D_MODEL)), _const_spec((D_MODEL, 3 * D_MODEL)),
                mix_c, mix_l, mix_c, mix_l, t_c, t_l, mix_spec, mix_spec,
                _const_spec((1, D_MIX)), _const_spec((1, D_MIX)), _const_spec((D_MIX, D_MIX)),
                _const_spec((D_MIX, D_MODEL)), _const_spec((D_MIX, D_MODEL)),
                _const_spec((D_MIX, D_MODEL)), _const_spec((D_MODEL, D_MODEL))]
    return pl.pallas_call(
        _merge_kernel,
        grid=(N_BLK,),
        in_specs=in_specs,
        out_specs=row_spec,
        out_shape=jax.ShapeDtypeStruct((N_TOK, D_MODEL), F32),
        compiler_params=pltpu.CompilerParams(dimension_semantics=("arbitrary",),
                                             vmem_limit_bytes=VMEM_LIMIT),
        name="merge",
    )(x, mod, ng, wgate, yssd_c, yssd_l, ygla_c, ygla_l, orw_c, orw_l, bonus, g, lnw, lnb, bd,
      wso, wgo, wro, wout)


def _grid_pos_embed(rows, cols, dim):
    quarter = dim // 4
    omega = 1.0 / (10000.0 ** (jnp.arange(quarter, dtype=F32) / quarter))
    er = jnp.arange(rows, dtype=F32)[:, None] * omega
    ec = jnp.arange(cols, dtype=F32)[:, None] * omega
    er = jnp.concatenate([jnp.sin(er), jnp.cos(er)], axis=-1)
    ec = jnp.concatenate([jnp.sin(ec), jnp.cos(ec)], axis=-1)
    emb = jnp.concatenate([jnp.broadcast_to(er[:, None], (rows, cols, dim // 2)),
                           jnp.broadcast_to(ec[None], (rows, cols, dim // 2))], axis=-1)
    return emb.reshape(rows * cols, dim)


def _pad_cols(a, n):
    return jnp.pad(a, [(0, 0)] * (a.ndim - 1) + [(0, n - a.shape[-1])])


def _rows_at(a, off, n):
    return jnp.pad(a, ((off, n - off - a.shape[0]), (0, 0)))


def _block_diag_ones(n, blk):
    i = np.arange(n)
    return jnp.asarray((i[:, None] // blk) == (i[None, :] // blk), BF16)


def _expand01(row0):
    m = np.zeros((LANE, D_MIX), np.float32)
    for h in range(SSD_H):
        m[row0 + h, h * SSD_P:(h + 1) * SSD_P] = 1.0
    return jnp.asarray(m, BF16)


def kernel(x_prompt, x_sample, state_ssd, state_gla, state_rwkv, c, c_ctx, norm_g, w_ada, b_ada,
           ffn_gate, ffn_up, ffn_down, w_in, ssd_conv_w, ssd_conv_b, ssd_dt_bias, ssd_A_log, ssd_D,
           ssd_norm, w_ssd_o, gla_gk_w, gla_gk_b, gla_norm, w_gla_o, rw_mu, rw_w0, rw_w2, rw_a0,
           rw_a2, rw_g2, rw_kk, rw_ka, rw_rk, rw_ln_w, rw_ln_b, w_rw_o, w_out, final_norm):
    pos = _grid_pos_embed(DEC_SEQ // GRID_W, GRID_W, D_MODEL)
    x = (x_prompt.reshape(N_CTX_TOK, D_MODEL), x_sample.reshape(-1, D_MODEL))
    s0_rw = state_rwkv.reshape(DEC_BATCH, DEPTH, 2, RW_H, RW_VQ, RW_NV_LAT, RW_N).transpose(
        1, 2, 6, 5, 4, 0, 3).reshape(DEPTH, 2, RW_N, RW_NV_LAT, LANE)

    cond8 = jnp.concatenate([c_ctx[None], c, jnp.zeros((SUB - 1 - DEC_BATCH, D_MODEL), F32)])
    ada = _ada_call(cond8, w_ada, b_ada)
    cond_of_blk = np.concatenate([np.zeros(N_CTX_BLK, np.int32),
                                  1 + np.arange(N_BLK - N_CTX_BLK, dtype=np.int32) // LAT_BLK_PER_SEQ])

    bd = _block_diag_ones(D_MIX, RW_N)
    ef, eb = _expand01(0), _expand01(SSD_H)
    o_ssd = D_MIX + SSD_XBC + 2 * SSD_H
    o_gla = o_ssd + 2 * GLA_H * GLA_DK + 2 * D_MIX + 2 * GLA_LR
    o_rw = o_gla + 3 * D_MIX + 2 * RW_LW + RW_LA + RW_LG

    new_ssd, new_gla, new_rw = [], [], []
    for l in range(DEPTH):
        mod = ada[l][cond_of_blk].reshape(N_BLK, N_MOD, D_MODEL)
        ng = norm_g[l]
        wi = w_in[l]
        w_ssd = _pad_cols(wi[:, :o_ssd], SSD_W).astype(BF16)
        w_gla = _pad_cols(wi[:, o_ssd:o_gla], GLA_W).astype(BF16)
        wr_ = wi[:, o_gla:o_rw]
        w_rw = jnp.concatenate([wr_[:, :1664], wr_[:, 1728:1856], _pad_cols(wr_[:, 1664:1728], LANE)],
                               axis=1).astype(BF16)
        mu_ = rw_mu[l]
        mu = jnp.concatenate([mu_[:1664], mu_[1728:1856], _pad_cols(mu_[1664:1728], LANE)])[None]
        w_gate = wi[:, o_rw:].astype(BF16)

        x = _ffn_call(x, mod, ng[0:1], ffn_gate[l, 0].astype(BF16), ffn_up[l, 0].astype(BF16),
                      ffn_down[l, 0].astype(BF16), mod_row=0, pos=pos if l == 0 else None)

        ssd_args = (x, mod, ng[1:2], w_ssd, ssd_conv_w[l], ssd_conv_b[l][None],
                    _pad_cols(ssd_dt_bias[l].reshape(1, -1), LANE),
                    _pad_cols(ssd_A_log[l].reshape(1, -1), LANE),
                    jnp.repeat(ssd_D[l], SSD_P, axis=1), ssd_norm[l][None], ef, eb)
        y_ssd_c, s_ssd = _ssd_call(*ssd_args, None, ctx=True)
        (y_ssd_l,) = _ssd_call(*ssd_args, state_ssd[:, l], ctx=False)
        new_ssd.append(s_ssd)

        gkw = jnp.stack([_rows_at(gla_gk_w[l, d], d * GLA_LR, LANE) for d in range(2)])
        gla_args = (x, mod, ng[1:2], w_gla, gkw.astype(BF16), gla_gk_b[l],
                    gla_norm[l][None])
        y_gla_c, s_gla = _gla_call(*gla_args, None, ctx=True)
        (y_gla_l,) = _gla_call(*gla_args, state_gla[:, l], ctx=False)
        new_gla.append(s_gla)

        w2p = jnp.stack([_rows_at(rw_w2[l, d], d * RW_LW, LANE) for d in range(2)]).astype(BF16)
        r, wf, wb, k, v, kk, nkka, g, bonus = _rw_prep_call(
            x, mod, ng[1:2], w_rw, mu, rw_a0[l][None], _rows_at(rw_a2[l], 0, LANE).astype(BF16),
            rw_g2[l].astype(BF16), rw_kk[l][None], rw_ka[l][None], rw_rk[l].reshape(1, D_MIX),
            rw_w0[l], w2p, bd)
        rc, kc, kkc, ac, wc = _rl_k_call(r, k, kk, nkka, wf, wb, lat=False)
        o_c, s_rw = _rw_scan_call(rc, wc, kc, _rl_v_call(v, lat=False), kkc, ac, None,
                                  want_final=True)
        new_rw.append(s_rw.reshape(2, RW_N, RW_N, BATCH, RW_H).transpose(3, 0, 4, 2, 1))
        rl, kl, kkl, al, wl = _rl_k_call(r, k, kk, nkka, wf, wb, lat=True)
        (o_l,) = _rw_scan_call(rl, wl, kl, _rl_v_call(v, lat=True), kkl, al, s0_rw[l],
                               want_final=False)

        x = _merge_call(x, mod, ng[1:2], w_gate, y_ssd_c, y_ssd_l, y_gla_c, y_gla_l,
                        _rl_out_call(o_c, lat=False), _rl_out_call(o_l, lat=True), bonus, g,
                        rw_ln_w[l][None], rw_ln_b[l][None], bd, w_ssd_o[l].astype(BF16),
                        w_gla_o[l].astype(BF16), w_rw_o[l].astype(BF16), w_out[l].astype(BF16))

        x = _ffn_call(x, mod, ng[2:3], ffn_gate[l, 1].astype(BF16), ffn_up[l, 1].astype(BF16),
                      ffn_down[l, 1].astype(BF16), mod_row=6,
                      final_g=final_norm[None] if l == DEPTH - 1 else None)

    y_prompt = x[0].reshape(BATCH, SEQ, D_MODEL)
    y_sample = x[1].reshape(DEC_BATCH, DEC_SEQ, D_MODEL)
    return (y_prompt, y_sample, jnp.stack(new_ssd, axis=1), jnp.stack(new_gla, axis=1),
            jnp.stack(new_rw, axis=1))
```

```python
import functools
import math

import numpy as np
import jax
import jax.numpy as jnp
from jax import lax
from jax.experimental import pallas as pl
from jax.experimental.pallas import tpu as pltpu

F32 = jnp.float32
BF16 = jnp.bfloat16

D_MODEL = 1024
BATCH = 16
SEQ = 256
DEPTH = 2
DEC_BATCH = 4
DEC_SEQ = 1024
GRID_W = 64
D_MIX = 512
D_FF = 2816
N_MOD = 9
SSD_P = 64
SSD_H = 8
SSD_N = 64
SSD_G = 2
SSD_XBC = 768
GLA_H = 4
GLA_DK = 64
GLA_DV = 128
GLA_LR = 16
GLA_GATE_NORM = 16.0
GLA_CHUNK = 64
RW_N = 64
RW_H = 8
RW_LW = 64
RW_LA = 64
RW_LG = 128
RMS_EPS = 1e-6
RW_GN_EPS = 64e-5

LANE = 128
SUB = 8
RB = 256
N_CTX_TOK = BATCH * SEQ
N_TOK = N_CTX_TOK + DEC_BATCH * DEC_SEQ
N_BLK = N_TOK // RB
N_CTX_BLK = N_CTX_TOK // RB
LAT_BLK_PER_SEQ = DEC_SEQ // RB
SSD_W = D_MIX + SSD_XBC + LANE
GLA_QK = GLA_H * GLA_DK
GLA_W = 2 * GLA_QK + 2 * D_MIX + LANE
RW_W = 3 * D_MIX + 3 * LANE
SCAN_TB = 32
FFN_SUB = 1
VMEM_LIMIT = 56 * 1024 * 1024


def _dot(a, b):
    return jnp.dot(a.astype(BF16), b.astype(BF16), preferred_element_type=F32)


def _dot_nt(a, b):
    return lax.dot_general(a.astype(BF16), b.astype(BF16), (((1,), (1,)), ((), ())),
                           preferred_element_type=F32)


def _split3(x):
    hi = x.astype(BF16)
    r1 = x - hi.astype(F32)
    mid = r1.astype(BF16)
    lo = (r1 - mid.astype(F32)).astype(BF16)
    return hi, mid, lo


def _sel_dot(m01, x):
    hi, mid, lo = _split3(x)
    f = lambda p: jnp.dot(m01, p, preferred_element_type=F32)
    return f(hi) + f(mid) + f(lo)


def _dot_sel(x, m01):
    hi, mid, lo = _split3(x)
    f = lambda p: jnp.dot(p, m01, preferred_element_type=F32)
    return f(hi) + f(mid) + f(lo)


def _sigmoid(x):
    return 0.5 * jnp.tanh(0.5 * x) + 0.5


def _silu(x):
    return x * _sigmoid(x)


def _softplus(x):
    return jnp.maximum(x, 0.0) + jnp.log(1.0 + jnp.exp(-jnp.abs(x)))


def _rmsnorm(x, g):
    return x * lax.rsqrt(jnp.mean(x * x, axis=-1, keepdims=True) + RMS_EPS) * g


def _norm_mod(x, g, shift, scale):
    return _rmsnorm(x, g) * (1.0 + scale) + shift


def _iota(shape, dim):
    return lax.broadcasted_iota(jnp.int32, shape, dim)


def _tri01(n, lower, chunk=None):
    t = _iota((n, n), 0)
    s = _iota((n, n), 1)
    m = (s <= t) if lower else (s >= t)
    if chunk is not None:
        m = m & ((t // chunk) == (s // chunk))
    return m


def _ada_kernel(c_ref, w_ref, b_ref, o_ref):
    o_ref[0] = _dot(_silu(c_ref[...]), w_ref[0]) + b_ref[0]


def _ada_call(cond8, w_ada, b_ada):
    tn = 1024
    nj = (N_MOD * D_MODEL) // tn
    return pl.pallas_call(
        _ada_kernel,
        grid=(DEPTH, nj),
        in_specs=[pl.BlockSpec((SUB, D_MODEL), lambda l, j: (0, 0)),
                  pl.BlockSpec((1, D_MODEL, tn), lambda l, j: (l, 0, j)),
                  pl.BlockSpec((1, 1, tn), lambda l, j: (l, 0, j))],
        out_specs=pl.BlockSpec((1, SUB, tn), lambda l, j: (l, 0, j)),
        out_shape=jax.ShapeDtypeStruct((DEPTH, SUB, N_MOD * D_MODEL), F32),
        compiler_params=pltpu.CompilerParams(dimension_semantics=("arbitrary", "arbitrary"),
                                             vmem_limit_bytes=VMEM_LIMIT),
        name="ada",
    )(cond8, w_ada, b_ada.reshape(DEPTH, 1, N_MOD * D_MODEL))


def _ffn_kernel(*refs, mod_row, first, last):
    it = iter(refs)
    is_ctx = pl.program_id(0) < N_CTX_BLK // FFN_SUB
    if first:
        xc_ref, xl_ref, pos_ref = next(it), next(it), next(it)
        x = jnp.where(is_ctx, xc_ref[...], xl_ref[...] + pos_ref[...])
    else:
        x = next(it)[...]
    mod_ref, ng_ref, wg_ref, wu_ref, wd_ref = next(it), next(it), next(it), next(it), next(it)
    fin_ref = next(it) if last else None
    shift = mod_ref[0, mod_row:mod_row + 1, :]
    scale = mod_ref[0, mod_row + 1:mod_row + 2, :]
    gate = mod_ref[0, mod_row + 2:mod_row + 3, :]
    h = _norm_mod(x, ng_ref[...], shift, scale).astype(BF16)
    a = _silu(jnp.dot(h, wg_ref[...], preferred_element_type=F32))
    a = (a * jnp.dot(h, wu_ref[...], preferred_element_type=F32)).astype(BF16)
    y = x + 0.5 * gate * jnp.dot(a, wd_ref[...], preferred_element_type=F32)
    if last:
        y = _rmsnorm(y, fin_ref[...])
        oc_ref, ol_ref = next(it), next(it)

        @pl.when(is_ctx)
        def _():
            oc_ref[...] = y

        @pl.when(jnp.logical_not(is_ctx))
        def _():
            ol_ref[...] = y
    else:
        next(it)[...] = y


def _const_spec(shape):
    nd = len(shape)
    return pl.BlockSpec(shape, lambda i: (0,) * nd, pipeline_mode=pl.Buffered(1))


def _ctx_idx(i):
    return jnp.minimum(i, N_CTX_BLK - 1)


def _lat_idx(i):
    return jnp.maximum(i - N_CTX_BLK, 0)


def _ffn_call(x, mod, ng, wg, wu, wd, *, mod_row, pos=None, final_g=None):
    first = pos is not None
    last = final_g is not None
    fb = FFN_SUB * RB
    nctx = N_CTX_BLK // FFN_SUB
    row_spec = pl.BlockSpec((fb, D_MODEL), lambda i: (i, 0))
    ctx_spec = pl.BlockSpec((fb, D_MODEL), lambda i: (jnp.minimum(i, nctx - 1), 0))
    lat_spec = pl.BlockSpec((fb, D_MODEL), lambda i: (jnp.maximum(i - nctx, 0), 0))
    if first:
        in_specs = [ctx_spec, lat_spec,
                    pl.BlockSpec((fb, D_MODEL),
                                 lambda i: (jnp.maximum(i - nctx, 0) % (DEC_SEQ // fb), 0))]
        args = [x[0], x[1], pos]
    else:
        in_specs = [row_spec]
        args = [x]
    in_specs += [pl.BlockSpec((1, N_MOD, D_MODEL), lambda i: (FFN_SUB * i, 0, 0)),
                 _const_spec((1, D_MODEL)), _const_spec((D_MODEL, D_FF)),
                 _const_spec((D_MODEL, D_FF)), _const_spec((D_FF, D_MODEL))]
    args += [mod, ng, wg, wu, wd]
    if last:
        in_specs.append(_const_spec((1, D_MODEL)))
        args.append(final_g)
        half = jax.ShapeDtypeStruct((N_TOK // 2, D_MODEL), F32)
        out_specs, out_shape = [ctx_spec, lat_spec], [half, half]
    else:
        out_specs, out_shape = row_spec, jax.ShapeDtypeStruct((N_TOK, D_MODEL), F32)
    return pl.pallas_call(
        functools.partial(_ffn_kernel, mod_row=mod_row, first=first, last=last),
        grid=(N_BLK // FFN_SUB,),
        in_specs=in_specs,
        out_specs=out_specs,
        out_shape=out_shape,
        compiler_params=pltpu.CompilerParams(dimension_semantics=("arbitrary",),
                                             vmem_limit_bytes=VMEM_LIMIT),
        name="ffn",
    )(*args)


def _ssd_kernel(*refs, L, has_init, want_final):
    it = iter(refs)
    x_ref, mod_ref, ng_ref, w_ref = next(it), next(it), next(it), next(it)
    cw_ref, cb_ref, dtb_ref, alog_ref, d_ref, nw_ref = (next(it), next(it), next(it), next(it),
                                                        next(it), next(it))
    ef_ref, eb_ref = next(it), next(it)
    s0_ref = next(it) if has_init else None
    y_ref = next(it)
    sfin_ref = next(it) if want_final else None
    yacc = next(it)

    C = RB
    nc = L // C
    x = x_ref[...]
    u = _norm_mod(x, ng_ref[...], mod_ref[0, 3:4, :], mod_ref[0, 4:5, :]).astype(BF16)
    p = jnp.dot(u, w_ref[...], preferred_element_type=F32)
    z = p[:, :D_MIX]
    xbc = p[:, D_MIX:D_MIX + SSD_XBC]
    dtp = p[:, D_MIX + SSD_XBC:]
    rows = _iota((L, 1), 0)
    prev = jnp.where(rows == 0, 0.0, pltpu.roll(xbc, 1, 0))
    nxt = jnp.where(rows == L - 1, 0.0, pltpu.roll(xbc, L - 1, 0))
    xc = cb_ref[...] + prev * cw_ref[0:1, :] + xbc * cw_ref[1:2, :] + nxt * cw_ref[2:3, :]
    xc = _silu(xc)
    xs = xc[:, :D_MIX]
    bm = xc[:, D_MIX:D_MIX + SSD_G * SSD_N]
    cm = xc[:, D_MIX + SSD_G * SSD_N:]
    dt = _softplus(dtp + dtb_ref[...])
    adt = dt * (-jnp.exp(alog_ref[...]))

    tril = _tri01(C, True)
    triu = _tri01(C, False)
    tril_b = jnp.where(tril, 1.0, 0.0).astype(BF16)
    triu_b = jnp.where(triu, 1.0, 0.0).astype(BF16)
    ef = ef_ref[...]
    eb = eb_ref[...]

    cs, csT, rcs, rcsT, dtT, loc_f, loc_b = [], [], [], [], [], [], []
    for c in range(nc):
        a_c = adt[c * C:(c + 1) * C]
        a_cT = a_c.T
        cs.append(_sel_dot(tril_b, a_c))
        rcs.append(_sel_dot(triu_b, a_c))
        csT.append(_dot_sel(a_cT, triu_b))
        rcsT.append(_dot_sel(a_cT, tril_b))
        dtT.append(dt[c * C:(c + 1) * C].T)

    need_states = want_final or nc > 1
    if need_states:
        for c in range(nc):
            xs_c = xs[c * C:(c + 1) * C]
            dt_c = dt[c * C:(c + 1) * C]
            wf = jnp.exp(cs[c][C - 1:C, :] - cs[c]) * dt_c
            wb = jnp.exp(rcs[c][0:1, :] - rcs[c]) * dt_c
            xwf = (xs_c * _dot_sel(wf, ef)).T
            xwb = (xs_c * _dot_sel(wb, eb)).T
            lf, lb = [], []
            for h in range(SSD_H):
                g = h // (SSD_H // SSD_G)
                bm_g = bm[c * C:(c + 1) * C, g * SSD_N:(g + 1) * SSD_N]
                lf.append(_dot(xwf[h * SSD_P:(h + 1) * SSD_P, :], bm_g))
                lb.append(_dot(xwb[h * SSD_P:(h + 1) * SSD_P, :], bm_g))
            loc_f.append(lf)
            loc_b.append(lb)

    zero_s = jnp.zeros((SSD_P, SSD_N), F32)
    sin_f = [[None] * SSD_H for _ in range(nc + 1)]
    sin_b = [[None] * SSD_H for _ in range(nc + 1)]
    for h in range(SSD_H):
        sin_f[0][h] = s0_ref[0, 0, h] if has_init else zero_s
        sin_b[nc][h] = s0_ref[0, 1, h] if has_init else zero_s
    if need_states:
        for c in range(nc):
            dec = jnp.exp(cs[c][C - 1:C, :])
            for h in range(SSD_H):
                sin_f[c + 1][h] = dec[:, h:h + 1] * sin_f[c][h] + loc_f[c][h]
        for c in range(nc - 1, -1, -1):
            dec = jnp.exp(rcs[c][0:1, :])
            for h in range(SSD_H):
                sin_b[c][h] = dec[:, SSD_H + h:SSD_H + h + 1] * sin_b[c + 1][h] + loc_b[c][h]

    for c in range(nc):
        sl = slice(c * C, (c + 1) * C)
        ecs = jnp.exp(cs[c])
        ercs = jnp.exp(rcs[c])
        for h in range(SSD_H):
            g = h // (SSD_H // SSD_G)
            cm_g = cm[sl, g * SSD_N:(g + 1) * SSD_N]
            bm_g = bm[sl, g * SSD_N:(g + 1) * SSD_N]
            cb = _dot_nt(cm_g, bm_g)
            lf = jnp.exp(jnp.where(tril, cs[c][:, h:h + 1] - csT[c][h:h + 1, :], -jnp.inf))
            lb = jnp.exp(jnp.where(triu, rcs[c][:, SSD_H + h:SSD_H + h + 1]
                                   - rcsT[c][SSD_H + h:SSD_H + h + 1, :], -jnp.inf))
            m = cb * (lf * dtT[c][h:h + 1, :] + lb * dtT[c][SSD_H + h:SSD_H + h + 1, :])
            y_h = _dot(m, xs[sl, h * SSD_P:(h + 1) * SSD_P])
            if has_init or nc > 1:
                y_h = y_h + ecs[:, h:h + 1] * _dot_nt(cm_g, sin_f[c][h])
                y_h = y_h + ercs[:, SSD_H + h:SSD_H + h + 1] * _dot_nt(cm_g, sin_b[c + 1][h])
            yacc[sl, h * SSD_P:(h + 1) * SSD_P] = y_h

    y = yacc[...] + xs * (d_ref[0:1, :] + d_ref[1:2, :])
    y = _rmsnorm(y * _silu(z), nw_ref[...])
    y_ref[...] = y.astype(BF16)
    if want_final:
        for h in range(SSD_H):
            sfin_ref[0, 0, h] = sin_f[nc][h]
            sfin_ref[0, 1, h] = sin_b[0][h]


def _ssd_call(x, mod, ng, w, cw, cb, dtb, alog, dexp, nw, ef, eb, s0, *, ctx):
    L = SEQ if ctx else DEC_SEQ
    nseq = BATCH if ctx else DEC_BATCH
    blk0 = 0 if ctx else N_CTX_TOK // L
    mod_of = (lambda i: (i, 0, 0)) if ctx else (lambda i: (N_CTX_BLK + LAT_BLK_PER_SEQ * i, 0, 0))
    in_specs = [pl.BlockSpec((L, D_MODEL), lambda i: (blk0 + i, 0)),
                pl.BlockSpec((1, N_MOD, D_MODEL), mod_of),
                _const_spec((1, D_MODEL)), _const_spec((D_MODEL, SSD_W)),
                _const_spec((3, SSD_XBC)), _const_spec((1, SSD_XBC)),
                _const_spec((1, LANE)), _const_spec((1, LANE)),
                _const_spec((2, D_MIX)), _const_spec((1, D_MIX)),
                _const_spec((LANE, D_MIX)), _const_spec((LANE, D_MIX))]
    args = [x, mod, ng, w, cw, cb, dtb, alog, dexp, nw, ef, eb]
    st_spec = pl.BlockSpec((1, 2, SSD_H, SSD_P, SSD_N), lambda i: (i, 0, 0, 0, 0))
    out_specs = [pl.BlockSpec((L, D_MIX), lambda i: (i, 0))]
    out_shape = [jax.ShapeDtypeStruct((nseq * L, D_MIX), BF16)]
    if ctx:
        out_specs.append(st_spec)
        out_shape.append(jax.ShapeDtypeStruct((nseq, 2, SSD_H, SSD_P, SSD_N), F32))
    else:
        in_specs.append(st_spec)
        args.append(s0)
    return pl.pallas_call(
        functools.partial(_ssd_kernel, L=L, has_init=not ctx, want_final=ctx),
        grid=(nseq,),
        in_specs=in_specs,
        out_specs=out_specs,
        out_shape=out_shape,
        scratch_shapes=[pltpu.VMEM((L, D_MIX), F32)],
        compiler_params=pltpu.CompilerParams(dimension_semantics=("arbitrary",),
                                             vmem_limit_bytes=VMEM_LIMIT),
        name="ssd_ctx" if ctx else "ssd_lat",
    )(*args)


def _gla_kernel(*refs, L, has_init, want_final):
    it = iter(refs)
    x_ref, mod_ref, ng_ref, w_ref = next(it), next(it), next(it), next(it)
    gkw_ref, gkb_ref, nw_ref = next(it), next(it), next(it)
    s0_ref = next(it) if has_init else None
    y_ref = next(it)
    sfin_ref = next(it) if want_final else None
    oacc = next(it)

    C = GLA_CHUNK
    B = RB
    cpb = B // C
    nb = L // B
    x = x_ref[...]
    u = _norm_mod(x, ng_ref[...], mod_ref[0, 3:4, :], mod_ref[0, 4:5, :]).astype(BF16)
    p = jnp.dot(u, w_ref[...], preferred_element_type=F32)
    q = p[:, :GLA_QK] * (GLA_DK ** -0.5)
    k = p[:, GLA_QK:2 * GLA_QK]
    v = p[:, 2 * GLA_QK:2 * GLA_QK + D_MIX]
    gg = p[:, 2 * GLA_QK + D_MIX:2 * GLA_QK + 2 * D_MIX]
    lr = p[:, 2 * GLA_QK + 2 * D_MIX:].astype(BF16)
    lg = []
    for d in range(2):
        pre = jnp.dot(lr, gkw_ref[d], preferred_element_type=F32) + gkb_ref[d:d + 1, :]
        lg.append(-_softplus(-pre) / GLA_GATE_NORM)

    low = _tri01(B, True, C)
    upp = _tri01(B, False, C)
    low_b = jnp.where(low, 1.0, 0.0).astype(BF16)
    upp_b = jnp.where(upp, 1.0, 0.0).astype(BF16)
    rowi = _iota((B, 1), 0)

    if has_init:
        s_f = [s0_ref[0, 0, h] for h in range(GLA_H)]
        s_b = [s0_ref[0, 1, h] for h in range(GLA_H)]
    else:
        s_f = [jnp.zeros((GLA_DK, GLA_DV), F32)] * GLA_H
        s_b = list(s_f)

    hpp = LANE // GLA_DK
    lane_head = _iota((1, LANE), 1) // GLA_DK

    def pair(h):
        return slice((h // hpp) * LANE, (h // hpp + 1) * LANE)

    def only(h, a):
        return jnp.where(lane_head == h % hpp, a[:, pair(h)], 0.0)

    def pair_state(s, h):
        j = h // hpp
        return jnp.concatenate(s[j * hpp:(j + 1) * hpp], axis=0)

    blocks = []
    for b in range(nb):
        sl = slice(b * B, (b + 1) * B)
        bf = _sel_dot(low_b, lg[0][sl])
        rb = _sel_dot(upp_b, lg[1][sl])
        q_b, k_b = q[sl], k[sl]
        qf, qb = q_b * jnp.exp(bf), q_b * jnp.exp(rb)
        blocks.append(dict(sl=sl, bf=bf, rb=rb, bfT=bf.T, rbT=rb.T, k=k_b,
                           kf=k_b * jnp.exp(-bf), kb=k_b * jnp.exp(-rb),
                           qf=[only(h, qf) for h in range(GLA_H)],
                           qb=[only(h, qb) for h in range(GLA_H)]))

    for blk in blocks:
        sl = blk["sl"]
        for h in range(GLA_H):
            a = jnp.where(low, _dot_nt(blk["qf"][h], blk["kf"][:, pair(h)]), 0.0)
            a = a + jnp.where(upp, _dot_nt(blk["qb"][h], blk["kb"][:, pair(h)]), 0.0)
            oacc[sl, h * GLA_DV:(h + 1) * GLA_DV] = _dot(a, v[sl, h * GLA_DV:(h + 1) * GLA_DV])

    def inter(blk, c, qd, cum, cumT, edge, s):
        sl = blk["sl"]
        rs = slice(sl.start + c * C, sl.start + (c + 1) * C)
        kdecT = (blk["k"] * jnp.exp(cum[edge:edge + 1, :] - cum)).T
        in_c = (rowi >= c * C) & (rowi < (c + 1) * C)
        s_in = [pair_state(s, h) for h in range(0, GLA_H, hpp)]
        for h in range(GLA_H):
            ks = slice(h * GLA_DK, (h + 1) * GLA_DK)
            vs = slice(h * GLA_DV, (h + 1) * GLA_DV)
            oacc[rs, vs] = oacc[rs, vs] + _dot(qd[h][c * C:(c + 1) * C], s_in[h // hpp])
            kv = _dot(kdecT[ks, :], jnp.where(in_c, v[sl, vs], 0.0))
            s[h] = jnp.exp(cumT[ks, edge:edge + 1]) * s[h] + kv

    for blk in blocks:
        for c in range(cpb):
            inter(blk, c, blk["qf"], blk["bf"], blk["bfT"], (c + 1) * C - 1, s_f)
    for blk in reversed(blocks):
        for c in range(cpb - 1, -1, -1):
            inter(blk, c, blk["qb"], blk["rb"], blk["rbT"], c * C, s_b)

    for h in range(GLA_H):
        vs = slice(h * GLA_DV, (h + 1) * GLA_DV)
        o_h = _rmsnorm(oacc[:, vs], nw_ref[...])
        y_ref[:, vs] = (o_h * _silu(gg[:, vs])).astype(BF16)
        if want_final:
            sfin_ref[0, 0, h] = s_f[h]
            sfin_ref[0, 1, h] = s_b[h]


def _gla_call(x, mod, ng, w, gkw, gkb, nw, s0, *, ctx):
    L = SEQ if ctx else DEC_SEQ
    nseq = BATCH if ctx else DEC_BATCH
    blk0 = 0 if ctx else N_CTX_TOK // L
    mod_of = (lambda i: (i, 0, 0)) if ctx else (lambda i: (N_CTX_BLK + LAT_BLK_PER_SEQ * i, 0, 0))
    in_specs = [pl.BlockSpec((L, D_MODEL), lambda i: (blk0 + i, 0)),
                pl.BlockSpec((1, N_MOD, D_MODEL), mod_of),
                _const_spec((1, D_MODEL)), _const_spec((D_MODEL, GLA_W)),
                _const_spec((2, LANE, GLA_QK)), _const_spec((2, GLA_QK)),
                _const_spec((1, GLA_DV))]
    args = [x, mod, ng, w, gkw, gkb, nw]
    st_spec = pl.BlockSpec((1, 2, GLA_H, GLA_DK, GLA_DV), lambda i: (i, 0, 0, 0, 0))
    out_specs = [pl.BlockSpec((L, D_MIX), lambda i: (i, 0))]
    out_shape = [jax.ShapeDtypeStruct((nseq * L, D_MIX), BF16)]
    if ctx:
        out_specs.append(st_spec)
        out_shape.append(jax.ShapeDtypeStruct((nseq, 2, GLA_H, GLA_DK, GLA_DV), F32))
    else:
        in_specs.append(st_spec)
        args.append(s0)
    return pl.pallas_call(
        functools.partial(_gla_kernel, L=L, has_init=not ctx, want_final=ctx),
        grid=(nseq,),
        in_specs=in_specs,
        out_specs=out_specs,
        out_shape=out_shape,
        scratch_shapes=[pltpu.VMEM((L, D_MIX), F32)],
        compiler_params=pltpu.CompilerParams(dimension_semantics=("arbitrary",),
                                             vmem_limit_bytes=VMEM_LIMIT),
        name="gla_ctx" if ctx else "gla_lat",
    )(*args)


def _rw_prep_kernel(x_ref, xp_ref, xn_ref, mod_ref, ng_ref, w_ref, mu_ref, a0_ref, a2_ref, g2_ref,
                    kkw_ref, ka_ref, rk_ref, w0_ref, w2_ref, bd_ref,
                    r_ref, wf_ref, wb_ref, k_ref, v_ref, kk_ref, nkka_ref, g_ref, bonus_ref):
    i = pl.program_id(0)
    j = (i - N_CTX_BLK) % LAT_BLK_PER_SEQ
    is_first = (i < N_CTX_BLK) | (j == 0)
    is_last = (i < N_CTX_BLK) | (j == LAT_BLK_PER_SEQ - 1)
    x_all = jnp.concatenate([x_ref[...], xp_ref[...], xn_ref[...]], axis=0)
    u_all = _norm_mod(x_all, ng_ref[...], mod_ref[0, 3:4, :], mod_ref[0, 4:5, :]).astype(BF16)
    p_all = jnp.dot(u_all, w_ref[...], preferred_element_type=F32)
    p = p_all[:RB]
    p_prev = jnp.where(is_first, 0.0, p_all[RB + SUB - 1:RB + SUB, :])
    p_next = jnp.where(is_last, 0.0, p_all[RB + SUB:RB + SUB + 1, :])
    rows = _iota((RB, 1), 0)
    prev = jnp.where(rows == 0, p_prev, pltpu.roll(p, 1, 0))
    nxt = jnp.where(rows == RB - 1, p_next, pltpu.roll(p, RB - 1, 0))
    p = p + (0.5 * (prev + nxt) - p) * mu_ref[...]

    r = p[:, :D_MIX]
    k = p[:, D_MIX:2 * D_MIX]
    v = p[:, 2 * D_MIX:3 * D_MIX]
    wlr = p[:, 3 * D_MIX:3 * D_MIX + LANE]
    glr = p[:, 3 * D_MIX + LANE:3 * D_MIX + 2 * LANE]
    alr = p[:, 3 * D_MIX + 2 * LANE:]
    bd = bd_ref[...]
    a = _sigmoid(a0_ref[...] + _dot(alr, a2_ref[...]))
    g = _dot(_sigmoid(glr), g2_ref[...])
    kk = k * kkw_ref[...]
    kk = kk / jnp.maximum(jnp.sqrt(_dot_sel(kk * kk, bd)), 1e-12)
    k = k * (1.0 + (a - 1.0) * ka_ref[...])
    tw = jnp.tanh(wlr).astype(BF16)
    for d, o_ref in ((0, wf_ref), (1, wb_ref)):
        pre = w0_ref[d:d + 1, :] + jnp.dot(tw, w2_ref[d], preferred_element_type=F32)
        o_ref[0] = jnp.exp(-math.exp(-0.5) * _sigmoid(pre)).T
    r_ref[0] = r.T
    k_ref[0] = k.T
    v_ref[0] = v.T
    kk_ref[0] = kk.T
    nkka_ref[0] = (-(kk * a)).T
    g_ref[...] = g
    bonus_ref[...] = _dot_sel(r * k * rk_ref[...], bd) * v


def _rw_prep_call(x, mod, ng, w, mu, a0, a2, g2, kkw, ka, rk, w0, w2, bd):
    hb = RB // SUB
    row_spec = pl.BlockSpec((RB, D_MODEL), lambda i: (i, 0))
    out_spec = pl.BlockSpec((RB, D_MIX), lambda i: (i, 0))
    in_specs = [row_spec,
                pl.BlockSpec((SUB, D_MODEL), lambda i: (jnp.maximum(i * hb - 1, 0), 0)),
                pl.BlockSpec((SUB, D_MODEL), lambda i: (jnp.minimum((i + 1) * hb, N_TOK // SUB - 1), 0)),
                pl.BlockSpec((1, N_MOD, D_MODEL), lambda i: (i, 0, 0)),
                _const_spec((1, D_MODEL)), _const_spec((D_MODEL, RW_W)), _const_spec((1, RW_W)),
                _const_spec((1, D_MIX)), _const_spec((LANE, D_MIX)), _const_spec((LANE, D_MIX)),
                _const_spec((1, D_MIX)), _const_spec((1, D_MIX)), _const_spec((1, D_MIX)),
                _const_spec((2, D_MIX)), _const_spec((2, LANE, D_MIX)), _const_spec((D_MIX, D_MIX))]
    t_spec = pl.BlockSpec((1, D_MIX, RB), lambda i: (i, 0, 0))
    t_shape = jax.ShapeDtypeStruct((N_BLK, D_MIX, RB), F32)
    tok_shape = jax.ShapeDtypeStruct((N_TOK, D_MIX), F32)
    return pl.pallas_call(
        _rw_prep_kernel,
        grid=(N_BLK,),
        in_specs=in_specs,
        out_specs=[t_spec] * 7 + [out_spec] * 2,
        out_shape=[t_shape] * 7 + [tok_shape] * 2,
        compiler_params=pltpu.CompilerParams(dimension_semantics=("arbitrary",),
                                             vmem_limit_bytes=VMEM_LIMIT),
        name="rw_prep",
    )(x, x, x, mod, ng, w, mu, a0, a2, g2, kkw, ka, rk, w0, w2, bd)


RW_VH = LANE // (2 * DEC_BATCH * RW_H)
RW_NV_LAT = RW_N // RW_VH
LAT_ROWS = DEC_BATCH * RW_H
LAT_VIEW = (N_BLK // LAT_BLK_PER_SEQ, LAT_BLK_PER_SEQ, RW_H, RW_N, RB)


def _rl_rows(ref, nl, lat):
    x = ref[:, 0, :, nl, :] if lat else ref[:, :, nl, :]
    return x.reshape(-1, RB)


def _time_flip(x):
    ex = _iota((RB, RB), 0) + _iota((RB, RB), 1) == RB - 1
    return _dot_sel(x, jnp.where(ex, 1.0, 0.0).astype(BF16))


def _lat_spec(n_of, back):
    def index(j, n):
        return (1, LAT_BLK_PER_SEQ - 1 - j if back else j, 0, n_of(n), 0)
    return pl.BlockSpec((DEC_BATCH, 1, RW_H, SUB, RB), index)


def _rl_k_ctx_kernel(r_ref, k_ref, kk_ref, a_ref, wf_ref, wb_ref, ro, ko, kko, ao, wo):
    for nl in range(SUB):
        ro[nl] = _rl_rows(r_ref, nl, False).T
        ko[nl] = _rl_rows(k_ref, nl, False).T
        kko[nl] = _rl_rows(kk_ref, nl, False).T
        ao[nl] = _rl_rows(a_ref, nl, False).T
        wo[0, nl] = _rl_rows(wf_ref, nl, False).T
        wo[1, nl] = _rl_rows(wb_ref, nl, False).T


def _rl_k_lat_kernel(rf, rb, kf, kb, kkf, kkb, af, ab, wff, wbb, ro, ko, kko, ao, wo):
    sets = ((rf, rb, ro), (kf, kb, ko), (kkf, kkb, kko), (af, ab, ao), (wff, wbb, wo))
    back = _time_flip(jnp.concatenate(
        [_rl_rows(b, nl, True) for _, b, _ in sets for nl in range(SUB)], axis=0))
    i = 0
    for f, _, o in sets:
        for nl in range(SUB):
            pair = [_rl_rows(f, nl, True), back[i * LAT_ROWS:(i + 1) * LAT_ROWS]]
            o[nl] = jnp.concatenate(pair * RW_VH, axis=0).T
            i += 1


def _rl_k_call(r, k, kk, nkka, wf, wb, *, lat):
    nb = RW_N // SUB
    if lat:
        L, G = DEC_SEQ, 1
        fwd = _lat_spec(lambda n: n, False)
        bwd = _lat_spec(lambda n: n, True)
        ospec = pl.BlockSpec((SUB, RB, LANE), lambda j, n: (n, j, 0))
        args = [a.reshape(LAT_VIEW) for a in (r, r, k, k, kk, kk, nkka, nkka, wf, wb)]
        call = dict(grid=(LAT_BLK_PER_SEQ, nb), in_specs=[fwd, bwd] * 5, out_specs=[ospec] * 5)
        body, name = _rl_k_lat_kernel, "rl_k_lat"
        wshape = jax.ShapeDtypeStruct((RW_N, L, LANE), F32)
    else:
        L, G = SEQ, 2
        ispec = pl.BlockSpec((BATCH, RW_H, SUB, RB), lambda n: (0, 0, n, 0))
        ospec = pl.BlockSpec((SUB, RB, LANE), lambda n: (n, 0, 0))
        wspec = pl.BlockSpec((2, SUB, RB, LANE), lambda n: (0, n, 0, 0))
        args = [a.reshape(N_BLK, RW_H, RW_N, RB) for a in (r, k, kk, nkka, wf, wb)]
        call = dict(grid=(nb,), in_specs=[ispec] * 6, out_specs=[ospec] * 4 + [wspec])
        body, name = _rl_k_ctx_kernel, "rl_k_ctx"
        wshape = jax.ShapeDtypeStruct((2, RW_N, L, LANE), F32)
    kshape = jax.ShapeDtypeStruct((RW_N, L, LANE), F32)
    outs = pl.pallas_call(
        body,
        out_shape=[kshape] * 4 + [wshape],
        compiler_params=pltpu.CompilerParams(
            dimension_semantics=("arbitrary",) * len(call["grid"]), vmem_limit_bytes=VMEM_LIMIT),
        name=name, **call,
    )(*args)
    return list(outs[:4]) + [outs[4].reshape(G, RW_N, L, LANE)]


def _rl_v_ctx_kernel(v_ref, o_ref):
    for nl in range(SUB):
        o_ref[:, nl, :] = _rl_rows(v_ref, nl, False).T


def _rl_v_lat_kernel(f0, f1, b0, b1, o_ref):
    back = _time_flip(jnp.concatenate(
        [_rl_rows(b, nl, True) for b in (b0, b1) for nl in range(SUB)], axis=0))
    for nl in range(SUB):
        parts = []
        for h, f in enumerate((f0, f1)):
            i = h * SUB + nl
            parts += [_rl_rows(f, nl, True), back[i * LAT_ROWS:(i + 1) * LAT_ROWS]]
        o_ref[:, nl, :] = jnp.concatenate(parts, axis=0).T


def _rl_v_call(v, *, lat):
    if lat:
        nvb = RW_NV_LAT // SUB
        half = lambda h: (lambda n: h * nvb + n)
        call = dict(grid=(LAT_BLK_PER_SEQ, nvb),
                    in_specs=[_lat_spec(half(0), False), _lat_spec(half(1), False),
                              _lat_spec(half(0), True), _lat_spec(half(1), True)],
                    out_specs=pl.BlockSpec((RB, SUB, LANE), lambda j, n: (j, n, 0)))
        args = [v.reshape(LAT_VIEW)] * 4
        oshape = jax.ShapeDtypeStruct((DEC_SEQ, RW_NV_LAT, LANE), F32)
        body, name = _rl_v_lat_kernel, "rl_v_lat"
    else:
        call = dict(grid=(RW_N // SUB,),
                    in_specs=[pl.BlockSpec((BATCH, RW_H, SUB, RB), lambda n: (0, 0, n, 0))],
                    out_specs=pl.BlockSpec((RB, SUB, LANE), lambda n: (0, n, 0)))
        args = [v.reshape(N_BLK, RW_H, RW_N, RB)]
        oshape = jax.ShapeDtypeStruct((SEQ, RW_N, LANE), F32)
        body, name = _rl_v_ctx_kernel, "rl_v_ctx"
    return pl.pallas_call(
        body,
        out_shape=oshape,
        compiler_params=pltpu.CompilerParams(
            dimension_semantics=("arbitrary",) * len(call["grid"]), vmem_limit_bytes=VMEM_LIMIT),
        name=name, **call,
    )(*args)


def _rl_out_ctx_kernel(o_ref, ot_ref):
    for nl in range(SUB):
        x = (o_ref[0, :, nl, :] + o_ref[1, :, nl, :]).T
        ot_ref[:, :, nl, :] = x.reshape(BATCH, RW_H, RB)


def _rl_out_lat_kernel(of_ref, ob_ref, ot_ref):
    fwd = [of_ref[0, :, nl, :].T for nl in range(SUB)]
    mir = [ob_ref[0, :, nl, :].T for nl in range(SUB)]
    lo = lambda h, d: (2 * h + d) * LAT_ROWS
    back = _time_flip(jnp.concatenate(
        [mir[nl][lo(h, 1):lo(h, 1) + LAT_ROWS] for nl in range(SUB) for h in range(RW_VH)], axis=0))
    for nl in range(SUB):
        for h in range(RW_VH):
            i = nl * RW_VH + h
            x = fwd[nl][lo(h, 0):lo(h, 0) + LAT_ROWS] + back[i * LAT_ROWS:(i + 1) * LAT_ROWS]
            ot_ref[:, 0, :, h, nl, :] = x.reshape(DEC_BATCH, RW_H, RB)


def _rl_out_call(o, *, lat):
    if lat:
        nvb = RW_NV_LAT // SUB
        call = dict(grid=(LAT_BLK_PER_SEQ, nvb),
                    in_specs=[pl.BlockSpec((1, RB, SUB, LANE), lambda j, n: (0, j, n, 0)),
                              pl.BlockSpec((1, RB, SUB, LANE),
                                           lambda j, n: (0, LAT_BLK_PER_SEQ - 1 - j, n, 0))],
                    out_specs=pl.BlockSpec((DEC_BATCH, 1, RW_H, RW_VH, SUB, RB),
                                           lambda j, n: (0, j, 0, 0, n, 0)))
        oshape = (DEC_BATCH, LAT_BLK_PER_SEQ, RW_H, RW_VH, RW_NV_LAT, RB)
        args, body, name = [o, o], _rl_out_lat_kernel, "rl_out_lat"
    else:
        call = dict(grid=(RW_N // SUB,),
                    in_specs=[pl.BlockSpec((2, RB, SUB, LANE), lambda n: (0, 0, n, 0))],
                    out_specs=pl.BlockSpec((BATCH, RW_H, SUB, RB), lambda n: (0, 0, n, 0)))
        oshape = (BATCH, RW_H, RW_N, RB)
        args, body, name = [o], _rl_out_ctx_kernel, "rl_out_ctx"
    out = pl.pallas_call(
        body,
        out_shape=jax.ShapeDtypeStruct(oshape, F32),
        compiler_params=pltpu.CompilerParams(
            dimension_semantics=("arbitrary",) * len(call["grid"]), vmem_limit_bytes=VMEM_LIMIT),
        name=name, **call,
    )(*args)
    return out.reshape(N_CTX_BLK, D_MIX, RB)


def _rw_scan_kernel(*refs, vb, npart, has_init, want_final):
    it = iter(refs)
    r_ref, w_ref, k_ref, v_ref, kk_ref, nkka_ref = (next(it), next(it), next(it), next(it),
                                                    next(it), next(it))
    s0_ref = next(it) if has_init else None
    o_ref = next(it)
    sfin_ref = next(it) if want_final else None
    s_scr = next(it)
    g = pl.program_id(0)
    tb = pl.program_id(1)

    @pl.when(tb == 0)
    def _():
        if has_init:
            s_scr[...] = s0_ref[0]
        else:
            s_scr[...] = jnp.zeros(s_scr.shape, F32)

    def bcast(ref, t, kx):
        return jnp.broadcast_to(ref[kx, pl.ds(t, 1), :], (SUB, LANE))

    def t_of(i):
        return jnp.where(g == 0, i, SCAN_TB - 1 - i)

    def zeros():
        return [[jnp.zeros((SUB, LANE), F32) for _ in range(vb)] for _ in range(npart)]

    def total(parts):
        return [functools.reduce(lambda a, b: a + b, [p[j] for p in parts]) for j in range(vb)]

    t0 = t_of(0)
    acc = zeros()
    for kx in range(RW_N):
        kkb = bcast(kk_ref, t0, kx)
        for j in range(vb):
            acc[kx % npart][j] = acc[kx % npart][j] + s_scr[kx, j * SUB:(j + 1) * SUB, :] * kkb

    def step(i, skk):
        t = t_of(i)
        tn = t_of(jnp.minimum(i + 1, SCAN_TB - 1))
        vv = [v_ref[t, j * SUB:(j + 1) * SUB, :] for j in range(vb)]
        oacc, nacc = zeros(), zeros()
        for kx in range(RW_N):
            wb = jnp.broadcast_to(w_ref[0, kx, pl.ds(t, 1), :], (SUB, LANE))
            ab = bcast(nkka_ref, t, kx)
            kb = bcast(k_ref, t, kx)
            rb = bcast(r_ref, t, kx)
            kkn = bcast(kk_ref, tn, kx)
            p = kx % npart
            for j in range(vb):
                s = s_scr[kx, j * SUB:(j + 1) * SUB, :] * wb + skk[j] * ab + vv[j] * kb
                s_scr[kx, j * SUB:(j + 1) * SUB, :] = s
                oacc[p][j] = oacc[p][j] + s * rb
                nacc[p][j] = nacc[p][j] + s * kkn
        for j, o in enumerate(total(oacc)):
            o_ref[0, t, j * SUB:(j + 1) * SUB, :] = o
        return tuple(total(nacc))

    lax.fori_loop(0, SCAN_TB, step, tuple(total(acc)))

    if want_final:
        @pl.when(tb == pl.num_programs(1) - 1)
        def _():
            sfin_ref[0] = s_scr[...]


def _rw_scan_call(r, w, k, v, kk, nkka, s0, *, want_final):
    G, L = w.shape[0], r.shape[1]
    nv = v.shape[1]
    ntb = L // SCAN_TB
    has_init = s0 is not None
    tmap = lambda g, t: t + g * (ntb - 1 - 2 * t)
    kspec = pl.BlockSpec((RW_N, SCAN_TB, LANE), lambda g, t: (0, tmap(g, t), 0))
    wspec = pl.BlockSpec((1, RW_N, SCAN_TB, LANE), lambda g, t: (g, 0, tmap(g, t), 0))
    vspec = pl.BlockSpec((SCAN_TB, nv, LANE), lambda g, t: (tmap(g, t), 0, 0))
    ospec = pl.BlockSpec((1, SCAN_TB, nv, LANE), lambda g, t: (g, tmap(g, t), 0, 0))
    sspec = pl.BlockSpec((1, RW_N, nv, LANE), lambda g, t: (g, 0, 0, 0))
    in_specs = [kspec, wspec, kspec, vspec, kspec, kspec]
    args = [r, w, k, v, kk, nkka]
    if has_init:
        in_specs.append(sspec)
        args.append(s0)
    out_specs = [ospec]
    out_shape = [jax.ShapeDtypeStruct((G, L, nv, LANE), F32)]
    if want_final:
        out_specs.append(sspec)
        out_shape.append(jax.ShapeDtypeStruct((G, RW_N, nv, LANE), F32))
    return pl.pallas_call(
        functools.partial(_rw_scan_kernel, vb=nv // SUB, npart=1,
                          has_init=has_init, want_final=want_final),
        grid=(G, ntb),
        in_specs=in_specs,
        out_specs=out_specs,
        out_shape=out_shape,
        scratch_shapes=[pltpu.VMEM((RW_N, nv, LANE), F32)],
        compiler_params=pltpu.CompilerParams(dimension_semantics=("arbitrary", "arbitrary"),
                                             vmem_limit_bytes=VMEM_LIMIT),
        name="rw_scan",
    )(*args)


def _merge_kernel(x_ref, mod_ref, ng_ref, wgate_ref, yssd_c, yssd_l, ygla_c, ygla_l, orw_c, orw_l,
                  bonus_ref, g_ref, lnw_ref, lnb_ref, bd_ref, wso_ref, wgo_ref, wro_ref, wout_ref,
                  o_ref):
    is_ctx = pl.program_id(0) < N_CTX_BLK
    x = x_ref[...]
    u = _norm_mod(x, ng_ref[...], mod_ref[0, 3:4, :], mod_ref[0, 4:5, :]).astype(BF16)
    bd = bd_ref[...]
    yssd = jnp.where(is_ctx, yssd_c[...], yssd_l[...])
    ygla = jnp.where(is_ctx, ygla_c[...], ygla_l[...])
    o = jnp.where(is_ctx, orw_c[0], orw_l[0]).T
    mu = _dot_sel(o, bd) * (1.0 / RW_N)
    oc = o - mu
    var = _dot_sel(oc * oc, bd) * (1.0 / RW_N)
    o = oc * lax.rsqrt(var + RW_GN_EPS) * lnw_ref[...] + lnb_ref[...]
    y_rw = ((o + bonus_ref[...]) * g_ref[...]).astype(BF16)
    merged = jnp.zeros((RB, D_MODEL), F32)
    for b, (y, wo_ref) in enumerate(((yssd, wso_ref), (ygla, wgo_ref), (y_rw, wro_ref))):
        gate = _sigmoid(jnp.dot(u, wgate_ref[:, b * D_MODEL:(b + 1) * D_MODEL],
                                preferred_element_type=F32))
        merged = merged + gate * jnp.dot(y, wo_ref[...], preferred_element_type=F32)
    m = jnp.dot(merged.astype(BF16), wout_ref[...], preferred_element_type=F32)
    o_ref[...] = x + mod_ref[0, 5:6, :] * m


def _merge_call(x, mod, ng, wgate, yssd_c, yssd_l, ygla_c, ygla_l, orw_c, orw_l, bonus, g, lnw, lnb,
                bd, wso, wgo, wro, wout):
    row_spec = pl.BlockSpec((RB, D_MODEL), lambda i: (i, 0))
    mix_spec = pl.BlockSpec((RB, D_MIX), lambda i: (i, 0))
    mix_c = pl.BlockSpec((RB, D_MIX), lambda i: (_ctx_idx(i), 0))
    mix_l = pl.BlockSpec((RB, D_MIX), lambda i: (_lat_idx(i), 0))
    t_c = pl.BlockSpec((1, D_MIX, RB), lambda i: (_ctx_idx(i), 0, 0))
    t_l = pl.BlockSpec((1, D_MIX, RB), lambda i: (_lat_idx(i), 0, 0))
    in_specs = [row_spec, pl.BlockSpec((1, N_MOD, D_MODEL), lambda i: (i, 0, 0)),
                _const_spec((1, D_MODEL)), _const_spec((D_MODEL, 3 * D_MODEL)),
                mix_c, mix_l, mix_c, mix_l, t_c, t_l, mix_spec, mix_spec,
                _const_spec((1, D_MIX)), _const_spec((1, D_MIX)), _const_spec((D_MIX, D_MIX)),
                _const_spec((D_MIX, D_MODEL)), _const_spec((D_MIX, D_MODEL)),
                _const_spec((D_MIX, D_MODEL)), _const_spec((D_MODEL, D_MODEL))]
    return pl.pallas_call(
        _merge_kernel,
        grid=(N_BLK,),
        in_specs=in_specs,
        out_specs=row_spec,
        out_shape=jax.ShapeDtypeStruct((N_TOK, D_MODEL), F32),
        compiler_params=pltpu.CompilerParams(dimension_semantics=("arbitrary",),
                                             vmem_limit_bytes=VMEM_LIMIT),
        name="merge",
    )(x, mod, ng, wgate, yssd_c, yssd_l, ygla_c, ygla_l, orw_c, orw_l, bonus, g, lnw, lnb, bd,
      wso, wgo, wro, wout)


def _grid_pos_embed(rows, cols, dim):
    quarter = dim // 4
    omega = 1.0 / (10000.0 ** (jnp.arange(quarter, dtype=F32) / quarter))
    er = jnp.arange(rows, dtype=F32)[:, None] * omega
    ec = jnp.arange(cols, dtype=F32)[:, None] * omega
    er = jnp.concatenate([jnp.sin(er), jnp.cos(er)], axis=-1)
    ec = jnp.concatenate([jnp.sin(ec), jnp.cos(ec)], axis=-1)
    emb = jnp.concatenate([jnp.broadcast_to(er[:, None], (rows, cols, dim // 2)),
                           jnp.broadcast_to(ec[None], (rows, cols, dim // 2))], axis=-1)
    return emb.reshape(rows * cols, dim)


def _pad_cols(a, n):
    return jnp.pad(a, [(0, 0)] * (a.ndim - 1) + [(0, n - a.shape[-1])])


def _rows_at(a, off, n):
    return jnp.pad(a, ((off, n - off - a.shape[0]), (0, 0)))


def _block_diag_ones(n, blk):
    i = np.arange(n)
    return jnp.asarray((i[:, None] // blk) == (i[None, :] // blk), BF16)


def _expand01(row0):
    m = np.zeros((LANE, D_MIX), np.float32)
    for h in range(SSD_H):
        m[row0 + h, h * SSD_P:(h + 1) * SSD_P] = 1.0
    return jnp.asarray(m, BF16)


def kernel(x_prompt, x_sample, state_ssd, state_gla, state_rwkv, c, c_ctx, norm_g, w_ada, b_ada,
           ffn_gate, ffn_up, ffn_down, w_in, ssd_conv_w, ssd_conv_b, ssd_dt_bias, ssd_A_log, ssd_D,
           ssd_norm, w_ssd_o, gla_gk_w, gla_gk_b, gla_norm, w_gla_o, rw_mu, rw_w0, rw_w2, rw_a0,
           rw_a2, rw_g2, rw_kk, rw_ka, rw_rk, rw_ln_w, rw_ln_b, w_rw_o, w_out, final_norm):
    pos = _grid_pos_embed(DEC_SEQ // GRID_W, GRID_W, D_MODEL)
    x = (x_prompt.reshape(N_CTX_TOK, D_MODEL), x_sample.reshape(-1, D_MODEL))
    s0_rw = state_rwkv.reshape(DEC_BATCH, DEPTH, 2, RW_H, RW_VH, RW_NV_LAT, RW_N).transpose(
        1, 6, 5, 4, 2, 0, 3).reshape(DEPTH, 1, RW_N, RW_NV_LAT, LANE)

    cond8 = jnp.concatenate([c_ctx[None], c, jnp.zeros((SUB - 1 - DEC_BATCH, D_MODEL), F32)])
    ada = _ada_call(cond8, w_ada, b_ada)
    cond_of_blk = np.concatenate([np.zeros(N_CTX_BLK, np.int32),
                                  1 + np.arange(N_BLK - N_CTX_BLK, dtype=np.int32) // LAT_BLK_PER_SEQ])

    bd = _block_diag_ones(D_MIX, RW_N)
    ef, eb = _expand01(0), _expand01(SSD_H)
    o_ssd = D_MIX + SSD_XBC + 2 * SSD_H
    o_gla = o_ssd + 2 * GLA_H * GLA_DK + 2 * D_MIX + 2 * GLA_LR
    o_rw = o_gla + 3 * D_MIX + 2 * RW_LW + RW_LA + RW_LG

    new_ssd, new_gla, new_rw = [], [], []
    for l in range(DEPTH):
        mod = ada[l][cond_of_blk].reshape(N_BLK, N_MOD, D_MODEL)
        ng = norm_g[l]
        wi = w_in[l]
        w_ssd = _pad_cols(wi[:, :o_ssd], SSD_W).astype(BF16)
        w_gla = _pad_cols(wi[:, o_ssd:o_gla], GLA_W).astype(BF16)
        wr_ = wi[:, o_gla:o_rw]
        w_rw = jnp.concatenate([wr_[:, :1664], wr_[:, 1728:1856], _pad_cols(wr_[:, 1664:1728], LANE)],
                               axis=1).astype(BF16)
        mu_ = rw_mu[l]
        mu = jnp.concatenate([mu_[:1664], mu_[1728:1856], _pad_cols(mu_[1664:1728], LANE)])[None]
        w_gate = wi[:, o_rw:].astype(BF16)

        x = _ffn_call(x, mod, ng[0:1], ffn_gate[l, 0].astype(BF16), ffn_up[l, 0].astype(BF16),
                      ffn_down[l, 0].astype(BF16), mod_row=0, pos=pos if l == 0 else None)

        ssd_args = (x, mod, ng[1:2], w_ssd, ssd_conv_w[l], ssd_conv_b[l][None],
                    _pad_cols(ssd_dt_bias[l].reshape(1, -1), LANE),
                    _pad_cols(ssd_A_log[l].reshape(1, -1), LANE),
                    jnp.repeat(ssd_D[l], SSD_P, axis=1), ssd_norm[l][None], ef, eb)
        y_ssd_c, s_ssd = _ssd_call(*ssd_args, None, ctx=True)
        (y_ssd_l,) = _ssd_call(*ssd_args, state_ssd[:, l], ctx=False)
        new_ssd.append(s_ssd)

        gkw = jnp.stack([_rows_at(gla_gk_w[l, d], d * GLA_LR, LANE) for d in range(2)])
        gla_args = (x, mod, ng[1:2], w_gla, gkw.astype(BF16), gla_gk_b[l],
                    gla_norm[l][None])
        y_gla_c, s_gla = _gla_call(*gla_args, None, ctx=True)
        (y_gla_l,) = _gla_call(*gla_args, state_gla[:, l], ctx=False)
        new_gla.append(s_gla)

        w2p = jnp.stack([_rows_at(rw_w2[l, d], d * RW_LW, LANE) for d in range(2)]).astype(BF16)
        r, wf, wb, k, v, kk, nkka, g, bonus = _rw_prep_call(
            x, mod, ng[1:2], w_rw, mu, rw_a0[l][None], _rows_at(rw_a2[l], 0, LANE).astype(BF16),
            rw_g2[l].astype(BF16), rw_kk[l][None], rw_ka[l][None], rw_rk[l].reshape(1, D_MIX),
            rw_w0[l], w2p, bd)
        rc, kc, kkc, ac, wc = _rl_k_call(r, k, kk, nkka, wf, wb, lat=False)
        o_c, s_rw = _rw_scan_call(rc, wc, kc, _rl_v_call(v, lat=False), kkc, ac, None,
                                  want_final=True)
        new_rw.append(s_rw.reshape(2, RW_N, RW_N, BATCH, RW_H).transpose(3, 0, 4, 2, 1))
        rl, kl, kkl, al, wl = _rl_k_call(r, k, kk, nkka, wf, wb, lat=True)
        (o_l,) = _rw_scan_call(rl, wl, kl, _rl_v_call(v, lat=True), kkl, al, s0_rw[l],
                               want_final=False)

        x = _merge_call(x, mod, ng[1:2], w_gate, y_ssd_c, y_ssd_l, y_gla_c, y_gla_l,
                        _rl_out_call(o_c, lat=False), _rl_out_call(o_l, lat=True), bonus, g,
                        rw_ln_w[l][None], rw_ln_b[l][None], bd, w_ssd_o[l].astype(BF16),
                        w_gla_o[l].astype(BF16), w_rw_o[l].astype(BF16), w_out[l].astype(BF16))

        x = _ffn_call(x, mod, ng[2:3], ffn_gate[l, 1].astype(BF16), ffn_up[l, 1].astype(BF16),
                      ffn_down[l, 1].astype(BF16), mod_row=6,
                      final_g=final_norm[None] if l == DEPTH - 1 else None)

    y_prompt = x[0].reshape(BATCH, SEQ, D_MODEL)
    y_sample = x[1].reshape(DEC_BATCH, DEC_SEQ, D_MODEL)
    return (y_prompt, y_sample, jnp.stack(new_ssd, axis=1), jnp.stack(new_gla, axis=1),
            jnp.stack(new_rw, axis=1))
```

```python
import functools
import math

import numpy as np
import jax
import jax.numpy as jnp
from jax import lax
from jax.experimental import pallas as pl
from jax.experimental.pallas import tpu as pltpu

F32 = jnp.float32
BF16 = jnp.bfloat16

D_MODEL = 1024
BATCH = 16
SEQ = 256
DEPTH = 2
DEC_BATCH = 4
DEC_SEQ = 1024
GRID_W = 64
D_MIX = 512
D_FF = 2816
N_MOD = 9
SSD_P = 64
SSD_H = 8
SSD_N = 64
SSD_G = 2
SSD_XBC = 768
GLA_H = 4
GLA_DK = 64
GLA_DV = 128
GLA_LR = 16
GLA_GATE_NORM = 16.0
GLA_CHUNK = 64
RW_N = 64
RW_H = 8
RW_LW = 64
RW_LA = 64
RW_LG = 128
RMS_EPS = 1e-6
RW_GN_EPS = 64e-5

LANE = 128
SUB = 8
RB = 256
N_CTX_TOK = BATCH * SEQ
N_TOK = N_CTX_TOK + DEC_BATCH * DEC_SEQ
N_BLK = N_TOK // RB
N_CTX_BLK = N_CTX_TOK // RB
LAT_BLK_PER_SEQ = DEC_SEQ // RB
SSD_W = D_MIX + SSD_XBC + LANE
GLA_QK = GLA_H * GLA_DK
GLA_W = 2 * GLA_QK + 2 * D_MIX + LANE
RW_W = 3 * D_MIX + 3 * LANE
SCAN_TB = 32
FFN_SUB = 1
VMEM_LIMIT = 56 * 1024 * 1024


def _dot(a, b):
    return jnp.dot(a.astype(BF16), b.astype(BF16), preferred_element_type=F32)


def _dot_nt(a, b):
    return lax.dot_general(a.astype(BF16), b.astype(BF16), (((1,), (1,)), ((), ())),
                           preferred_element_type=F32)


def _split3(x):
    hi = x.astype(BF16)
    r1 = x - hi.astype(F32)
    mid = r1.astype(BF16)
    lo = (r1 - mid.astype(F32)).astype(BF16)
    return hi, mid, lo


def _sel_dot(m01, x):
    hi, mid, lo = _split3(x)
    f = lambda p: jnp.dot(m01, p, preferred_element_type=F32)
    return f(hi) + f(mid) + f(lo)


def _dot_sel(x, m01, terms=3):
    f = lambda p: jnp.dot(p, m01, preferred_element_type=F32)
    return functools.reduce(lambda a, b: a + b, [f(p) for p in _split3(x)[:terms]])


def _sigmoid(x):
    return 0.5 * jnp.tanh(0.5 * x) + 0.5


def _silu(x):
    return x * _sigmoid(x)


def _softplus(x):
    return jnp.maximum(x, 0.0) + jnp.log(1.0 + jnp.exp(-jnp.abs(x)))


def _rmsnorm(x, g):
    return x * lax.rsqrt(jnp.mean(x * x, axis=-1, keepdims=True) + RMS_EPS) * g


def _norm_mod(x, g, shift, scale):
    return _rmsnorm(x, g) * (1.0 + scale) + shift


def _iota(shape, dim):
    return lax.broadcasted_iota(jnp.int32, shape, dim)


def _tri01(n, lower, chunk=None):
    t = _iota((n, n), 0)
    s = _iota((n, n), 1)
    m = (s <= t) if lower else (s >= t)
    if chunk is not None:
        m = m & ((t // chunk) == (s // chunk))
    return m


def _ada_kernel(c_ref, w_ref, b_ref, o_ref):
    o_ref[0] = _dot(_silu(c_ref[...]), w_ref[0]) + b_ref[0]


def _ada_call(cond8, w_ada, b_ada):
    tn = 2304
    nj = (N_MOD * D_MODEL) // tn
    return pl.pallas_call(
        _ada_kernel,
        grid=(DEPTH, nj),
        in_specs=[pl.BlockSpec((SUB, D_MODEL), lambda l, j: (0, 0)),
                  pl.BlockSpec((1, D_MODEL, tn), lambda l, j: (l, 0, j)),
                  pl.BlockSpec((1, 1, tn), lambda l, j: (l, 0, j))],
        out_specs=pl.BlockSpec((1, SUB, tn), lambda l, j: (l, 0, j)),
        out_shape=jax.ShapeDtypeStruct((DEPTH, SUB, N_MOD * D_MODEL), F32),
        compiler_params=pltpu.CompilerParams(dimension_semantics=("arbitrary", "arbitrary"),
                                             vmem_limit_bytes=VMEM_LIMIT),
        name="ada",
    )(cond8, w_ada, b_ada.reshape(DEPTH, 1, N_MOD * D_MODEL))


def _ffn_kernel(*refs, mod_row, first, last):
    it = iter(refs)
    is_ctx = pl.program_id(0) < N_CTX_BLK // FFN_SUB
    if first:
        xc_ref, xl_ref, pos_ref = next(it), next(it), next(it)
        x = jnp.where(is_ctx, xc_ref[...], xl_ref[...] + pos_ref[...])
    else:
        x = next(it)[...]
    mod_ref, ng_ref, wg_ref, wu_ref, wd_ref = next(it), next(it), next(it), next(it), next(it)
    fin_ref = next(it) if last else None
    shift = mod_ref[0, mod_row:mod_row + 1, :]
    scale = mod_ref[0, mod_row + 1:mod_row + 2, :]
    gate = mod_ref[0, mod_row + 2:mod_row + 3, :]
    h = _norm_mod(x, ng_ref[...], shift, scale).astype(BF16)
    a = _silu(jnp.dot(h, wg_ref[0, 0], preferred_element_type=F32))
    a = (a * jnp.dot(h, wu_ref[0, 0], preferred_element_type=F32)).astype(BF16)
    y = x + 0.5 * gate * jnp.dot(a, wd_ref[0, 0], preferred_element_type=F32)
    if last:
        y = _rmsnorm(y, fin_ref[...])
        oc_ref, ol_ref = next(it), next(it)

        @pl.when(is_ctx)
        def _():
            oc_ref[...] = y

        @pl.when(jnp.logical_not(is_ctx))
        def _():
            ol_ref[...] = y
    else:
        next(it)[...] = y


def _const_spec(shape):
    nd = len(shape)
    return pl.BlockSpec(shape, lambda i: (0,) * nd, pipeline_mode=pl.Buffered(1))


def _stacked_spec(shape, lead):
    nd = len(shape)
    return pl.BlockSpec((1,) * len(lead) + shape, lambda i: tuple(lead) + (0,) * nd,
                        pipeline_mode=pl.Buffered(1))


def _ctx_idx(i):
    return jnp.minimum(i, N_CTX_BLK - 1)


def _lat_idx(i):
    return jnp.maximum(i - N_CTX_BLK, 0)


def _ffn_call(x, mod, ng, wg, wu, wd, lj, *, mod_row, pos=None, final_g=None):
    first = pos is not None
    last = final_g is not None
    fb = FFN_SUB * RB
    nctx = N_CTX_BLK // FFN_SUB
    row_spec = pl.BlockSpec((fb, D_MODEL), lambda i: (i, 0))
    ctx_spec = pl.BlockSpec((fb, D_MODEL), lambda i: (jnp.minimum(i, nctx - 1), 0))
    lat_spec = pl.BlockSpec((fb, D_MODEL), lambda i: (jnp.maximum(i - nctx, 0), 0))
    if first:
        in_specs = [ctx_spec, lat_spec,
                    pl.BlockSpec((fb, D_MODEL),
                                 lambda i: (jnp.maximum(i - nctx, 0) % (DEC_SEQ // fb), 0))]
        args = [x[0], x[1], pos]
    else:
        in_specs = [row_spec]
        args = [x]
    in_specs += [pl.BlockSpec((1, N_MOD, D_MODEL), lambda i: (FFN_SUB * i, 0, 0)),
                 _const_spec((1, D_MODEL)), _stacked_spec((D_MODEL, D_FF), lj),
                 _stacked_spec((D_MODEL, D_FF), lj), _stacked_spec((D_FF, D_MODEL), lj)]
    args += [mod, ng, wg, wu, wd]
    if last:
        in_specs.append(_const_spec((1, D_MODEL)))
        args.append(final_g)
        half = jax.ShapeDtypeStruct((N_TOK // 2, D_MODEL), F32)
        out_specs, out_shape = [ctx_spec, lat_spec], [half, half]
    else:
        out_specs, out_shape = row_spec, jax.ShapeDtypeStruct((N_TOK, D_MODEL), F32)
    return pl.pallas_call(
        functools.partial(_ffn_kernel, mod_row=mod_row, first=first, last=last),
        grid=(N_BLK // FFN_SUB,),
        in_specs=in_specs,
        out_specs=out_specs,
        out_shape=out_shape,
        compiler_params=pltpu.CompilerParams(dimension_semantics=("arbitrary",),
                                             vmem_limit_bytes=VMEM_LIMIT),
        name="ffn",
    )(*args)


def _ssd_kernel(*refs, L, has_init, want_final):
    it = iter(refs)
    x_ref, mod_ref, ng_ref, w_ref = next(it), next(it), next(it), next(it)
    cw_ref, cb_ref, dtb_ref, alog_ref, d_ref, nw_ref = (next(it), next(it), next(it), next(it),
                                                        next(it), next(it))
    ef_ref, eb_ref = next(it), next(it)
    s0_ref = next(it) if has_init else None
    y_ref = next(it)
    sfin_ref = next(it) if want_final else None
    yacc = next(it)

    C = RB
    nc = L // C
    x = x_ref[...]
    u = _norm_mod(x, ng_ref[...], mod_ref[0, 3:4, :], mod_ref[0, 4:5, :]).astype(BF16)
    p = jnp.dot(u, w_ref[...], preferred_element_type=F32)
    z = p[:, :D_MIX]
    xbc = p[:, D_MIX:D_MIX + SSD_XBC]
    dtp = p[:, D_MIX + SSD_XBC:]
    rows = _iota((L, 1), 0)
    prev = jnp.where(rows == 0, 0.0, pltpu.roll(xbc, 1, 0))
    nxt = jnp.where(rows == L - 1, 0.0, pltpu.roll(xbc, L - 1, 0))
    xc = cb_ref[...] + prev * cw_ref[0:1, :] + xbc * cw_ref[1:2, :] + nxt * cw_ref[2:3, :]
    xc = _silu(xc)
    xs = xc[:, :D_MIX]
    bm = xc[:, D_MIX:D_MIX + SSD_G * SSD_N]
    cm = xc[:, D_MIX + SSD_G * SSD_N:]
    dt = _softplus(dtp + dtb_ref[...])
    adt = dt * (-jnp.exp(alog_ref[...]))

    tril = _tri01(C, True)
    triu = _tri01(C, False)
    tril_b = jnp.where(tril, 1.0, 0.0).astype(BF16)
    triu_b = jnp.where(triu, 1.0, 0.0).astype(BF16)
    ef = ef_ref[...]
    eb = eb_ref[...]

    cs, csT, rcs, rcsT, dtT, loc_f, loc_b = [], [], [], [], [], [], []
    for c in range(nc):
        a_c = adt[c * C:(c + 1) * C]
        a_cT = a_c.T
        cs.append(_sel_dot(tril_b, a_c))
        rcs.append(_sel_dot(triu_b, a_c))
        csT.append(_dot_sel(a_cT, triu_b))
        rcsT.append(_dot_sel(a_cT, tril_b))
        dtT.append(dt[c * C:(c + 1) * C].T)

    need_states = want_final or nc > 1
    if need_states:
        for c in range(nc):
            xs_c = xs[c * C:(c + 1) * C]
            dt_c = dt[c * C:(c + 1) * C]
            wf = jnp.exp(cs[c][C - 1:C, :] - cs[c]) * dt_c
            wb = jnp.exp(rcs[c][0:1, :] - rcs[c]) * dt_c
            xwf = (xs_c * _dot_sel(wf, ef, terms=2)).T
            xwb = (xs_c * _dot_sel(wb, eb, terms=2)).T
            lf, lb = [], []
            for h in range(SSD_H):
                g = h // (SSD_H // SSD_G)
                bm_g = bm[c * C:(c + 1) * C, g * SSD_N:(g + 1) * SSD_N]
                lf.append(_dot(xwf[h * SSD_P:(h + 1) * SSD_P, :], bm_g))
                lb.append(_dot(xwb[h * SSD_P:(h + 1) * SSD_P, :], bm_g))
            loc_f.append(lf)
            loc_b.append(lb)

    zero_s = jnp.zeros((SSD_P, SSD_N), F32)
    sin_f = [[None] * SSD_H for _ in range(nc + 1)]
    sin_b = [[None] * SSD_H for _ in range(nc + 1)]
    for h in range(SSD_H):
        sin_f[0][h] = s0_ref[0, 0, h] if has_init else zero_s
        sin_b[nc][h] = s0_ref[0, 1, h] if has_init else zero_s
    if need_states:
        for c in range(nc):
            dec = jnp.exp(cs[c][C - 1:C, :])
            for h in range(SSD_H):
                sin_f[c + 1][h] = dec[:, h:h + 1] * sin_f[c][h] + loc_f[c][h]
        for c in range(nc - 1, -1, -1):
            dec = jnp.exp(rcs[c][0:1, :])
            for h in range(SSD_H):
                sin_b[c][h] = dec[:, SSD_H + h:SSD_H + h + 1] * sin_b[c + 1][h] + loc_b[c][h]

    for c in range(nc):
        sl = slice(c * C, (c + 1) * C)
        ecs = jnp.exp(cs[c])
        ercs = jnp.exp(rcs[c])
        for h in range(SSD_H):
            g = h // (SSD_H // SSD_G)
            cm_g = cm[sl, g * SSD_N:(g + 1) * SSD_N]
            bm_g = bm[sl, g * SSD_N:(g + 1) * SSD_N]
            cb = _dot_nt(cm_g, bm_g)
            lf = jnp.exp(jnp.where(tril, cs[c][:, h:h + 1] - csT[c][h:h + 1, :], -jnp.inf))
            lb = jnp.exp(jnp.where(triu, rcs[c][:, SSD_H + h:SSD_H + h + 1]
                                   - rcsT[c][SSD_H + h:SSD_H + h + 1, :], -jnp.inf))
            m = cb * (lf * dtT[c][h:h + 1, :] + lb * dtT[c][SSD_H + h:SSD_H + h + 1, :])
            y_h = _dot(m, xs[sl, h * SSD_P:(h + 1) * SSD_P])
            if has_init or nc > 1:
                y_h = y_h + ecs[:, h:h + 1] * _dot_nt(cm_g, sin_f[c][h])
                y_h = y_h + ercs[:, SSD_H + h:SSD_H + h + 1] * _dot_nt(cm_g, sin_b[c + 1][h])
            yacc[sl, h * SSD_P:(h + 1) * SSD_P] = y_h

    y = yacc[...] + xs * (d_ref[0:1, :] + d_ref[1:2, :])
    y = _rmsnorm(y * _silu(z), nw_ref[...])
    y_ref[...] = y.astype(BF16)
    if want_final:
        for h in range(SSD_H):
            sfin_ref[0, 0, h] = sin_f[nc][h]
            sfin_ref[0, 1, h] = sin_b[0][h]


def _ssd_call(x, mod, ng, w, cw, cb, dtb, alog, dexp, nw, ef, eb, s0, *, ctx):
    L = SEQ if ctx else DEC_SEQ
    nseq = BATCH if ctx else DEC_BATCH
    blk0 = 0 if ctx else N_CTX_TOK // L
    mod_of = (lambda i: (i, 0, 0)) if ctx else (lambda i: (N_CTX_BLK + LAT_BLK_PER_SEQ * i, 0, 0))
    in_specs = [pl.BlockSpec((L, D_MODEL), lambda i: (blk0 + i, 0)),
                pl.BlockSpec((1, N_MOD, D_MODEL), mod_of),
                _const_spec((1, D_MODEL)), _const_spec((D_MODEL, SSD_W)),
                _const_spec((3, SSD_XBC)), _const_spec((1, SSD_XBC)),
                _const_spec((1, LANE)), _const_spec((1, LANE)),
                _const_spec((2, D_MIX)), _const_spec((1, D_MIX)),
                _const_spec((LANE, D_MIX)), _const_spec((LANE, D_MIX))]
    args = [x, mod, ng, w, cw, cb, dtb, alog, dexp, nw, ef, eb]
    st_spec = pl.BlockSpec((1, 2, SSD_H, SSD_P, SSD_N), lambda i: (i, 0, 0, 0, 0))
    out_specs = [pl.BlockSpec((L, D_MIX), lambda i: (i, 0))]
    out_shape = [jax.ShapeDtypeStruct((nseq * L, D_MIX), BF16)]
    if ctx:
        out_specs.append(st_spec)
        out_shape.append(jax.ShapeDtypeStruct((nseq, 2, SSD_H, SSD_P, SSD_N), F32))
    else:
        in_specs.append(st_spec)
        args.append(s0)
    return pl.pallas_call(
        functools.partial(_ssd_kernel, L=L, has_init=not ctx, want_final=ctx),
        grid=(nseq,),
        in_specs=in_specs,
        out_specs=out_specs,
        out_shape=out_shape,
        scratch_shapes=[pltpu.VMEM((L, D_MIX), F32)],
        compiler_params=pltpu.CompilerParams(dimension_semantics=("arbitrary",),
                                             vmem_limit_bytes=VMEM_LIMIT),
        name="ssd_ctx" if ctx else "ssd_lat",
    )(*args)


def _gla_kernel(*refs, L, has_init, want_final):
    it = iter(refs)
    x_ref, mod_ref, ng_ref, w_ref = next(it), next(it), next(it), next(it)
    gkw_ref, gkb_ref, nw_ref = next(it), next(it), next(it)
    s0_ref = next(it) if has_init else None
    y_ref = next(it)
    sfin_ref = next(it) if want_final else None
    oacc = next(it)

    C = GLA_CHUNK
    B = RB
    cpb = B // C
    nb = L // B
    x = x_ref[...]
    u = _norm_mod(x, ng_ref[...], mod_ref[0, 3:4, :], mod_ref[0, 4:5, :]).astype(BF16)
    p = jnp.dot(u, w_ref[...], preferred_element_type=F32)
    q = p[:, :GLA_QK] * (GLA_DK ** -0.5)
    k = p[:, GLA_QK:2 * GLA_QK]
    v = p[:, 2 * GLA_QK:2 * GLA_QK + D_MIX]
    gg = p[:, 2 * GLA_QK + D_MIX:2 * GLA_QK + 2 * D_MIX]
    lr = p[:, 2 * GLA_QK + 2 * D_MIX:].astype(BF16)
    lg = []
    for d in range(2):
        pre = jnp.dot(lr, gkw_ref[d], preferred_element_type=F32) + gkb_ref[d:d + 1, :]
        lg.append(-_softplus(-pre) / GLA_GATE_NORM)

    low = _tri01(B, True, C)
    upp = _tri01(B, False, C)
    low_b = jnp.where(low, 1.0, 0.0).astype(BF16)
    upp_b = jnp.where(upp, 1.0, 0.0).astype(BF16)
    rowi = _iota((B, 1), 0)

    if has_init:
        s_f = [s0_ref[0, 0, h] for h in range(GLA_H)]
        s_b = [s0_ref[0, 1, h] for h in range(GLA_H)]
    else:
        s_f = [jnp.zeros((GLA_DK, GLA_DV), F32)] * GLA_H
        s_b = list(s_f)

    hpp = LANE // GLA_DK
    lane_head = _iota((1, LANE), 1) // GLA_DK

    def pair(h):
        return slice((h // hpp) * LANE, (h // hpp + 1) * LANE)

    def only(h, a):
        return jnp.where(lane_head == h % hpp, a[:, pair(h)], 0.0)

    def pair_state(s, h):
        j = h // hpp
        return jnp.concatenate(s[j * hpp:(j + 1) * hpp], axis=0)

    blocks = []
    for b in range(nb):
        sl = slice(b * B, (b + 1) * B)
        bf = _sel_dot(low_b, lg[0][sl])
        rb = _sel_dot(upp_b, lg[1][sl])
        q_b, k_b = q[sl], k[sl]
        qf, qb = q_b * jnp.exp(bf), q_b * jnp.exp(rb)
        blocks.append(dict(sl=sl, bf=bf, rb=rb, bfT=bf.T, rbT=rb.T, k=k_b,
                           kf=k_b * jnp.exp(-bf), kb=k_b * jnp.exp(-rb),
                           qf=[only(h, qf) for h in range(GLA_H)],
                           qb=[only(h, qb) for h in range(GLA_H)]))

    for blk in blocks:
        sl = blk["sl"]
        for h in range(GLA_H):
            a = jnp.where(low, _dot_nt(blk["qf"][h], blk["kf"][:, pair(h)]), 0.0)
            a = a + jnp.where(upp, _dot_nt(blk["qb"][h], blk["kb"][:, pair(h)]), 0.0)
            oacc[sl, h * GLA_DV:(h + 1) * GLA_DV] = _dot(a, v[sl, h * GLA_DV:(h + 1) * GLA_DV])

    def inter(blk, c, qd, cum, cumT, edge, s):
        sl = blk["sl"]
        rs = slice(sl.start + c * C, sl.start + (c + 1) * C)
        kdecT = (blk["k"] * jnp.exp(cum[edge:edge + 1, :] - cum)).T
        in_c = (rowi >= c * C) & (rowi < (c + 1) * C)
        s_in = [pair_state(s, h) for h in range(0, GLA_H, hpp)]
        for h in range(GLA_H):
            ks = slice(h * GLA_DK, (h + 1) * GLA_DK)
            vs = slice(h * GLA_DV, (h + 1) * GLA_DV)
            oacc[rs, vs] = oacc[rs, vs] + _dot(qd[h][c * C:(c + 1) * C], s_in[h // hpp])
            kv = _dot(kdecT[ks, :], jnp.where(in_c, v[sl, vs], 0.0))
            s[h] = jnp.exp(cumT[ks, edge:edge + 1]) * s[h] + kv

    for blk in blocks:
        for c in range(cpb):
            inter(blk, c, blk["qf"], blk["bf"], blk["bfT"], (c + 1) * C - 1, s_f)
    for blk in reversed(blocks):
        for c in range(cpb - 1, -1, -1):
            inter(blk, c, blk["qb"], blk["rb"], blk["rbT"], c * C, s_b)

    for h in range(GLA_H):
        vs = slice(h * GLA_DV, (h + 1) * GLA_DV)
        o_h = _rmsnorm(oacc[:, vs], nw_ref[...])
        y_ref[:, vs] = (o_h * _silu(gg[:, vs])).astype(BF16)
        if want_final:
            sfin_ref[0, 0, h] = s_f[h]
            sfin_ref[0, 1, h] = s_b[h]


def _gla_call(x, mod, ng, w, gkw, gkb, nw, s0, *, ctx):
    L = SEQ if ctx else DEC_SEQ
    nseq = BATCH if ctx else DEC_BATCH
    blk0 = 0 if ctx else N_CTX_TOK // L
    mod_of = (lambda i: (i, 0, 0)) if ctx else (lambda i: (N_CTX_BLK + LAT_BLK_PER_SEQ * i, 0, 0))
    in_specs = [pl.BlockSpec((L, D_MODEL), lambda i: (blk0 + i, 0)),
                pl.BlockSpec((1, N_MOD, D_MODEL), mod_of),
                _const_spec((1, D_MODEL)), _const_spec((D_MODEL, GLA_W)),
                _const_spec((2, LANE, GLA_QK)), _const_spec((2, GLA_QK)),
                _const_spec((1, GLA_DV))]
    args = [x, mod, ng, w, gkw, gkb, nw]
    st_spec = pl.BlockSpec((1, 2, GLA_H, GLA_DK, GLA_DV), lambda i: (i, 0, 0, 0, 0))
    out_specs = [pl.BlockSpec((L, D_MIX), lambda i: (i, 0))]
    out_shape = [jax.ShapeDtypeStruct((nseq * L, D_MIX), BF16)]
    if ctx:
        out_specs.append(st_spec)
        out_shape.append(jax.ShapeDtypeStruct((nseq, 2, GLA_H, GLA_DK, GLA_DV), F32))
    else:
        in_specs.append(st_spec)
        args.append(s0)
    return pl.pallas_call(
        functools.partial(_gla_kernel, L=L, has_init=not ctx, want_final=ctx),
        grid=(nseq,),
        in_specs=in_specs,
        out_specs=out_specs,
        out_shape=out_shape,
        scratch_shapes=[pltpu.VMEM((L, D_MIX), F32)],
        compiler_params=pltpu.CompilerParams(dimension_semantics=("arbitrary",),
                                             vmem_limit_bytes=VMEM_LIMIT),
        name="gla_ctx" if ctx else "gla_lat",
    )(*args)


def _rw_prep_kernel(x_ref, xp_ref, xn_ref, mod_ref, ng_ref, w_ref, mu_ref, a0_ref, a2_ref, g2_ref,
                    kkw_ref, ka_ref, rk_ref, w0_ref, w2_ref, bd_ref,
                    r_ref, wf_ref, wb_ref, k_ref, v_ref, kk_ref, nkka_ref, g_ref, bonus_ref):
    i = pl.program_id(0)
    j = (i - N_CTX_BLK) % LAT_BLK_PER_SEQ
    is_first = (i < N_CTX_BLK) | (j == 0)
    is_last = (i < N_CTX_BLK) | (j == LAT_BLK_PER_SEQ - 1)
    x_all = jnp.concatenate([x_ref[...], xp_ref[...], xn_ref[...]], axis=0)
    u_all = _norm_mod(x_all, ng_ref[...], mod_ref[0, 3:4, :], mod_ref[0, 4:5, :]).astype(BF16)
    p_all = jnp.dot(u_all, w_ref[...], preferred_element_type=F32)
    p = p_all[:RB]
    p_prev = jnp.where(is_first, 0.0, p_all[RB + SUB - 1:RB + SUB, :])
    p_next = jnp.where(is_last, 0.0, p_all[RB + SUB:RB + SUB + 1, :])
    rows = _iota((RB, 1), 0)
    prev = jnp.where(rows == 0, p_prev, pltpu.roll(p, 1, 0))
    nxt = jnp.where(rows == RB - 1, p_next, pltpu.roll(p, RB - 1, 0))
    p = p + (0.5 * (prev + nxt) - p) * mu_ref[...]

    r = p[:, :D_MIX]
    k = p[:, D_MIX:2 * D_MIX]
    v = p[:, 2 * D_MIX:3 * D_MIX]
    wlr = p[:, 3 * D_MIX:3 * D_MIX + LANE]
    glr = p[:, 3 * D_MIX + LANE:3 * D_MIX + 2 * LANE]
    alr = p[:, 3 * D_MIX + 2 * LANE:]
    bd = bd_ref[...]
    a = _sigmoid(a0_ref[...] + _dot(alr, a2_ref[...]))
    g = _dot(_sigmoid(glr), g2_ref[...])
    kk = k * kkw_ref[...]
    kk = kk / jnp.maximum(jnp.sqrt(_dot_sel(kk * kk, bd)), 1e-12)
    k = k * (1.0 + (a - 1.0) * ka_ref[...])
    tw = jnp.tanh(wlr).astype(BF16)
    for d, o_ref in ((0, wf_ref), (1, wb_ref)):
        pre = w0_ref[d:d + 1, :] + jnp.dot(tw, w2_ref[d], preferred_element_type=F32)
        o_ref[0] = jnp.exp(-math.exp(-0.5) * _sigmoid(pre)).T
    r_ref[0] = r.T
    k_ref[0] = k.T
    v_ref[0] = v.T
    kk_ref[0] = kk.T
    nkka_ref[0] = (-(kk * a)).T
    g_ref[...] = g
    bonus_ref[...] = _dot_sel(r * k * rk_ref[...], bd, terms=2) * v


def _rw_prep_call(x, mod, ng, w, mu, a0, a2, g2, kkw, ka, rk, w0, w2, bd):
    hb = RB // SUB
    row_spec = pl.BlockSpec((RB, D_MODEL), lambda i: (i, 0))
    out_spec = pl.BlockSpec((RB, D_MIX), lambda i: (i, 0))
    in_specs = [row_spec,
                pl.BlockSpec((SUB, D_MODEL), lambda i: (jnp.maximum(i * hb - 1, 0), 0)),
                pl.BlockSpec((SUB, D_MODEL), lambda i: (jnp.minimum((i + 1) * hb, N_TOK // SUB - 1), 0)),
                pl.BlockSpec((1, N_MOD, D_MODEL), lambda i: (i, 0, 0)),
                _const_spec((1, D_MODEL)), _const_spec((D_MODEL, RW_W)), _const_spec((1, RW_W)),
                _const_spec((1, D_MIX)), _const_spec((LANE, D_MIX)), _const_spec((LANE, D_MIX)),
                _const_spec((1, D_MIX)), _const_spec((1, D_MIX)), _const_spec((1, D_MIX)),
                _const_spec((2, D_MIX)), _const_spec((2, LANE, D_MIX)), _const_spec((D_MIX, D_MIX))]
    t_spec = pl.BlockSpec((1, D_MIX, RB), lambda i: (i, 0, 0))
    t_shape = jax.ShapeDtypeStruct((N_BLK, D_MIX, RB), F32)
    tok_shape = jax.ShapeDtypeStruct((N_TOK, D_MIX), F32)
    return pl.pallas_call(
        _rw_prep_kernel,
        grid=(N_BLK,),
        in_specs=in_specs,
        out_specs=[t_spec] * 7 + [out_spec] * 2,
        out_shape=[t_shape] * 7 + [tok_shape] * 2,
        compiler_params=pltpu.CompilerParams(dimension_semantics=("arbitrary",),
                                             vmem_limit_bytes=VMEM_LIMIT),
        name="rw_prep",
    )(x, x, x, mod, ng, w, mu, a0, a2, g2, kkw, ka, rk, w0, w2, bd)


RW_VH = LANE // (2 * DEC_BATCH * RW_H)
RW_NV_LAT = RW_N // RW_VH
LAT_ROWS = DEC_BATCH * RW_H
LAT_VIEW = (N_BLK // LAT_BLK_PER_SEQ, LAT_BLK_PER_SEQ, RW_H, RW_N, RB)


def _rl_rows(ref, nl, lat):
    x = ref[:, 0, :, nl, :] if lat else ref[:, :, nl, :]
    return x.reshape(-1, RB)


def _time_flip(x):
    ex = _iota((RB, RB), 0) + _iota((RB, RB), 1) == RB - 1
    return _dot_sel(x, jnp.where(ex, 1.0, 0.0).astype(BF16))


def _lat_spec(n_of, back):
    def index(j, n):
        return (1, LAT_BLK_PER_SEQ - 1 - j if back else j, 0, n_of(n), 0)
    return pl.BlockSpec((DEC_BATCH, 1, RW_H, SUB, RB), index)


def _rl_k_ctx_kernel(r_ref, k_ref, kk_ref, a_ref, wf_ref, wb_ref, ro, ko, kko, ao, wo):
    for nl in range(SUB):
        ro[nl] = _rl_rows(r_ref, nl, False).T
        ko[nl] = _rl_rows(k_ref, nl, False).T
        kko[nl] = _rl_rows(kk_ref, nl, False).T
        ao[nl] = _rl_rows(a_ref, nl, False).T
        wo[0, nl] = _rl_rows(wf_ref, nl, False).T
        wo[1, nl] = _rl_rows(wb_ref, nl, False).T


def _rl_k_lat_kernel(rf, rb, kf, kb, kkf, kkb, af, ab, wff, wbb, ro, ko, kko, ao, wo):
    sets = ((rf, rb, ro), (kf, kb, ko), (kkf, kkb, kko), (af, ab, ao), (wff, wbb, wo))
    back = _time_flip(jnp.concatenate(
        [_rl_rows(b, nl, True) for _, b, _ in sets for nl in range(SUB)], axis=0))
    i = 0
    for f, _, o in sets:
        for nl in range(SUB):
            pair = [_rl_rows(f, nl, True), back[i * LAT_ROWS:(i + 1) * LAT_ROWS]]
            o[nl] = jnp.concatenate(pair * RW_VH, axis=0).T
            i += 1


def _rl_k_call(r, k, kk, nkka, wf, wb, *, lat):
    nb = RW_N // SUB
    if lat:
        L, G = DEC_SEQ, 1
        fwd = _lat_spec(lambda n: n, False)
        bwd = _lat_spec(lambda n: n, True)
        ospec = pl.BlockSpec((SUB, RB, LANE), lambda j, n: (n, j, 0))
        args = [a.reshape(LAT_VIEW) for a in (r, r, k, k, kk, kk, nkka, nkka, wf, wb)]
        call = dict(grid=(LAT_BLK_PER_SEQ, nb), in_specs=[fwd, bwd] * 5, out_specs=[ospec] * 5)
        body, name = _rl_k_lat_kernel, "rl_k_lat"
        wshape = jax.ShapeDtypeStruct((RW_N, L, LANE), F32)
    else:
        L, G = SEQ, 2
        ispec = pl.BlockSpec((BATCH, RW_H, SUB, RB), lambda n: (0, 0, n, 0))
        ospec = pl.BlockSpec((SUB, RB, LANE), lambda n: (n, 0, 0))
        wspec = pl.BlockSpec((2, SUB, RB, LANE), lambda n: (0, n, 0, 0))
        args = [a.reshape(N_BLK, RW_H, RW_N, RB) for a in (r, k, kk, nkka, wf, wb)]
        call = dict(grid=(nb,), in_specs=[ispec] * 6, out_specs=[ospec] * 4 + [wspec])
        body, name = _rl_k_ctx_kernel, "rl_k_ctx"
        wshape = jax.ShapeDtypeStruct((2, RW_N, L, LANE), F32)
    kshape = jax.ShapeDtypeStruct((RW_N, L, LANE), F32)
    outs = pl.pallas_call(
        body,
        out_shape=[kshape] * 4 + [wshape],
        compiler_params=pltpu.CompilerParams(
            dimension_semantics=("arbitrary",) * len(call["grid"]), vmem_limit_bytes=VMEM_LIMIT),
        name=name, **call,
    )(*args)
    return list(outs[:4]) + [outs[4].reshape(G, RW_N, L, LANE)]


def _rl_v_ctx_kernel(v_ref, o_ref):
    for nl in range(SUB):
        o_ref[:, nl, :] = _rl_rows(v_ref, nl, False).T


def _rl_v_lat_kernel(f0, f1, b0, b1, o_ref):
    back = _time_flip(jnp.concatenate(
        [_rl_rows(b, nl, True) for b in (b0, b1) for nl in range(SUB)], axis=0))
    for nl in range(SUB):
        parts = []
        for h, f in enumerate((f0, f1)):
            i = h * SUB + nl
            parts += [_rl_rows(f, nl, True), back[i * LAT_ROWS:(i + 1) * LAT_ROWS]]
        o_ref[:, nl, :] = jnp.concatenate(parts, axis=0).T


def _rl_v_call(v, *, lat):
    if lat:
        nvb = RW_NV_LAT // SUB
        half = lambda h: (lambda n: h * nvb + n)
        call = dict(grid=(LAT_BLK_PER_SEQ, nvb),
                    in_specs=[_lat_spec(half(0), False), _lat_spec(half(1), False),
                              _lat_spec(half(0), True), _lat_spec(half(1), True)],
                    out_specs=pl.BlockSpec((RB, SUB, LANE), lambda j, n: (j, n, 0)))
        args = [v.reshape(LAT_VIEW)] * 4
        oshape = jax.ShapeDtypeStruct((DEC_SEQ, RW_NV_LAT, LANE), F32)
        body, name = _rl_v_lat_kernel, "rl_v_lat"
    else:
        call = dict(grid=(RW_N // SUB,),
                    in_specs=[pl.BlockSpec((BATCH, RW_H, SUB, RB), lambda n: (0, 0, n, 0))],
                    out_specs=pl.BlockSpec((RB, SUB, LANE), lambda n: (0, n, 0)))
        args = [v.reshape(N_BLK, RW_H, RW_N, RB)]
        oshape = jax.ShapeDtypeStruct((SEQ, RW_N, LANE), F32)
        body, name = _rl_v_ctx_kernel, "rl_v_ctx"
    return pl.pallas_call(
        body,
        out_shape=oshape,
        compiler_params=pltpu.CompilerParams(
            dimension_semantics=("arbitrary",) * len(call["grid"]), vmem_limit_bytes=VMEM_LIMIT),
        name=name, **call,
    )(*args)


def _rl_out_ctx_kernel(o_ref, ot_ref):
    for nl in range(SUB):
        x = (o_ref[0, :, nl, :] + o_ref[1, :, nl, :]).T
        ot_ref[:, :, nl, :] = x.reshape(BATCH, RW_H, RB)


def _rl_out_lat_kernel(of_ref, ob_ref, ot_ref):
    fwd = [of_ref[0, :, nl, :].T for nl in range(SUB)]
    mir = [ob_ref[0, :, nl, :].T for nl in range(SUB)]
    lo = lambda h, d: (2 * h + d) * LAT_ROWS
    back = _time_flip(jnp.concatenate(
        [mir[nl][lo(h, 1):lo(h, 1) + LAT_ROWS] for nl in range(SUB) for h in range(RW_VH)], axis=0))
    for nl in range(SUB):
        for h in range(RW_VH):
            i = nl * RW_VH + h
            x = fwd[nl][lo(h, 0):lo(h, 0) + LAT_ROWS] + back[i * LAT_ROWS:(i + 1) * LAT_ROWS]
            ot_ref[:, 0, :, h, nl, :] = x.reshape(DEC_BATCH, RW_H, RB)


def _rl_out_call(o, *, lat):
    if lat:
        nvb = RW_NV_LAT // SUB
        call = dict(grid=(LAT_BLK_PER_SEQ, nvb),
                    in_specs=[pl.BlockSpec((1, RB, SUB, LANE), lambda j, n: (0, j, n, 0)),
                              pl.BlockSpec((1, RB, SUB, LANE),
                                           lambda j, n: (0, LAT_BLK_PER_SEQ - 1 - j, n, 0))],
                    out_specs=pl.BlockSpec((DEC_BATCH, 1, RW_H, RW_VH, SUB, RB),
                                           lambda j, n: (0, j, 0, 0, n, 0)))
        oshape = (DEC_BATCH, LAT_BLK_PER_SEQ, RW_H, RW_VH, RW_NV_LAT, RB)
        args, body, name = [o, o], _rl_out_lat_kernel, "rl_out_lat"
    else:
        call = dict(grid=(RW_N // SUB,),
                    in_specs=[pl.BlockSpec((2, RB, SUB, LANE), lambda n: (0, 0, n, 0))],
                    out_specs=pl.BlockSpec((BATCH, RW_H, SUB, RB), lambda n: (0, 0, n, 0)))
        oshape = (BATCH, RW_H, RW_N, RB)
        args, body, name = [o], _rl_out_ctx_kernel, "rl_out_ctx"
    out = pl.pallas_call(
        body,
        out_shape=jax.ShapeDtypeStruct(oshape, F32),
        compiler_params=pltpu.CompilerParams(
            dimension_semantics=("arbitrary",) * len(call["grid"]), vmem_limit_bytes=VMEM_LIMIT),
        name=name, **call,
    )(*args)
    return out.reshape(N_CTX_BLK, D_MIX, RB)


def _rw_scan_kernel(*refs, vb, npart, has_init, want_final):
    it = iter(refs)
    r_ref, w_ref, k_ref, v_ref, kk_ref, nkka_ref = (next(it), next(it), next(it), next(it),
                                                    next(it), next(it))
    s0_ref = next(it) if has_init else None
    o_ref = next(it)
    sfin_ref = next(it) if want_final else None
    s_scr = next(it)
    g = pl.program_id(0)
    tb = pl.program_id(1)

    @pl.when(tb == 0)
    def _():
        if has_init:
            s_scr[...] = s0_ref[0]
        else:
            s_scr[...] = jnp.zeros(s_scr.shape, F32)

    def bcast(ref, t, kx):
        return jnp.broadcast_to(ref[kx, pl.ds(t, 1), :], (SUB, LANE))

    def t_of(i):
        return jnp.where(g == 0, i, SCAN_TB - 1 - i)

    def zeros():
        return [[jnp.zeros((SUB, LANE), F32) for _ in range(vb)] for _ in range(npart)]

    def total(parts):
        return [functools.reduce(lambda a, b: a + b, [p[j] for p in parts]) for j in range(vb)]

    t0 = t_of(0)
    acc = zeros()
    for kx in range(RW_N):
        kkb = bcast(kk_ref, t0, kx)
        for j in range(vb):
            acc[kx % npart][j] = acc[kx % npart][j] + s_scr[kx, j * SUB:(j + 1) * SUB, :] * kkb

    def step(i, skk):
        t = t_of(i)
        tn = t_of(jnp.minimum(i + 1, SCAN_TB - 1))
        vv = [v_ref[t, j * SUB:(j + 1) * SUB, :] for j in range(vb)]
        oacc, nacc = zeros(), zeros()
        for kx in range(RW_N):
            wb = jnp.broadcast_to(w_ref[0, kx, pl.ds(t, 1), :], (SUB, LANE))
            ab = bcast(nkka_ref, t, kx)
            kb = bcast(k_ref, t, kx)
            rb = bcast(r_ref, t, kx)
            kkn = bcast(kk_ref, tn, kx)
            p = kx % npart
            for j in range(vb):
                s = s_scr[kx, j * SUB:(j + 1) * SUB, :] * wb + skk[j] * ab + vv[j] * kb
                s_scr[kx, j * SUB:(j + 1) * SUB, :] = s
                oacc[p][j] = oacc[p][j] + s * rb
                nacc[p][j] = nacc[p][j] + s * kkn
        for j, o in enumerate(total(oacc)):
            o_ref[0, t, j * SUB:(j + 1) * SUB, :] = o
        return tuple(total(nacc))

    lax.fori_loop(0, SCAN_TB, step, tuple(total(acc)))

    if want_final:
        @pl.when(tb == pl.num_programs(1) - 1)
        def _():
            sfin_ref[0] = s_scr[...]


def _rw_scan_call(r, w, k, v, kk, nkka, s0, *, want_final):
    G, L = w.shape[0], r.shape[1]
    nv = v.shape[1]
    ntb = L // SCAN_TB
    has_init = s0 is not None
    tmap = lambda g, t: t + g * (ntb - 1 - 2 * t)
    kspec = pl.BlockSpec((RW_N, SCAN_TB, LANE), lambda g, t: (0, tmap(g, t), 0))
    wspec = pl.BlockSpec((1, RW_N, SCAN_TB, LANE), lambda g, t: (g, 0, tmap(g, t), 0))
    vspec = pl.BlockSpec((SCAN_TB, nv, LANE), lambda g, t: (tmap(g, t), 0, 0))
    ospec = pl.BlockSpec((1, SCAN_TB, nv, LANE), lambda g, t: (g, tmap(g, t), 0, 0))
    sspec = pl.BlockSpec((1, RW_N, nv, LANE), lambda g, t: (g, 0, 0, 0))
    in_specs = [kspec, wspec, kspec, vspec, kspec, kspec]
    args = [r, w, k, v, kk, nkka]
    if has_init:
        in_specs.append(sspec)
        args.append(s0)
    out_specs = [ospec]
    out_shape = [jax.ShapeDtypeStruct((G, L, nv, LANE), F32)]
    if want_final:
        out_specs.append(sspec)
        out_shape.append(jax.ShapeDtypeStruct((G, RW_N, nv, LANE), F32))
    return pl.pallas_call(
        functools.partial(_rw_scan_kernel, vb=nv // SUB, npart=1,
                          has_init=has_init, want_final=want_final),
        grid=(G, ntb),
        in_specs=in_specs,
        out_specs=out_specs,
        out_shape=out_shape,
        scratch_shapes=[pltpu.VMEM((RW_N, nv, LANE), F32)],
        compiler_params=pltpu.CompilerParams(dimension_semantics=("arbitrary", "arbitrary"),
                                             vmem_limit_bytes=VMEM_LIMIT),
        name="rw_scan",
    )(*args)


def _merge_kernel(x_ref, mod_ref, ng_ref, wgate_ref, yssd_c, yssd_l, ygla_c, ygla_l, orw_c, orw_l,
                  bonus_ref, g_ref, lnw_ref, lnb_ref, bd_ref, wso_ref, wgo_ref, wro_ref, wout_ref,
                  o_ref):
    is_ctx = pl.program_id(0) < N_CTX_BLK
    x = x_ref[...]
    u = _norm_mod(x, ng_ref[...], mod_ref[0, 3:4, :], mod_ref[0, 4:5, :]).astype(BF16)
    bd = bd_ref[...]
    yssd = jnp.where(is_ctx, yssd_c[...], yssd_l[...])
    ygla = jnp.where(is_ctx, ygla_c[...], ygla_l[...])
    o = jnp.where(is_ctx, orw_c[0], orw_l[0]).T
    mu = _dot_sel(o, bd, terms=2) * (1.0 / RW_N)
    oc = o - mu
    var = _dot_sel(oc * oc, bd, terms=2) * (1.0 / RW_N)
    o = oc * lax.rsqrt(var + RW_GN_EPS) * lnw_ref[...] + lnb_ref[...]
    y_rw = ((o + bonus_ref[...]) * g_ref[...]).astype(BF16)
    merged = jnp.zeros((RB, D_MODEL), F32)
    for b, (y, wo_ref) in enumerate(((yssd, wso_ref), (ygla, wgo_ref), (y_rw, wro_ref))):
        gate = _sigmoid(jnp.dot(u, wgate_ref[:, b * D_MODEL:(b + 1) * D_MODEL],
                                preferred_element_type=F32))
        merged = merged + gate * jnp.dot(y, wo_ref[0], preferred_element_type=F32)
    m = jnp.dot(merged.astype(BF16), wout_ref[0], preferred_element_type=F32)
    o_ref[...] = x + mod_ref[0, 5:6, :] * m


def _merge_call(x, mod, ng, wgate, yssd_c, yssd_l, ygla_c, ygla_l, orw_c, orw_l, bonus, g, lnw, lnb,
                bd, wso, wgo, wro, wout, l):
    row_spec = pl.BlockSpec((RB, D_MODEL), lambda i: (i, 0))
    mix_spec = pl.BlockSpec((RB, D_MIX), lambda i: (i, 0))
    mix_c = pl.BlockSpec((RB, D_MIX), lambda i: (_ctx_idx(i), 0))
    mix_l = pl.BlockSpec((RB, D_MIX), lambda i: (_lat_idx(i), 0))
    t_c = pl.BlockSpec((1, D_MIX, RB), lambda i: (_ctx_idx(i), 0, 0))
    t_l = pl.BlockSpec((1, D_MIX, RB), lambda i: (_lat_idx(i), 0, 0))
    in_specs = [row_spec, pl.BlockSpec((1, N_MOD, D_MODEL), lambda i: (i, 0, 0)),
                _const_spec((1, D_MODEL)), _const_spec((D_MODEL, 3 * D_MODEL)),
                mix_c, mix_l, mix_c, mix_l, t_c, t_l, mix_spec, mix_spec,
                _const_spec((1, D_MIX)), _const_spec((1, D_MIX)), _const_spec((D_MIX, D_MIX)),
                _stacked_spec((D_MIX, D_MODEL), (l,)), _stacked_spec((D_MIX, D_MODEL), (l,)),
                _stacked_spec((D_MIX, D_MODEL), (l,)), _stacked_spec((D_MODEL, D_MODEL), (l,))]
    return pl.pallas_call(
        _merge_kernel,
        grid=(N_BLK,),
        in_specs=in_specs,
        out_specs=row_spec,
        out_shape=jax.ShapeDtypeStruct((N_TOK, D_MODEL), F32),
        compiler_params=pltpu.CompilerParams(dimension_semantics=("arbitrary",),
                                             vmem_limit_bytes=VMEM_LIMIT),
        name="merge",
    )(x, mod, ng, wgate, yssd_c, yssd_l, ygla_c, ygla_l, orw_c, orw_l, bonus, g, lnw, lnb, bd,
      wso, wgo, wro, wout)


def _grid_pos_embed(rows, cols, dim):
    quarter = dim // 4
    omega = 1.0 / (10000.0 ** (jnp.arange(quarter, dtype=F32) / quarter))
    er = jnp.arange(rows, dtype=F32)[:, None] * omega
    ec = jnp.arange(cols, dtype=F32)[:, None] * omega
    er = jnp.concatenate([jnp.sin(er), jnp.cos(er)], axis=-1)
    ec = jnp.concatenate([jnp.sin(ec), jnp.cos(ec)], axis=-1)
    emb = jnp.concatenate([jnp.broadcast_to(er[:, None], (rows, cols, dim // 2)),
                           jnp.broadcast_to(ec[None], (rows, cols, dim // 2))], axis=-1)
    return emb.reshape(rows * cols, dim)


def _pad_cols(a, n):
    return jnp.pad(a, [(0, 0)] * (a.ndim - 1) + [(0, n - a.shape[-1])])


def _rows_at(a, off, n):
    return jnp.pad(a, ((off, n - off - a.shape[0]), (0, 0)))


def _block_diag_ones(n, blk):
    i = np.arange(n)
    return jnp.asarray((i[:, None] // blk) == (i[None, :] // blk), BF16)


def _expand01(row0):
    m = np.zeros((LANE, D_MIX), np.float32)
    for h in range(SSD_H):
        m[row0 + h, h * SSD_P:(h + 1) * SSD_P] = 1.0
    return jnp.asarray(m, BF16)


def kernel(x_prompt, x_sample, state_ssd, state_gla, state_rwkv, c, c_ctx, norm_g, w_ada, b_ada,
           ffn_gate, ffn_up, ffn_down, w_in, ssd_conv_w, ssd_conv_b, ssd_dt_bias, ssd_A_log, ssd_D,
           ssd_norm, w_ssd_o, gla_gk_w, gla_gk_b, gla_norm, w_gla_o, rw_mu, rw_w0, rw_w2, rw_a0,
           rw_a2, rw_g2, rw_kk, rw_ka, rw_rk, rw_ln_w, rw_ln_b, w_rw_o, w_out, final_norm):
    pos = _grid_pos_embed(DEC_SEQ // GRID_W, GRID_W, D_MODEL)
    x = (x_prompt.reshape(N_CTX_TOK, D_MODEL), x_sample.reshape(-1, D_MODEL))
    s0_rw = state_rwkv.reshape(DEC_BATCH, DEPTH, 2, RW_H, RW_VH, RW_NV_LAT, RW_N).transpose(
        1, 6, 5, 4, 2, 0, 3).reshape(DEPTH, 1, RW_N, RW_NV_LAT, LANE)

    cond8 = jnp.concatenate([c_ctx[None], c, jnp.zeros((SUB - 1 - DEC_BATCH, D_MODEL), F32)])
    ada = _ada_call(cond8, w_ada, b_ada)
    cond_of_blk = np.concatenate([np.zeros(N_CTX_BLK, np.int32),
                                  1 + np.arange(N_BLK - N_CTX_BLK, dtype=np.int32) // LAT_BLK_PER_SEQ])

    bd = _block_diag_ones(D_MIX, RW_N)
    ef, eb = _expand01(0), _expand01(SSD_H)
    o_ssd = D_MIX + SSD_XBC + 2 * SSD_H
    o_gla = o_ssd + 2 * GLA_H * GLA_DK + 2 * D_MIX + 2 * GLA_LR
    o_rw = o_gla + 3 * D_MIX + 2 * RW_LW + RW_LA + RW_LG

    ffn_gate_b, ffn_up_b, ffn_down_b = (a.astype(BF16) for a in (ffn_gate, ffn_up, ffn_down))
    w_ssd_o_b, w_gla_o_b, w_rw_o_b, w_out_b = (a.astype(BF16)
                                               for a in (w_ssd_o, w_gla_o, w_rw_o, w_out))

    new_ssd, new_gla, new_rw = [], [], []
    for l in range(DEPTH):
        mod = ada[l][cond_of_blk].reshape(N_BLK, N_MOD, D_MODEL)
        ng = norm_g[l]
        wi = w_in[l]
        w_ssd = _pad_cols(wi[:, :o_ssd], SSD_W).astype(BF16)
        w_gla = _pad_cols(wi[:, o_ssd:o_gla], GLA_W).astype(BF16)
        wr_ = wi[:, o_gla:o_rw]
        w_rw = jnp.concatenate([wr_[:, :1664], wr_[:, 1728:1856], _pad_cols(wr_[:, 1664:1728], LANE)],
                               axis=1).astype(BF16)
        mu_ = rw_mu[l]
        mu = jnp.concatenate([mu_[:1664], mu_[1728:1856], _pad_cols(mu_[1664:1728], LANE)])[None]
        w_gate = wi[:, o_rw:].astype(BF16)

        x = _ffn_call(x, mod, ng[0:1], ffn_gate_b, ffn_up_b, ffn_down_b, (l, 0), mod_row=0,
                      pos=pos if l == 0 else None)

        ssd_args = (x, mod, ng[1:2], w_ssd, ssd_conv_w[l], ssd_conv_b[l][None],
                    _pad_cols(ssd_dt_bias[l].reshape(1, -1), LANE),
                    _pad_cols(ssd_A_log[l].reshape(1, -1), LANE),
                    jnp.repeat(ssd_D[l], SSD_P, axis=1), ssd_norm[l][None], ef, eb)
        y_ssd_c, s_ssd = _ssd_call(*ssd_args, None, ctx=True)
        (y_ssd_l,) = _ssd_call(*ssd_args, state_ssd[:, l], ctx=False)
        new_ssd.append(s_ssd)

        gkw = jnp.stack([_rows_at(gla_gk_w[l, d], d * GLA_LR, LANE) for d in range(2)])
        gla_args = (x, mod, ng[1:2], w_gla, gkw.astype(BF16), gla_gk_b[l],
                    gla_norm[l][None])
        y_gla_c, s_gla = _gla_call(*gla_args, None, ctx=True)
        (y_gla_l,) = _gla_call(*gla_args, state_gla[:, l], ctx=False)
        new_gla.append(s_gla)

        w2p = jnp.stack([_rows_at(rw_w2[l, d], d * RW_LW, LANE) for d in range(2)]).astype(BF16)
        r, wf, wb, k, v, kk, nkka, g, bonus = _rw_prep_call(
            x, mod, ng[1:2], w_rw, mu, rw_a0[l][None], _rows_at(rw_a2[l], 0, LANE).astype(BF16),
            rw_g2[l].astype(BF16), rw_kk[l][None], rw_ka[l][None], rw_rk[l].reshape(1, D_MIX),
            rw_w0[l], w2p, bd)
        rc, kc, kkc, ac, wc = _rl_k_call(r, k, kk, nkka, wf, wb, lat=False)
        o_c, s_rw = _rw_scan_call(rc, wc, kc, _rl_v_call(v, lat=False), kkc, ac, None,
                                  want_final=True)
        new_rw.append(s_rw.reshape(2, RW_N, RW_N, BATCH, RW_H).transpose(3, 0, 4, 2, 1))
        rl, kl, kkl, al, wl = _rl_k_call(r, k, kk, nkka, wf, wb, lat=True)
        (o_l,) = _rw_scan_call(rl, wl, kl, _rl_v_call(v, lat=True), kkl, al, s0_rw[l],
                               want_final=False)

        x = _merge_call(x, mod, ng[1:2], w_gate, y_ssd_c, y_ssd_l, y_gla_c, y_gla_l,
                        _rl_out_call(o_c, lat=False), _rl_out_call(o_l, lat=True), bonus, g,
                        rw_ln_w[l][None], rw_ln_b[l][None], bd, w_ssd_o_b, w_gla_o_b, w_rw_o_b,
                        w_out_b, l)

        x = _ffn_call(x, mod, ng[2:3], ffn_gate_b, ffn_up_b, ffn_down_b, (l, 1), mod_row=6,
                      final_g=final_norm[None] if l == DEPTH - 1 else None)

    y_prompt = x[0].reshape(BATCH, SEQ, D_MODEL)
    y_sample = x[1].reshape(DEC_BATCH, DEC_SEQ, D_MODEL)
    return (y_prompt, y_sample, jnp.stack(new_ssd, axis=1), jnp.stack(new_gla, axis=1),
            jnp.stack(new_rw, axis=1))
```

```python
import functools
import math

import numpy as np
import jax
import jax.numpy as jnp
from jax import lax
from jax.experimental import pallas as pl
from jax.experimental.pallas import tpu as pltpu

F32 = jnp.float32
BF16 = jnp.bfloat16

D_MODEL = 1024
BATCH = 16
SEQ = 256
DEPTH = 2
DEC_BATCH = 4
DEC_SEQ = 1024
GRID_W = 64
D_MIX = 512
D_FF = 2816
N_MOD = 9
SSD_P = 64
SSD_H = 8
SSD_N = 64
SSD_G = 2
SSD_XBC = 768
GLA_H = 4
GLA_DK = 64
GLA_DV = 128
GLA_LR = 16
GLA_GATE_NORM = 16.0
GLA_CHUNK = 64
RW_N = 64
RW_H = 8
RW_LW = 64
RW_LA = 64
RW_LG = 128
RMS_EPS = 1e-6
RW_GN_EPS = 64e-5

LANE = 128
SUB = 8
RB = 256
N_CTX_TOK = BATCH * SEQ
N_TOK = N_CTX_TOK + DEC_BATCH * DEC_SEQ
N_BLK = N_TOK // RB
N_CTX_BLK = N_CTX_TOK // RB
LAT_BLK_PER_SEQ = DEC_SEQ // RB
SSD_W = D_MIX + SSD_XBC + LANE
GLA_QK = GLA_H * GLA_DK
GLA_W = 2 * GLA_QK + 2 * D_MIX + LANE
RW_W = 3 * D_MIX + 3 * LANE
SCAN_TB = 32
FFN_SUB = 1
VMEM_LIMIT = 56 * 1024 * 1024


def _dot(a, b):
    return jnp.dot(a.astype(BF16), b.astype(BF16), preferred_element_type=F32)


def _dot_nt(a, b):
    return lax.dot_general(a.astype(BF16), b.astype(BF16), (((1,), (1,)), ((), ())),
                           preferred_element_type=F32)


def _split3(x):
    hi = x.astype(BF16)
    r1 = x - hi.astype(F32)
    mid = r1.astype(BF16)
    lo = (r1 - mid.astype(F32)).astype(BF16)
    return hi, mid, lo


def _sel_dot(m01, x):
    hi, mid, lo = _split3(x)
    f = lambda p: jnp.dot(m01, p, preferred_element_type=F32)
    return f(hi) + f(mid) + f(lo)


def _dot_sel(x, m01, terms=3):
    f = lambda p: jnp.dot(p, m01, preferred_element_type=F32)
    return functools.reduce(lambda a, b: a + b, [f(p) for p in _split3(x)[:terms]])


def _sigmoid(x):
    return 0.5 * jnp.tanh(0.5 * x) + 0.5


def _silu(x):
    return x * _sigmoid(x)


def _softplus(x):
    return jnp.maximum(x, 0.0) + jnp.log(1.0 + jnp.exp(-jnp.abs(x)))


def _rmsnorm(x, g):
    return x * lax.rsqrt(jnp.mean(x * x, axis=-1, keepdims=True) + RMS_EPS) * g


def _norm_mod(x, g, shift, scale):
    return _rmsnorm(x, g) * (1.0 + scale) + shift


def _iota(shape, dim):
    return lax.broadcasted_iota(jnp.int32, shape, dim)


def _tri01(n, lower, chunk=None):
    t = _iota((n, n), 0)
    s = _iota((n, n), 1)
    m = (s <= t) if lower else (s >= t)
    if chunk is not None:
        m = m & ((t // chunk) == (s // chunk))
    return m


def _ada_kernel(c_ref, w_ref, b_ref, o_ref):
    o_ref[0] = _dot(_silu(c_ref[...]), w_ref[0]) + b_ref[0]


def _ada_call(cond8, w_ada, b_ada):
    tn = 2304
    nj = (N_MOD * D_MODEL) // tn
    return pl.pallas_call(
        _ada_kernel,
        grid=(DEPTH, nj),
        in_specs=[pl.BlockSpec((SUB, D_MODEL), lambda l, j: (0, 0)),
                  pl.BlockSpec((1, D_MODEL, tn), lambda l, j: (l, 0, j)),
                  pl.BlockSpec((1, 1, tn), lambda l, j: (l, 0, j))],
        out_specs=pl.BlockSpec((1, SUB, tn), lambda l, j: (l, 0, j)),
        out_shape=jax.ShapeDtypeStruct((DEPTH, SUB, N_MOD * D_MODEL), F32),
        compiler_params=pltpu.CompilerParams(dimension_semantics=("arbitrary", "arbitrary"),
                                             vmem_limit_bytes=VMEM_LIMIT),
        name="ada",
    )(cond8, w_ada, b_ada.reshape(DEPTH, 1, N_MOD * D_MODEL))


def _ffn_kernel(*refs, mod_row, first, last):
    it = iter(refs)
    is_ctx = pl.program_id(0) < N_CTX_BLK // FFN_SUB
    if first:
        xc_ref, xl_ref, pos_ref = next(it), next(it), next(it)
        x = jnp.where(is_ctx, xc_ref[...], xl_ref[...] + pos_ref[...])
    else:
        x = next(it)[...]
    mod_ref, ng_ref, wg_ref, wu_ref, wd_ref = next(it), next(it), next(it), next(it), next(it)
    fin_ref = next(it) if last else None
    shift = mod_ref[0, mod_row:mod_row + 1, :]
    scale = mod_ref[0, mod_row + 1:mod_row + 2, :]
    gate = mod_ref[0, mod_row + 2:mod_row + 3, :]
    h = _norm_mod(x, ng_ref[...], shift, scale).astype(BF16)
    a = _silu(jnp.dot(h, wg_ref[0, 0], preferred_element_type=F32))
    a = (a * jnp.dot(h, wu_ref[0, 0], preferred_element_type=F32)).astype(BF16)
    y = x + 0.5 * gate * jnp.dot(a, wd_ref[0, 0], preferred_element_type=F32)
    if last:
        y = _rmsnorm(y, fin_ref[...])
        oc_ref, ol_ref = next(it), next(it)

        @pl.when(is_ctx)
        def _():
            oc_ref[...] = y

        @pl.when(jnp.logical_not(is_ctx))
        def _():
            ol_ref[...] = y
    else:
        next(it)[...] = y


def _const_spec(shape):
    nd = len(shape)
    return pl.BlockSpec(shape, lambda i: (0,) * nd, pipeline_mode=pl.Buffered(1))


def _stacked_spec(shape, lead):
    nd = len(shape)
    return pl.BlockSpec((1,) * len(lead) + shape, lambda i: tuple(lead) + (0,) * nd,
                        pipeline_mode=pl.Buffered(1))


def _ctx_idx(i):
    return jnp.minimum(i, N_CTX_BLK - 1)


def _lat_idx(i):
    return jnp.maximum(i - N_CTX_BLK, 0)


def _ffn_call(x, mod, ng, wg, wu, wd, lj, *, mod_row, pos=None, final_g=None):
    first = pos is not None
    last = final_g is not None
    fb = FFN_SUB * RB
    nctx = N_CTX_BLK // FFN_SUB
    row_spec = pl.BlockSpec((fb, D_MODEL), lambda i: (i, 0))
    ctx_spec = pl.BlockSpec((fb, D_MODEL), lambda i: (jnp.minimum(i, nctx - 1), 0))
    lat_spec = pl.BlockSpec((fb, D_MODEL), lambda i: (jnp.maximum(i - nctx, 0), 0))
    if first:
        in_specs = [ctx_spec, lat_spec,
                    pl.BlockSpec((fb, D_MODEL),
                                 lambda i: (jnp.maximum(i - nctx, 0) % (DEC_SEQ // fb), 0))]
        args = [x[0], x[1], pos]
    else:
        in_specs = [row_spec]
        args = [x]
    in_specs += [pl.BlockSpec((1, N_MOD, D_MODEL), lambda i: (FFN_SUB * i, 0, 0)),
                 _const_spec((1, D_MODEL)), _stacked_spec((D_MODEL, D_FF), lj),
                 _stacked_spec((D_MODEL, D_FF), lj), _stacked_spec((D_FF, D_MODEL), lj)]
    args += [mod, ng, wg, wu, wd]
    if last:
        in_specs.append(_const_spec((1, D_MODEL)))
        args.append(final_g)
        half = jax.ShapeDtypeStruct((N_TOK // 2, D_MODEL), F32)
        out_specs, out_shape = [ctx_spec, lat_spec], [half, half]
    else:
        out_specs, out_shape = row_spec, jax.ShapeDtypeStruct((N_TOK, D_MODEL), F32)
    return pl.pallas_call(
        functools.partial(_ffn_kernel, mod_row=mod_row, first=first, last=last),
        grid=(N_BLK // FFN_SUB,),
        in_specs=in_specs,
        out_specs=out_specs,
        out_shape=out_shape,
        compiler_params=pltpu.CompilerParams(dimension_semantics=("arbitrary",),
                                             vmem_limit_bytes=VMEM_LIMIT),
        name="ffn",
    )(*args)


def _ssd_kernel(*refs, L, has_init, want_final):
    it = iter(refs)
    x_ref, mod_ref, ng_ref, w_ref = next(it), next(it), next(it), next(it)
    cw_ref, cb_ref, dtb_ref, alog_ref, d_ref, nw_ref = (next(it), next(it), next(it), next(it),
                                                        next(it), next(it))
    ef_ref, eb_ref = next(it), next(it)
    s0_ref = next(it) if has_init else None
    y_ref = next(it)
    sfin_ref = next(it) if want_final else None
    yacc = next(it)

    C = RB
    nc = L // C
    x = x_ref[...]
    u = _norm_mod(x, ng_ref[...], mod_ref[0, 3:4, :], mod_ref[0, 4:5, :]).astype(BF16)
    p = jnp.dot(u, w_ref[...], preferred_element_type=F32)
    z = p[:, :D_MIX]
    xbc = p[:, D_MIX:D_MIX + SSD_XBC]
    dtp = p[:, D_MIX + SSD_XBC:]
    rows = _iota((L, 1), 0)
    prev = jnp.where(rows == 0, 0.0, pltpu.roll(xbc, 1, 0))
    nxt = jnp.where(rows == L - 1, 0.0, pltpu.roll(xbc, L - 1, 0))
    xc = cb_ref[...] + prev * cw_ref[0:1, :] + xbc * cw_ref[1:2, :] + nxt * cw_ref[2:3, :]
    xc = _silu(xc)
    xs = xc[:, :D_MIX]
    bm = xc[:, D_MIX:D_MIX + SSD_G * SSD_N]
    cm = xc[:, D_MIX + SSD_G * SSD_N:]
    dt = _softplus(dtp + dtb_ref[...])
    adt = dt * (-jnp.exp(alog_ref[...]))

    tril = _tri01(C, True)
    triu = _tri01(C, False)
    tril_b = jnp.where(tril, 1.0, 0.0).astype(BF16)
    triu_b = jnp.where(triu, 1.0, 0.0).astype(BF16)
    ef = ef_ref[...]
    eb = eb_ref[...]

    cs, csT, rcs, rcsT, dtT, loc_f, loc_b = [], [], [], [], [], [], []
    for c in range(nc):
        a_c = adt[c * C:(c + 1) * C]
        a_cT = a_c.T
        cs.append(_sel_dot(tril_b, a_c))
        rcs.append(_sel_dot(triu_b, a_c))
        csT.append(_dot_sel(a_cT, triu_b))
        rcsT.append(_dot_sel(a_cT, tril_b))
        dtT.append(dt[c * C:(c + 1) * C].T)

    need_states = want_final or nc > 1
    if need_states:
        for c in range(nc):
            xs_c = xs[c * C:(c + 1) * C]
            dt_c = dt[c * C:(c + 1) * C]
            wf = jnp.exp(cs[c][C - 1:C, :] - cs[c]) * dt_c
            wb = jnp.exp(rcs[c][0:1, :] - rcs[c]) * dt_c
            xwf = (xs_c * _dot_sel(wf, ef, terms=2)).T
            xwb = (xs_c * _dot_sel(wb, eb, terms=2)).T
            lf, lb = [], []
            for h in range(SSD_H):
                g = h // (SSD_H // SSD_G)
                bm_g = bm[c * C:(c + 1) * C, g * SSD_N:(g + 1) * SSD_N]
                lf.append(_dot(xwf[h * SSD_P:(h + 1) * SSD_P, :], bm_g))
                lb.append(_dot(xwb[h * SSD_P:(h + 1) * SSD_P, :], bm_g))
            loc_f.append(lf)
            loc_b.append(lb)

    zero_s = jnp.zeros((SSD_P, SSD_N), F32)
    sin_f = [[None] * SSD_H for _ in range(nc + 1)]
    sin_b = [[None] * SSD_H for _ in range(nc + 1)]
    for h in range(SSD_H):
        sin_f[0][h] = s0_ref[0, 0, h] if has_init else zero_s
        sin_b[nc][h] = s0_ref[0, 1, h] if has_init else zero_s
    if need_states:
        for c in range(nc):
            dec = jnp.exp(cs[c][C - 1:C, :])
            for h in range(SSD_H):
                sin_f[c + 1][h] = dec[:, h:h + 1] * sin_f[c][h] + loc_f[c][h]
        for c in range(nc - 1, -1, -1):
            dec = jnp.exp(rcs[c][0:1, :])
            for h in range(SSD_H):
                sin_b[c][h] = dec[:, SSD_H + h:SSD_H + h + 1] * sin_b[c + 1][h] + loc_b[c][h]

    for c in range(nc):
        sl = slice(c * C, (c + 1) * C)
        ecs = jnp.exp(cs[c])
        ercs = jnp.exp(rcs[c])
        for h in range(SSD_H):
            g = h // (SSD_H // SSD_G)
            cm_g = cm[sl, g * SSD_N:(g + 1) * SSD_N]
            bm_g = bm[sl, g * SSD_N:(g + 1) * SSD_N]
            cb = _dot_nt(cm_g, bm_g)
            lf = jnp.exp(jnp.where(tril, cs[c][:, h:h + 1] - csT[c][h:h + 1, :], -jnp.inf))
            lb = jnp.exp(jnp.where(triu, rcs[c][:, SSD_H + h:SSD_H + h + 1]
                                   - rcsT[c][SSD_H + h:SSD_H + h + 1, :], -jnp.inf))
            m = cb * (lf * dtT[c][h:h + 1, :] + lb * dtT[c][SSD_H + h:SSD_H + h + 1, :])
            y_h = _dot(m, xs[sl, h * SSD_P:(h + 1) * SSD_P])
            if has_init or nc > 1:
                y_h = y_h + ecs[:, h:h + 1] * _dot_nt(cm_g, sin_f[c][h])
                y_h = y_h + ercs[:, SSD_H + h:SSD_H + h + 1] * _dot_nt(cm_g, sin_b[c + 1][h])
            yacc[sl, h * SSD_P:(h + 1) * SSD_P] = y_h

    y = yacc[...] + xs * (d_ref[0:1, :] + d_ref[1:2, :])
    y = _rmsnorm(y * _silu(z), nw_ref[...])
    y_ref[...] = y.astype(BF16)
    if want_final:
        for h in range(SSD_H):
            sfin_ref[0, 0, h] = sin_f[nc][h]
            sfin_ref[0, 1, h] = sin_b[0][h]


def _ssd_call(x, mod, ng, w, cw, cb, dtb, alog, dexp, nw, ef, eb, s0, *, ctx):
    L = SEQ if ctx else DEC_SEQ
    nseq = BATCH if ctx else DEC_BATCH
    blk0 = 0 if ctx else N_CTX_TOK // L
    mod_of = (lambda i: (i, 0, 0)) if ctx else (lambda i: (N_CTX_BLK + LAT_BLK_PER_SEQ * i, 0, 0))
    in_specs = [pl.BlockSpec((L, D_MODEL), lambda i: (blk0 + i, 0)),
                pl.BlockSpec((1, N_MOD, D_MODEL), mod_of),
                _const_spec((1, D_MODEL)), _const_spec((D_MODEL, SSD_W)),
                _const_spec((3, SSD_XBC)), _const_spec((1, SSD_XBC)),
                _const_spec((1, LANE)), _const_spec((1, LANE)),
                _const_spec((2, D_MIX)), _const_spec((1, D_MIX)),
                _const_spec((LANE, D_MIX)), _const_spec((LANE, D_MIX))]
    args = [x, mod, ng, w, cw, cb, dtb, alog, dexp, nw, ef, eb]
    st_spec = pl.BlockSpec((1, 2, SSD_H, SSD_P, SSD_N), lambda i: (i, 0, 0, 0, 0))
    out_specs = [pl.BlockSpec((L, D_MIX), lambda i: (i, 0))]
    out_shape = [jax.ShapeDtypeStruct((nseq * L, D_MIX), BF16)]
    if ctx:
        out_specs.append(st_spec)
        out_shape.append(jax.ShapeDtypeStruct((nseq, 2, SSD_H, SSD_P, SSD_N), F32))
    else:
        in_specs.append(st_spec)
        args.append(s0)
    return pl.pallas_call(
        functools.partial(_ssd_kernel, L=L, has_init=not ctx, want_final=ctx),
        grid=(nseq,),
        in_specs=in_specs,
        out_specs=out_specs,
        out_shape=out_shape,
        scratch_shapes=[pltpu.VMEM((L, D_MIX), F32)],
        compiler_params=pltpu.CompilerParams(dimension_semantics=("arbitrary",),
                                             vmem_limit_bytes=VMEM_LIMIT),
        name="ssd_ctx" if ctx else "ssd_lat",
    )(*args)


def _gla_kernel(*refs, L, has_init, want_final):
    it = iter(refs)
    x_ref, mod_ref, ng_ref, w_ref = next(it), next(it), next(it), next(it)
    gkw_ref, gkb_ref, nw_ref = next(it), next(it), next(it)
    s0_ref = next(it) if has_init else None
    y_ref = next(it)
    sfin_ref = next(it) if want_final else None
    oacc = next(it)

    C = GLA_CHUNK
    B = RB
    cpb = B // C
    nb = L // B
    x = x_ref[...]
    u = _norm_mod(x, ng_ref[...], mod_ref[0, 3:4, :], mod_ref[0, 4:5, :]).astype(BF16)
    p = jnp.dot(u, w_ref[...], preferred_element_type=F32)
    q = p[:, :GLA_QK] * (GLA_DK ** -0.5)
    k = p[:, GLA_QK:2 * GLA_QK]
    v = p[:, 2 * GLA_QK:2 * GLA_QK + D_MIX]
    gg = p[:, 2 * GLA_QK + D_MIX:2 * GLA_QK + 2 * D_MIX]
    lr = p[:, 2 * GLA_QK + 2 * D_MIX:].astype(BF16)
    lg = []
    for d in range(2):
        pre = jnp.dot(lr, gkw_ref[d], preferred_element_type=F32) + gkb_ref[d:d + 1, :]
        lg.append(-_softplus(-pre) / GLA_GATE_NORM)

    low = _tri01(B, True, C)
    upp = _tri01(B, False, C)
    low_b = jnp.where(low, 1.0, 0.0).astype(BF16)
    upp_b = jnp.where(upp, 1.0, 0.0).astype(BF16)
    rowi = _iota((B, 1), 0)

    if has_init:
        s_f = [s0_ref[0, 0, h] for h in range(GLA_H)]
        s_b = [s0_ref[0, 1, h] for h in range(GLA_H)]
    else:
        s_f = [jnp.zeros((GLA_DK, GLA_DV), F32)] * GLA_H
        s_b = list(s_f)

    hpp = LANE // GLA_DK
    lane_head = _iota((1, LANE), 1) // GLA_DK

    def pair(h):
        return slice((h // hpp) * LANE, (h // hpp + 1) * LANE)

    def only(h, a):
        return jnp.where(lane_head == h % hpp, a[:, pair(h)], 0.0)

    def pair_state(s, h):
        j = h // hpp
        return jnp.concatenate(s[j * hpp:(j + 1) * hpp], axis=0)

    blocks = []
    for b in range(nb):
        sl = slice(b * B, (b + 1) * B)
        bf = _sel_dot(low_b, lg[0][sl])
        rb = _sel_dot(upp_b, lg[1][sl])
        q_b, k_b = q[sl], k[sl]
        qf, qb = q_b * jnp.exp(bf), q_b * jnp.exp(rb)
        blocks.append(dict(sl=sl, bf=bf, rb=rb, bfT=bf.T, rbT=rb.T, k=k_b,
                           kf=k_b * jnp.exp(-bf), kb=k_b * jnp.exp(-rb),
                           qf=[only(h, qf) for h in range(GLA_H)],
                           qb=[only(h, qb) for h in range(GLA_H)]))

    for blk in blocks:
        sl = blk["sl"]
        for h in range(GLA_H):
            a = jnp.where(low, _dot_nt(blk["qf"][h], blk["kf"][:, pair(h)]), 0.0)
            a = a + jnp.where(upp, _dot_nt(blk["qb"][h], blk["kb"][:, pair(h)]), 0.0)
            oacc[sl, h * GLA_DV:(h + 1) * GLA_DV] = _dot(a, v[sl, h * GLA_DV:(h + 1) * GLA_DV])

    def inter(blk, c, qd, cum, cumT, edge, s):
        sl = blk["sl"]
        rs = slice(sl.start + c * C, sl.start + (c + 1) * C)
        kdecT = (blk["k"] * jnp.exp(cum[edge:edge + 1, :] - cum)).T
        in_c = (rowi >= c * C) & (rowi < (c + 1) * C)
        s_in = [pair_state(s, h) for h in range(0, GLA_H, hpp)]
        for h in range(GLA_H):
            ks = slice(h * GLA_DK, (h + 1) * GLA_DK)
            vs = slice(h * GLA_DV, (h + 1) * GLA_DV)
            oacc[rs, vs] = oacc[rs, vs] + _dot(qd[h][c * C:(c + 1) * C], s_in[h // hpp])
            kv = _dot(kdecT[ks, :], jnp.where(in_c, v[sl, vs], 0.0))
            s[h] = jnp.exp(cumT[ks, edge:edge + 1]) * s[h] + kv

    for blk in blocks:
        for c in range(cpb):
            inter(blk, c, blk["qf"], blk["bf"], blk["bfT"], (c + 1) * C - 1, s_f)
    for blk in reversed(blocks):
        for c in range(cpb - 1, -1, -1):
            inter(blk, c, blk["qb"], blk["rb"], blk["rbT"], c * C, s_b)

    for h in range(GLA_H):
        vs = slice(h * GLA_DV, (h + 1) * GLA_DV)
        o_h = _rmsnorm(oacc[:, vs], nw_ref[...])
        y_ref[:, vs] = (o_h * _silu(gg[:, vs])).astype(BF16)
        if want_final:
            sfin_ref[0, 0, h] = s_f[h]
            sfin_ref[0, 1, h] = s_b[h]


def _gla_call(x, mod, ng, w, gkw, gkb, nw, s0, *, ctx):
    L = SEQ if ctx else DEC_SEQ
    nseq = BATCH if ctx else DEC_BATCH
    blk0 = 0 if ctx else N_CTX_TOK // L
    mod_of = (lambda i: (i, 0, 0)) if ctx else (lambda i: (N_CTX_BLK + LAT_BLK_PER_SEQ * i, 0, 0))
    in_specs = [pl.BlockSpec((L, D_MODEL), lambda i: (blk0 + i, 0)),
                pl.BlockSpec((1, N_MOD, D_MODEL), mod_of),
                _const_spec((1, D_MODEL)), _const_spec((D_MODEL, GLA_W)),
                _const_spec((2, LANE, GLA_QK)), _const_spec((2, GLA_QK)),
                _const_spec((1, GLA_DV))]
    args = [x, mod, ng, w, gkw, gkb, nw]
    st_spec = pl.BlockSpec((1, 2, GLA_H, GLA_DK, GLA_DV), lambda i: (i, 0, 0, 0, 0))
    out_specs = [pl.BlockSpec((L, D_MIX), lambda i: (i, 0))]
    out_shape = [jax.ShapeDtypeStruct((nseq * L, D_MIX), BF16)]
    if ctx:
        out_specs.append(st_spec)
        out_shape.append(jax.ShapeDtypeStruct((nseq, 2, GLA_H, GLA_DK, GLA_DV), F32))
    else:
        in_specs.append(st_spec)
        args.append(s0)
    return pl.pallas_call(
        functools.partial(_gla_kernel, L=L, has_init=not ctx, want_final=ctx),
        grid=(nseq,),
        in_specs=in_specs,
        out_specs=out_specs,
        out_shape=out_shape,
        scratch_shapes=[pltpu.VMEM((L, D_MIX), F32)],
        compiler_params=pltpu.CompilerParams(dimension_semantics=("arbitrary",),
                                             vmem_limit_bytes=VMEM_LIMIT),
        name="gla_ctx" if ctx else "gla_lat",
    )(*args)


def _rw_prep_kernel(x_ref, xp_ref, xn_ref, mod_ref, ng_ref, w_ref, mu_ref, a0_ref, a2_ref, g2_ref,
                    kkw_ref, ka_ref, rk_ref, w0_ref, w2_ref, bd_ref,
                    r_ref, wf_ref, wb_ref, k_ref, v_ref, kk_ref, nkka_ref, g_ref, bonus_ref):
    i = pl.program_id(0)
    j = (i - N_CTX_BLK) % LAT_BLK_PER_SEQ
    is_first = (i < N_CTX_BLK) | (j == 0)
    is_last = (i < N_CTX_BLK) | (j == LAT_BLK_PER_SEQ - 1)
    x_all = jnp.concatenate([x_ref[...], xp_ref[...], xn_ref[...]], axis=0)
    u_all = _norm_mod(x_all, ng_ref[...], mod_ref[0, 3:4, :], mod_ref[0, 4:5, :]).astype(BF16)
    p_all = jnp.dot(u_all, w_ref[...], preferred_element_type=F32)
    p = p_all[:RB]
    p_prev = jnp.where(is_first, 0.0, p_all[RB + SUB - 1:RB + SUB, :])
    p_next = jnp.where(is_last, 0.0, p_all[RB + SUB:RB + SUB + 1, :])
    rows = _iota((RB, 1), 0)
    prev = jnp.where(rows == 0, p_prev, pltpu.roll(p, 1, 0))
    nxt = jnp.where(rows == RB - 1, p_next, pltpu.roll(p, RB - 1, 0))
    p = p + (0.5 * (prev + nxt) - p) * mu_ref[...]

    r = p[:, :D_MIX]
    k = p[:, D_MIX:2 * D_MIX]
    v = p[:, 2 * D_MIX:3 * D_MIX]
    wlr = p[:, 3 * D_MIX:3 * D_MIX + LANE]
    glr = p[:, 3 * D_MIX + LANE:3 * D_MIX + 2 * LANE]
    alr = p[:, 3 * D_MIX + 2 * LANE:]
    bd = bd_ref[...]
    a = _sigmoid(a0_ref[...] + _dot(alr, a2_ref[...]))
    g = _dot(_sigmoid(glr), g2_ref[...])
    kk = k * kkw_ref[...]
    kk = kk / jnp.maximum(jnp.sqrt(_dot_sel(kk * kk, bd)), 1e-12)
    k = k * (1.0 + (a - 1.0) * ka_ref[...])
    tw = jnp.tanh(wlr).astype(BF16)
    for d, o_ref in ((0, wf_ref), (1, wb_ref)):
        pre = w0_ref[d:d + 1, :] + jnp.dot(tw, w2_ref[d], preferred_element_type=F32)
        o_ref[0] = jnp.exp(-math.exp(-0.5) * _sigmoid(pre)).T
    r_ref[0] = r.T
    k_ref[0] = k.T
    v_ref[0] = v.T
    kk_ref[0] = kk.T
    nkka_ref[0] = (-(kk * a)).T
    g_ref[...] = g
    bonus_ref[...] = _dot_sel(r * k * rk_ref[...], bd, terms=2) * v


def _rw_prep_call(x, mod, ng, w, mu, a0, a2, g2, kkw, ka, rk, w0, w2, bd):
    hb = RB // SUB
    row_spec = pl.BlockSpec((RB, D_MODEL), lambda i: (i, 0))
    out_spec = pl.BlockSpec((RB, D_MIX), lambda i: (i, 0))
    in_specs = [row_spec,
                pl.BlockSpec((SUB, D_MODEL), lambda i: (jnp.maximum(i * hb - 1, 0), 0)),
                pl.BlockSpec((SUB, D_MODEL), lambda i: (jnp.minimum((i + 1) * hb, N_TOK // SUB - 1), 0)),
                pl.BlockSpec((1, N_MOD, D_MODEL), lambda i: (i, 0, 0)),
                _const_spec((1, D_MODEL)), _const_spec((D_MODEL, RW_W)), _const_spec((1, RW_W)),
                _const_spec((1, D_MIX)), _const_spec((LANE, D_MIX)), _const_spec((LANE, D_MIX)),
                _const_spec((1, D_MIX)), _const_spec((1, D_MIX)), _const_spec((1, D_MIX)),
                _const_spec((2, D_MIX)), _const_spec((2, LANE, D_MIX)), _const_spec((D_MIX, D_MIX))]
    t_spec = pl.BlockSpec((1, D_MIX, RB), lambda i: (i, 0, 0))
    t_shape = jax.ShapeDtypeStruct((N_BLK, D_MIX, RB), F32)
    tok_shape = jax.ShapeDtypeStruct((N_TOK, D_MIX), F32)
    return pl.pallas_call(
        _rw_prep_kernel,
        grid=(N_BLK,),
        in_specs=in_specs,
        out_specs=[t_spec] * 7 + [out_spec] * 2,
        out_shape=[t_shape] * 7 + [tok_shape] * 2,
        compiler_params=pltpu.CompilerParams(dimension_semantics=("arbitrary",),
                                             vmem_limit_bytes=VMEM_LIMIT),
        name="rw_prep",
    )(x, x, x, mod, ng, w, mu, a0, a2, g2, kkw, ka, rk, w0, w2, bd)


RW_VH = LANE // (2 * DEC_BATCH * RW_H)
RW_NV_LAT = RW_N // RW_VH
LAT_ROWS = DEC_BATCH * RW_H
CTX_VIEW = (N_BLK, RW_N, RW_H, RB)
LAT_VIEW = (N_BLK // LAT_BLK_PER_SEQ, LAT_BLK_PER_SEQ, RW_N, RW_H, RB)
RW_PERM = np.arange(D_MIX).reshape(RW_H, RW_N).T.reshape(-1)


def _rl_rows(ref, nl, lat):
    x = ref[:, 0, nl] if lat else ref[:, nl]
    return x.reshape(-1, RB)


def _time_flip(x):
    ex = _iota((RB, RB), 0) + _iota((RB, RB), 1) == RB - 1
    return _dot_sel(x, jnp.where(ex, 1.0, 0.0).astype(BF16))


def _lat_spec(n_of, back):
    def index(j, n):
        return (1, LAT_BLK_PER_SEQ - 1 - j if back else j, n_of(n), 0, 0)
    return pl.BlockSpec((DEC_BATCH, 1, SUB, RW_H, RB), index)


def _rl_k_ctx_kernel(r_ref, k_ref, kk_ref, a_ref, wf_ref, wb_ref, ro, ko, kko, ao, wo):
    for nl in range(SUB):
        ro[nl] = _rl_rows(r_ref, nl, False).T
        ko[nl] = _rl_rows(k_ref, nl, False).T
        kko[nl] = _rl_rows(kk_ref, nl, False).T
        ao[nl] = _rl_rows(a_ref, nl, False).T
        wo[0, nl] = _rl_rows(wf_ref, nl, False).T
        wo[1, nl] = _rl_rows(wb_ref, nl, False).T


def _rl_k_lat_kernel(rf, rb, kf, kb, kkf, kkb, af, ab, wff, wbb, ro, ko, kko, ao, wo):
    sets = ((rf, rb, ro), (kf, kb, ko), (kkf, kkb, kko), (af, ab, ao), (wff, wbb, wo))
    back = _time_flip(jnp.concatenate(
        [_rl_rows(b, nl, True) for _, b, _ in sets for nl in range(SUB)], axis=0))
    i = 0
    for f, _, o in sets:
        for nl in range(SUB):
            pair = [_rl_rows(f, nl, True), back[i * LAT_ROWS:(i + 1) * LAT_ROWS]]
            o[nl] = jnp.concatenate(pair * RW_VH, axis=0).T
            i += 1


def _rl_k_call(r, k, kk, nkka, wf, wb, *, lat):
    nb = RW_N // SUB
    if lat:
        L, G = DEC_SEQ, 1
        fwd = _lat_spec(lambda n: n, False)
        bwd = _lat_spec(lambda n: n, True)
        ospec = pl.BlockSpec((SUB, RB, LANE), lambda j, n: (n, j, 0))
        args = [a.reshape(LAT_VIEW) for a in (r, r, k, k, kk, kk, nkka, nkka, wf, wb)]
        call = dict(grid=(LAT_BLK_PER_SEQ, nb), in_specs=[fwd, bwd] * 5, out_specs=[ospec] * 5)
        body, name = _rl_k_lat_kernel, "rl_k_lat"
        wshape = jax.ShapeDtypeStruct((RW_N, L, LANE), F32)
    else:
        L, G = SEQ, 2
        ispec = pl.BlockSpec((BATCH, SUB, RW_H, RB), lambda n: (0, n, 0, 0))
        ospec = pl.BlockSpec((SUB, RB, LANE), lambda n: (n, 0, 0))
        wspec = pl.BlockSpec((2, SUB, RB, LANE), lambda n: (0, n, 0, 0))
        args = [a.reshape(CTX_VIEW) for a in (r, k, kk, nkka, wf, wb)]
        call = dict(grid=(nb,), in_specs=[ispec] * 6, out_specs=[ospec] * 4 + [wspec])
        body, name = _rl_k_ctx_kernel, "rl_k_ctx"
        wshape = jax.ShapeDtypeStruct((2, RW_N, L, LANE), F32)
    kshape = jax.ShapeDtypeStruct((RW_N, L, LANE), F32)
    outs = pl.pallas_call(
        body,
        out_shape=[kshape] * 4 + [wshape],
        compiler_params=pltpu.CompilerParams(
            dimension_semantics=("arbitrary",) * len(call["grid"]), vmem_limit_bytes=VMEM_LIMIT),
        name=name, **call,
    )(*args)
    return list(outs[:4]) + [outs[4].reshape(G, RW_N, L, LANE)]


def _rl_v_ctx_kernel(v_ref, o_ref):
    for nl in range(SUB):
        o_ref[:, nl, :] = _rl_rows(v_ref, nl, False).T


def _rl_v_lat_kernel(f0, f1, b0, b1, o_ref):
    back = _time_flip(jnp.concatenate(
        [_rl_rows(b, nl, True) for b in (b0, b1) for nl in range(SUB)], axis=0))
    for nl in range(SUB):
        parts = []
        for h, f in enumerate((f0, f1)):
            i = h * SUB + nl
            parts += [_rl_rows(f, nl, True), back[i * LAT_ROWS:(i + 1) * LAT_ROWS]]
        o_ref[:, nl, :] = jnp.concatenate(parts, axis=0).T


def _rl_v_call(v, *, lat):
    if lat:
        nvb = RW_NV_LAT // SUB
        half = lambda h: (lambda n: h * nvb + n)
        call = dict(grid=(LAT_BLK_PER_SEQ, nvb),
                    in_specs=[_lat_spec(half(0), False), _lat_spec(half(1), False),
                              _lat_spec(half(0), True), _lat_spec(half(1), True)],
                    out_specs=pl.BlockSpec((RB, SUB, LANE), lambda j, n: (j, n, 0)))
        args = [v.reshape(LAT_VIEW)] * 4
        oshape = jax.ShapeDtypeStruct((DEC_SEQ, RW_NV_LAT, LANE), F32)
        body, name = _rl_v_lat_kernel, "rl_v_lat"
    else:
        call = dict(grid=(RW_N // SUB,),
                    in_specs=[pl.BlockSpec((BATCH, SUB, RW_H, RB), lambda n: (0, n, 0, 0))],
                    out_specs=pl.BlockSpec((RB, SUB, LANE), lambda n: (0, n, 0)))
        args = [v.reshape(CTX_VIEW)]
        oshape = jax.ShapeDtypeStruct((SEQ, RW_N, LANE), F32)
        body, name = _rl_v_ctx_kernel, "rl_v_ctx"
    return pl.pallas_call(
        body,
        out_shape=oshape,
        compiler_params=pltpu.CompilerParams(
            dimension_semantics=("arbitrary",) * len(call["grid"]), vmem_limit_bytes=VMEM_LIMIT),
        name=name, **call,
    )(*args)


def _rl_out_ctx_kernel(o_ref, ot_ref):
    for nl in range(SUB):
        x = (o_ref[0, :, nl, :] + o_ref[1, :, nl, :]).T
        ot_ref[:, nl] = x.reshape(BATCH, RW_H, RB)


def _rl_out_lat_kernel(of_ref, ob_ref, ot_ref):
    fwd = [of_ref[0, :, nl, :].T for nl in range(SUB)]
    mir = [ob_ref[0, :, nl, :].T for nl in range(SUB)]
    lo = lambda h, d: (2 * h + d) * LAT_ROWS
    back = _time_flip(jnp.concatenate(
        [mir[nl][lo(h, 1):lo(h, 1) + LAT_ROWS] for nl in range(SUB) for h in range(RW_VH)], axis=0))
    for nl in range(SUB):
        for h in range(RW_VH):
            i = nl * RW_VH + h
            x = fwd[nl][lo(h, 0):lo(h, 0) + LAT_ROWS] + back[i * LAT_ROWS:(i + 1) * LAT_ROWS]
            ot_ref[:, 0, h, nl] = x.reshape(DEC_BATCH, RW_H, RB)


def _rl_out_call(o, *, lat):
    if lat:
        nvb = RW_NV_LAT // SUB
        call = dict(grid=(LAT_BLK_PER_SEQ, nvb),
                    in_specs=[pl.BlockSpec((1, RB, SUB, LANE), lambda j, n: (0, j, n, 0)),
                              pl.BlockSpec((1, RB, SUB, LANE),
                                           lambda j, n: (0, LAT_BLK_PER_SEQ - 1 - j, n, 0))],
                    out_specs=pl.BlockSpec((DEC_BATCH, 1, RW_VH, SUB, RW_H, RB),
                                           lambda j, n: (0, j, 0, n, 0, 0)))
        oshape = (DEC_BATCH, LAT_BLK_PER_SEQ, RW_VH, RW_NV_LAT, RW_H, RB)
        args, body, name = [o, o], _rl_out_lat_kernel, "rl_out_lat"
    else:
        call = dict(grid=(RW_N // SUB,),
                    in_specs=[pl.BlockSpec((2, RB, SUB, LANE), lambda n: (0, 0, n, 0))],
                    out_specs=pl.BlockSpec((BATCH, SUB, RW_H, RB), lambda n: (0, n, 0, 0)))
        oshape = (BATCH, RW_N, RW_H, RB)
        args, body, name = [o], _rl_out_ctx_kernel, "rl_out_ctx"
    out = pl.pallas_call(
        body,
        out_shape=jax.ShapeDtypeStruct(oshape, F32),
        compiler_params=pltpu.CompilerParams(
            dimension_semantics=("arbitrary",) * len(call["grid"]), vmem_limit_bytes=VMEM_LIMIT),
        name=name, **call,
    )(*args)
    return out.reshape(N_CTX_BLK, D_MIX, RB)


def _rw_scan_kernel(*refs, vb, npart, has_init, want_final):
    it = iter(refs)
    r_ref, w_ref, k_ref, v_ref, kk_ref, nkka_ref = (next(it), next(it), next(it), next(it),
                                                    next(it), next(it))
    s0_ref = next(it) if has_init else None
    o_ref = next(it)
    sfin_ref = next(it) if want_final else None
    s_scr = next(it)
    g = pl.program_id(0)
    tb = pl.program_id(1)

    @pl.when(tb == 0)
    def _():
        if has_init:
            s_scr[...] = s0_ref[0]
        else:
            s_scr[...] = jnp.zeros(s_scr.shape, F32)

    def bcast(ref, t, kx):
        return jnp.broadcast_to(ref[kx, pl.ds(t, 1), :], (SUB, LANE))

    def t_of(i):
        return jnp.where(g == 0, i, SCAN_TB - 1 - i)

    def zeros():
        return [[jnp.zeros((SUB, LANE), F32) for _ in range(vb)] for _ in range(npart)]

    def total(parts):
        return [functools.reduce(lambda a, b: a + b, [p[j] for p in parts]) for j in range(vb)]

    t0 = t_of(0)
    acc = zeros()
    for kx in range(RW_N):
        kkb = bcast(kk_ref, t0, kx)
        for j in range(vb):
            acc[kx % npart][j] = acc[kx % npart][j] + s_scr[kx, j * SUB:(j + 1) * SUB, :] * kkb

    def step(i, skk):
        t = t_of(i)
        tn = t_of(jnp.minimum(i + 1, SCAN_TB - 1))
        vv = [v_ref[t, j * SUB:(j + 1) * SUB, :] for j in range(vb)]
        oacc, nacc = zeros(), zeros()
        for kx in range(RW_N):
            wb = jnp.broadcast_to(w_ref[0, kx, pl.ds(t, 1), :], (SUB, LANE))
            ab = bcast(nkka_ref, t, kx)
            kb = bcast(k_ref, t, kx)
            rb = bcast(r_ref, t, kx)
            kkn = bcast(kk_ref, tn, kx)
            p = kx % npart
            for j in range(vb):
                s = s_scr[kx, j * SUB:(j + 1) * SUB, :] * wb + skk[j] * ab + vv[j] * kb
                s_scr[kx, j * SUB:(j + 1) * SUB, :] = s
                oacc[p][j] = oacc[p][j] + s * rb
                nacc[p][j] = nacc[p][j] + s * kkn
        for j, o in enumerate(total(oacc)):
            o_ref[0, t, j * SUB:(j + 1) * SUB, :] = o
        return tuple(total(nacc))

    lax.fori_loop(0, SCAN_TB, step, tuple(total(acc)))

    if want_final:
        @pl.when(tb == pl.num_programs(1) - 1)
        def _():
            sfin_ref[0] = s_scr[...]


def _rw_scan_call(r, w, k, v, kk, nkka, s0, *, want_final):
    G, L = w.shape[0], r.shape[1]
    nv = v.shape[1]
    ntb = L // SCAN_TB
    has_init = s0 is not None
    tmap = lambda g, t: t + g * (ntb - 1 - 2 * t)
    kspec = pl.BlockSpec((RW_N, SCAN_TB, LANE), lambda g, t: (0, tmap(g, t), 0))
    wspec = pl.BlockSpec((1, RW_N, SCAN_TB, LANE), lambda g, t: (g, 0, tmap(g, t), 0))
    vspec = pl.BlockSpec((SCAN_TB, nv, LANE), lambda g, t: (tmap(g, t), 0, 0))
    ospec = pl.BlockSpec((1, SCAN_TB, nv, LANE), lambda g, t: (g, tmap(g, t), 0, 0))
    sspec = pl.BlockSpec((1, RW_N, nv, LANE), lambda g, t: (g, 0, 0, 0))
    in_specs = [kspec, wspec, kspec, vspec, kspec, kspec]
    args = [r, w, k, v, kk, nkka]
    if has_init:
        in_specs.append(sspec)
        args.append(s0)
    out_specs = [ospec]
    out_shape = [jax.ShapeDtypeStruct((G, L, nv, LANE), F32)]
    if want_final:
        out_specs.append(sspec)
        out_shape.append(jax.ShapeDtypeStruct((G, RW_N, nv, LANE), F32))
    return pl.pallas_call(
        functools.partial(_rw_scan_kernel, vb=nv // SUB, npart=1,
                          has_init=has_init, want_final=want_final),
        grid=(G, ntb),
        in_specs=in_specs,
        out_specs=out_specs,
        out_shape=out_shape,
        scratch_shapes=[pltpu.VMEM((RW_N, nv, LANE), F32)],
        compiler_params=pltpu.CompilerParams(dimension_semantics=("arbitrary", "arbitrary"),
                                             vmem_limit_bytes=VMEM_LIMIT),
        name="rw_scan",
    )(*args)


def _merge_kernel(x_ref, mod_ref, ng_ref, wgate_ref, yssd_c, yssd_l, ygla_c, ygla_l, orw_c, orw_l,
                  bonus_ref, g_ref, lnw_ref, lnb_ref, bd_ref, wso_ref, wgo_ref, wro_ref, wout_ref,
                  o_ref):
    is_ctx = pl.program_id(0) < N_CTX_BLK
    x = x_ref[...]
    u = _norm_mod(x, ng_ref[...], mod_ref[0, 3:4, :], mod_ref[0, 4:5, :]).astype(BF16)
    bd = bd_ref[...]
    yssd = jnp.where(is_ctx, yssd_c[...], yssd_l[...])
    ygla = jnp.where(is_ctx, ygla_c[...], ygla_l[...])
    o = jnp.where(is_ctx, orw_c[0], orw_l[0]).T
    mu = _dot_sel(o, bd, terms=2) * (1.0 / RW_N)
    oc = o - mu
    var = _dot_sel(oc * oc, bd, terms=2) * (1.0 / RW_N)
    o = oc * lax.rsqrt(var + RW_GN_EPS) * lnw_ref[...] + lnb_ref[...]
    y_rw = ((o + bonus_ref[...]) * g_ref[...]).astype(BF16)
    merged = jnp.zeros((RB, D_MODEL), F32)
    for b, (y, wo_ref) in enumerate(((yssd, wso_ref), (ygla, wgo_ref), (y_rw, wro_ref))):
        gate = _sigmoid(jnp.dot(u, wgate_ref[:, b * D_MODEL:(b + 1) * D_MODEL],
                                preferred_element_type=F32))
        merged = merged + gate * jnp.dot(y, wo_ref[0], preferred_element_type=F32)
    m = jnp.dot(merged.astype(BF16), wout_ref[0], preferred_element_type=F32)
    o_ref[...] = x + mod_ref[0, 5:6, :] * m


def _merge_call(x, mod, ng, wgate, yssd_c, yssd_l, ygla_c, ygla_l, orw_c, orw_l, bonus, g, lnw, lnb,
                bd, wso, wgo, wro, wout, l):
    row_spec = pl.BlockSpec((RB, D_MODEL), lambda i: (i, 0))
    mix_spec = pl.BlockSpec((RB, D_MIX), lambda i: (i, 0))
    mix_c = pl.BlockSpec((RB, D_MIX), lambda i: (_ctx_idx(i), 0))
    mix_l = pl.BlockSpec((RB, D_MIX), lambda i: (_lat_idx(i), 0))
    t_c = pl.BlockSpec((1, D_MIX, RB), lambda i: (_ctx_idx(i), 0, 0))
    t_l = pl.BlockSpec((1, D_MIX, RB), lambda i: (_lat_idx(i), 0, 0))
    in_specs = [row_spec, pl.BlockSpec((1, N_MOD, D_MODEL), lambda i: (i, 0, 0)),
                _const_spec((1, D_MODEL)), _const_spec((D_MODEL, 3 * D_MODEL)),
                mix_c, mix_l, mix_c, mix_l, t_c, t_l, mix_spec, mix_spec,
                _const_spec((1, D_MIX)), _const_spec((1, D_MIX)), _const_spec((D_MIX, D_MIX)),
                _stacked_spec((D_MIX, D_MODEL), (l,)), _stacked_spec((D_MIX, D_MODEL), (l,)),
                _stacked_spec((D_MIX, D_MODEL), (l,)), _stacked_spec((D_MODEL, D_MODEL), (l,))]
    return pl.pallas_call(
        _merge_kernel,
        grid=(N_BLK,),
        in_specs=in_specs,
        out_specs=row_spec,
        out_shape=jax.ShapeDtypeStruct((N_TOK, D_MODEL), F32),
        compiler_params=pltpu.CompilerParams(dimension_semantics=("arbitrary",),
                                             vmem_limit_bytes=VMEM_LIMIT),
        name="merge",
    )(x, mod, ng, wgate, yssd_c, yssd_l, ygla_c, ygla_l, orw_c, orw_l, bonus, g, lnw, lnb, bd,
      wso, wgo, wro, wout)


def _grid_pos_embed(rows, cols, dim):
    quarter = dim // 4
    omega = 1.0 / (10000.0 ** (jnp.arange(quarter, dtype=F32) / quarter))
    er = jnp.arange(rows, dtype=F32)[:, None] * omega
    ec = jnp.arange(cols, dtype=F32)[:, None] * omega
    er = jnp.concatenate([jnp.sin(er), jnp.cos(er)], axis=-1)
    ec = jnp.concatenate([jnp.sin(ec), jnp.cos(ec)], axis=-1)
    emb = jnp.concatenate([jnp.broadcast_to(er[:, None], (rows, cols, dim // 2)),
                           jnp.broadcast_to(ec[None], (rows, cols, dim // 2))], axis=-1)
    return emb.reshape(rows * cols, dim)


def _pad_cols(a, n):
    return jnp.pad(a, [(0, 0)] * (a.ndim - 1) + [(0, n - a.shape[-1])])


def _rows_at(a, off, n):
    return jnp.pad(a, ((off, n - off - a.shape[0]), (0, 0)))


def _same_head_ones():
    i = np.arange(D_MIX)
    return jnp.asarray((i[:, None] % RW_H) == (i[None, :] % RW_H), BF16)


def _hn(a):
    return a[..., RW_PERM]


def _expand01(row0):
    m = np.zeros((LANE, D_MIX), np.float32)
    for h in range(SSD_H):
        m[row0 + h, h * SSD_P:(h + 1) * SSD_P] = 1.0
    return jnp.asarray(m, BF16)


def kernel(x_prompt, x_sample, state_ssd, state_gla, state_rwkv, c, c_ctx, norm_g, w_ada, b_ada,
           ffn_gate, ffn_up, ffn_down, w_in, ssd_conv_w, ssd_conv_b, ssd_dt_bias, ssd_A_log, ssd_D,
           ssd_norm, w_ssd_o, gla_gk_w, gla_gk_b, gla_norm, w_gla_o, rw_mu, rw_w0, rw_w2, rw_a0,
           rw_a2, rw_g2, rw_kk, rw_ka, rw_rk, rw_ln_w, rw_ln_b, w_rw_o, w_out, final_norm):
    pos = _grid_pos_embed(DEC_SEQ // GRID_W, GRID_W, D_MODEL)
    x = (x_prompt.reshape(N_CTX_TOK, D_MODEL), x_sample.reshape(-1, D_MODEL))
    s0_rw = state_rwkv.reshape(DEC_BATCH, DEPTH, 2, RW_H, RW_VH, RW_NV_LAT, RW_N).transpose(
        1, 6, 5, 4, 2, 0, 3).reshape(DEPTH, 1, RW_N, RW_NV_LAT, LANE)

    cond8 = jnp.concatenate([c_ctx[None], c, jnp.zeros((SUB - 1 - DEC_BATCH, D_MODEL), F32)])
    ada = _ada_call(cond8, w_ada, b_ada)
    cond_of_blk = np.concatenate([np.zeros(N_CTX_BLK, np.int32),
                                  1 + np.arange(N_BLK - N_CTX_BLK, dtype=np.int32) // LAT_BLK_PER_SEQ])

    bd = _same_head_ones()
    ef, eb = _expand01(0), _expand01(SSD_H)
    o_ssd = D_MIX + SSD_XBC + 2 * SSD_H
    o_gla = o_ssd + 2 * GLA_H * GLA_DK + 2 * D_MIX + 2 * GLA_LR
    o_rw = o_gla + 3 * D_MIX + 2 * RW_LW + RW_LA + RW_LG

    ffn_gate_b, ffn_up_b, ffn_down_b = (a.astype(BF16) for a in (ffn_gate, ffn_up, ffn_down))
    w_ssd_o_b, w_gla_o_b, w_rw_o_b, w_out_b = (
        a.astype(BF16) for a in (w_ssd_o, w_gla_o, w_rw_o[:, RW_PERM, :], w_out))

    new_ssd, new_gla, new_rw = [], [], []
    for l in range(DEPTH):
        mod = ada[l][cond_of_blk].reshape(N_BLK, N_MOD, D_MODEL)
        ng = norm_g[l]
        wi = w_in[l]
        w_ssd = _pad_cols(wi[:, :o_ssd], SSD_W).astype(BF16)
        w_gla = _pad_cols(wi[:, o_ssd:o_gla], GLA_W).astype(BF16)
        wr_ = wi[:, o_gla:o_rw]
        def rw_cols(a):
            rkv = [_hn(a[..., i * D_MIX:(i + 1) * D_MIX]) for i in range(3)]
            return jnp.concatenate(rkv + [a[..., 1536:1664], a[..., 1728:1856],
                                          _pad_cols(a[..., 1664:1728], LANE)], axis=-1)
        w_rw = rw_cols(wr_).astype(BF16)
        mu = rw_cols(rw_mu[l])[None]
        w_gate = wi[:, o_rw:].astype(BF16)

        x = _ffn_call(x, mod, ng[0:1], ffn_gate_b, ffn_up_b, ffn_down_b, (l, 0), mod_row=0,
                      pos=pos if l == 0 else None)

        ssd_args = (x, mod, ng[1:2], w_ssd, ssd_conv_w[l], ssd_conv_b[l][None],
                    _pad_cols(ssd_dt_bias[l].reshape(1, -1), LANE),
                    _pad_cols(ssd_A_log[l].reshape(1, -1), LANE),
                    jnp.repeat(ssd_D[l], SSD_P, axis=1), ssd_norm[l][None], ef, eb)
        y_ssd_c, s_ssd = _ssd_call(*ssd_args, None, ctx=True)
        (y_ssd_l,) = _ssd_call(*ssd_args, state_ssd[:, l], ctx=False)
        new_ssd.append(s_ssd)

        gkw = jnp.stack([_rows_at(gla_gk_w[l, d], d * GLA_LR, LANE) for d in range(2)])
        gla_args = (x, mod, ng[1:2], w_gla, gkw.astype(BF16), gla_gk_b[l],
                    gla_norm[l][None])
        y_gla_c, s_gla = _gla_call(*gla_args, None, ctx=True)
        (y_gla_l,) = _gla_call(*gla_args, state_gla[:, l], ctx=False)
        new_gla.append(s_gla)

        w2p = jnp.stack([_rows_at(_hn(rw_w2[l, d]), d * RW_LW, LANE) for d in range(2)]).astype(BF16)
        r, wf, wb, k, v, kk, nkka, g, bonus = _rw_prep_call(
            x, mod, ng[1:2], w_rw, mu, _hn(rw_a0[l])[None],
            _rows_at(_hn(rw_a2[l]), 0, LANE).astype(BF16), _hn(rw_g2[l]).astype(BF16),
            _hn(rw_kk[l])[None], _hn(rw_ka[l])[None], _hn(rw_rk[l].reshape(1, D_MIX)),
            _hn(rw_w0[l]), w2p, bd)
        rc, kc, kkc, ac, wc = _rl_k_call(r, k, kk, nkka, wf, wb, lat=False)
        o_c, s_rw = _rw_scan_call(rc, wc, kc, _rl_v_call(v, lat=False), kkc, ac, None,
                                  want_final=True)
        new_rw.append(s_rw.reshape(2, RW_N, RW_N, BATCH, RW_H).transpose(3, 0, 4, 2, 1))
        rl, kl, kkl, al, wl = _rl_k_call(r, k, kk, nkka, wf, wb, lat=True)
        (o_l,) = _rw_scan_call(rl, wl, kl, _rl_v_call(v, lat=True), kkl, al, s0_rw[l],
                               want_final=False)

        x = _merge_call(x, mod, ng[1:2], w_gate, y_ssd_c, y_ssd_l, y_gla_c, y_gla_l,
                        _rl_out_call(o_c, lat=False), _rl_out_call(o_l, lat=True), bonus, g,
                        _hn(rw_ln_w[l])[None], _hn(rw_ln_b[l])[None], bd, w_ssd_o_b, w_gla_o_b, w_rw_o_b,
                        w_out_b, l)

        x = _ffn_call(x, mod, ng[2:3], ffn_gate_b, ffn_up_b, ffn_down_b, (l, 1), mod_row=6,
                      final_g=final_norm[None] if l == DEPTH - 1 else None)

    y_prompt = x[0].reshape(BATCH, SEQ, D_MODEL)
    y_sample = x[1].reshape(DEC_BATCH, DEC_SEQ, D_MODEL)
    return (y_prompt, y_sample, jnp.stack(new_ssd, axis=1), jnp.stack(new_gla, axis=1),
            jnp.stack(new_rw, axis=1))
```

```python
import functools
import math

import numpy as np
import jax
import jax.numpy as jnp
from jax import lax
from jax.experimental import pallas as pl
from jax.experimental.pallas import tpu as pltpu

F32 = jnp.float32
BF16 = jnp.bfloat16

D_MODEL = 1024
BATCH = 16
SEQ = 256
DEPTH = 2
DEC_BATCH = 4
DEC_SEQ = 1024
GRID_W = 64
D_MIX = 512
D_FF = 2816
N_MOD = 9
SSD_P = 64
SSD_H = 8
SSD_N = 64
SSD_G = 2
SSD_XBC = 768
GLA_H = 4
GLA_DK = 64
GLA_DV = 128
GLA_LR = 16
GLA_GATE_NORM = 16.0
GLA_CHUNK = 64
RW_N = 64
RW_H = 8
RW_LW = 64
RW_LA = 64
RW_LG = 128
RMS_EPS = 1e-6
RW_GN_EPS = 64e-5

LANE = 128
SUB = 8
RB = 256
N_CTX_TOK = BATCH * SEQ
N_TOK = N_CTX_TOK + DEC_BATCH * DEC_SEQ
N_BLK = N_TOK // RB
N_CTX_BLK = N_CTX_TOK // RB
LAT_BLK_PER_SEQ = DEC_SEQ // RB
SSD_W = D_MIX + SSD_XBC + LANE
GLA_QK = GLA_H * GLA_DK
GLA_W = 2 * GLA_QK + 2 * D_MIX + LANE
RW_W = 3 * D_MIX + 3 * LANE
SCAN_TB = 64
FFN_SUB = 1
VMEM_LIMIT = 56 * 1024 * 1024


def _dot(a, b):
    return jnp.dot(a.astype(BF16), b.astype(BF16), preferred_element_type=F32)


def _dot_nt(a, b):
    return lax.dot_general(a.astype(BF16), b.astype(BF16), (((1,), (1,)), ((), ())),
                           preferred_element_type=F32)


def _split3(x):
    hi = x.astype(BF16)
    r1 = x - hi.astype(F32)
    mid = r1.astype(BF16)
    lo = (r1 - mid.astype(F32)).astype(BF16)
    return hi, mid, lo


def _sel_dot(m01, x):
    hi, mid, lo = _split3(x)
    f = lambda p: jnp.dot(m01, p, preferred_element_type=F32)
    return f(hi) + f(mid) + f(lo)


def _dot_sel(x, m01, terms=3):
    f = lambda p: jnp.dot(p, m01, preferred_element_type=F32)
    return functools.reduce(lambda a, b: a + b, [f(p) for p in _split3(x)[:terms]])


def _sigmoid(x):
    return 0.5 * jnp.tanh(0.5 * x) + 0.5


def _silu(x):
    return x * _sigmoid(x)


def _softplus(x):
    return jnp.maximum(x, 0.0) + jnp.log(1.0 + jnp.exp(-jnp.abs(x)))


def _rmsnorm(x, g):
    return x * lax.rsqrt(jnp.mean(x * x, axis=-1, keepdims=True) + RMS_EPS) * g


def _norm_mod(x, g, shift, scale):
    return _rmsnorm(x, g) * (1.0 + scale) + shift


def _iota(shape, dim):
    return lax.broadcasted_iota(jnp.int32, shape, dim)


def _tri01(n, lower, chunk=None):
    t = _iota((n, n), 0)
    s = _iota((n, n), 1)
    m = (s <= t) if lower else (s >= t)
    if chunk is not None:
        m = m & ((t // chunk) == (s // chunk))
    return m


def _ada_kernel(c_ref, w_ref, b_ref, o_ref):
    o_ref[0] = _dot(_silu(c_ref[...]), w_ref[0]) + b_ref[0]


def _ada_call(cond8, w_ada, b_ada):
    tn = 2304
    nj = (N_MOD * D_MODEL) // tn
    return pl.pallas_call(
        _ada_kernel,
        grid=(DEPTH, nj),
        in_specs=[pl.BlockSpec((SUB, D_MODEL), lambda l, j: (0, 0)),
                  pl.BlockSpec((1, D_MODEL, tn), lambda l, j: (l, 0, j)),
                  pl.BlockSpec((1, 1, tn), lambda l, j: (l, 0, j))],
        out_specs=pl.BlockSpec((1, SUB, tn), lambda l, j: (l, 0, j)),
        out_shape=jax.ShapeDtypeStruct((DEPTH, SUB, N_MOD * D_MODEL), F32),
        compiler_params=pltpu.CompilerParams(dimension_semantics=("arbitrary", "arbitrary"),
                                             vmem_limit_bytes=VMEM_LIMIT),
        name="ada",
    )(cond8, w_ada, b_ada.reshape(DEPTH, 1, N_MOD * D_MODEL))


def _ffn_kernel(*refs, mod_row, first, last):
    it = iter(refs)
    is_ctx = pl.program_id(0) < N_CTX_BLK // FFN_SUB
    if first:
        xc_ref, xl_ref, pos_ref = next(it), next(it), next(it)
        x = jnp.where(is_ctx, xc_ref[...], xl_ref[...] + pos_ref[...])
    else:
        x = next(it)[...]
    mod_ref, ng_ref, wg_ref, wu_ref, wd_ref = next(it), next(it), next(it), next(it), next(it)
    fin_ref = next(it) if last else None
    shift = mod_ref[0, mod_row:mod_row + 1, :]
    scale = mod_ref[0, mod_row + 1:mod_row + 2, :]
    gate = mod_ref[0, mod_row + 2:mod_row + 3, :]
    h = _norm_mod(x, ng_ref[...], shift, scale).astype(BF16)
    a = _silu(jnp.dot(h, wg_ref[0, 0], preferred_element_type=F32))
    a = (a * jnp.dot(h, wu_ref[0, 0], preferred_element_type=F32)).astype(BF16)
    y = x + 0.5 * gate * jnp.dot(a, wd_ref[0, 0], preferred_element_type=F32)
    if last:
        y = _rmsnorm(y, fin_ref[...])
        oc_ref, ol_ref = next(it), next(it)

        @pl.when(is_ctx)
        def _():
            oc_ref[...] = y

        @pl.when(jnp.logical_not(is_ctx))
        def _():
            ol_ref[...] = y
    else:
        next(it)[...] = y


def _const_spec(shape):
    nd = len(shape)
    return pl.BlockSpec(shape, lambda i: (0,) * nd, pipeline_mode=pl.Buffered(1))


def _stacked_spec(shape, lead):
    nd = len(shape)
    return pl.BlockSpec((1,) * len(lead) + shape, lambda i: tuple(lead) + (0,) * nd,
                        pipeline_mode=pl.Buffered(1))


def _ctx_idx(i):
    return jnp.minimum(i, N_CTX_BLK - 1)


def _lat_idx(i):
    return jnp.maximum(i - N_CTX_BLK, 0)


def _ffn_call(x, mod, ng, wg, wu, wd, lj, *, mod_row, pos=None, final_g=None):
    first = pos is not None
    last = final_g is not None
    fb = FFN_SUB * RB
    nctx = N_CTX_BLK // FFN_SUB
    row_spec = pl.BlockSpec((fb, D_MODEL), lambda i: (i, 0))
    ctx_spec = pl.BlockSpec((fb, D_MODEL), lambda i: (jnp.minimum(i, nctx - 1), 0))
    lat_spec = pl.BlockSpec((fb, D_MODEL), lambda i: (jnp.maximum(i - nctx, 0), 0))
    if first:
        in_specs = [ctx_spec, lat_spec,
                    pl.BlockSpec((fb, D_MODEL),
                                 lambda i: (jnp.maximum(i - nctx, 0) % (DEC_SEQ // fb), 0))]
        args = [x[0], x[1], pos]
    else:
        in_specs = [row_spec]
        args = [x]
    in_specs += [pl.BlockSpec((1, N_MOD, D_MODEL), lambda i: (FFN_SUB * i, 0, 0)),
                 _const_spec((1, D_MODEL)), _stacked_spec((D_MODEL, D_FF), lj),
                 _stacked_spec((D_MODEL, D_FF), lj), _stacked_spec((D_FF, D_MODEL), lj)]
    args += [mod, ng, wg, wu, wd]
    if last:
        in_specs.append(_const_spec((1, D_MODEL)))
        args.append(final_g)
        half = jax.ShapeDtypeStruct((N_TOK // 2, D_MODEL), F32)
        out_specs, out_shape = [ctx_spec, lat_spec], [half, half]
    else:
        out_specs, out_shape = row_spec, jax.ShapeDtypeStruct((N_TOK, D_MODEL), F32)
    return pl.pallas_call(
        functools.partial(_ffn_kernel, mod_row=mod_row, first=first, last=last),
        grid=(N_BLK // FFN_SUB,),
        in_specs=in_specs,
        out_specs=out_specs,
        out_shape=out_shape,
        compiler_params=pltpu.CompilerParams(dimension_semantics=("arbitrary",),
                                             vmem_limit_bytes=VMEM_LIMIT),
        name="ffn",
    )(*args)


def _ssd_kernel(*refs, L, has_init, want_final):
    it = iter(refs)
    x_ref, mod_ref, ng_ref, w_ref = next(it), next(it), next(it), next(it)
    cw_ref, cb_ref, dtb_ref, alog_ref, d_ref, nw_ref = (next(it), next(it), next(it), next(it),
                                                        next(it), next(it))
    ef_ref, eb_ref = next(it), next(it)
    s0_ref = next(it) if has_init else None
    y_ref = next(it)
    sfin_ref = next(it) if want_final else None
    yacc = next(it)

    C = RB
    nc = L // C
    x = x_ref[...]
    u = _norm_mod(x, ng_ref[...], mod_ref[0, 3:4, :], mod_ref[0, 4:5, :]).astype(BF16)
    p = jnp.dot(u, w_ref[...], preferred_element_type=F32)
    z = p[:, :D_MIX]
    xbc = p[:, D_MIX:D_MIX + SSD_XBC]
    dtp = p[:, D_MIX + SSD_XBC:]
    rows = _iota((L, 1), 0)
    prev = jnp.where(rows == 0, 0.0, pltpu.roll(xbc, 1, 0))
    nxt = jnp.where(rows == L - 1, 0.0, pltpu.roll(xbc, L - 1, 0))
    xc = cb_ref[...] + prev * cw_ref[0:1, :] + xbc * cw_ref[1:2, :] + nxt * cw_ref[2:3, :]
    xc = _silu(xc)
    xs = xc[:, :D_MIX]
    bm = xc[:, D_MIX:D_MIX + SSD_G * SSD_N]
    cm = xc[:, D_MIX + SSD_G * SSD_N:]
    dt = _softplus(dtp + dtb_ref[...])
    adt = dt * (-jnp.exp(alog_ref[...]))

    tril = _tri01(C, True)
    triu = _tri01(C, False)
    tril_b = jnp.where(tril, 1.0, 0.0).astype(BF16)
    triu_b = jnp.where(triu, 1.0, 0.0).astype(BF16)
    ef = ef_ref[...]
    eb = eb_ref[...]

    cs, csT, rcs, rcsT, dtT, loc_f, loc_b = [], [], [], [], [], [], []
    for c in range(nc):
        a_c = adt[c * C:(c + 1) * C]
        a_cT = a_c.T
        cs.append(_sel_dot(tril_b, a_c))
        rcs.append(_sel_dot(triu_b, a_c))
        csT.append(_dot_sel(a_cT, triu_b))
        rcsT.append(_dot_sel(a_cT, tril_b))
        dtT.append(dt[c * C:(c + 1) * C].T)

    need_states = want_final or nc > 1
    if need_states:
        for c in range(nc):
            xs_c = xs[c * C:(c + 1) * C]
            dt_c = dt[c * C:(c + 1) * C]
            wf = jnp.exp(cs[c][C - 1:C, :] - cs[c]) * dt_c
            wb = jnp.exp(rcs[c][0:1, :] - rcs[c]) * dt_c
            xwf = (xs_c * _dot_sel(wf, ef, terms=2)).T
            xwb = (xs_c * _dot_sel(wb, eb, terms=2)).T
            lf, lb = [], []
            for h in range(SSD_H):
                g = h // (SSD_H // SSD_G)
                bm_g = bm[c * C:(c + 1) * C, g * SSD_N:(g + 1) * SSD_N]
                lf.append(_dot(xwf[h * SSD_P:(h + 1) * SSD_P, :], bm_g))
                lb.append(_dot(xwb[h * SSD_P:(h + 1) * SSD_P, :], bm_g))
            loc_f.append(lf)
            loc_b.append(lb)

    zero_s = jnp.zeros((SSD_P, SSD_N), F32)
    sin_f = [[None] * SSD_H for _ in range(nc + 1)]
    sin_b = [[None] * SSD_H for _ in range(nc + 1)]
    for h in range(SSD_H):
        sin_f[0][h] = s0_ref[0, 0, h] if has_init else zero_s
        sin_b[nc][h] = s0_ref[0, 1, h] if has_init else zero_s
    if need_states:
        for c in range(nc):
            dec = jnp.exp(cs[c][C - 1:C, :])
            for h in range(SSD_H):
                sin_f[c + 1][h] = dec[:, h:h + 1] * sin_f[c][h] + loc_f[c][h]
        for c in range(nc - 1, -1, -1):
            dec = jnp.exp(rcs[c][0:1, :])
            for h in range(SSD_H):
                sin_b[c][h] = dec[:, SSD_H + h:SSD_H + h + 1] * sin_b[c + 1][h] + loc_b[c][h]

    for c in range(nc):
        sl = slice(c * C, (c + 1) * C)
        ecs = jnp.exp(cs[c])
        ercs = jnp.exp(rcs[c])
        for h in range(SSD_H):
            g = h // (SSD_H // SSD_G)
            cm_g = cm[sl, g * SSD_N:(g + 1) * SSD_N]
            bm_g = bm[sl, g * SSD_N:(g + 1) * SSD_N]
            cb = _dot_nt(cm_g, bm_g)
            lf = jnp.exp(jnp.where(tril, cs[c][:, h:h + 1] - csT[c][h:h + 1, :], -jnp.inf))
            lb = jnp.exp(jnp.where(triu, rcs[c][:, SSD_H + h:SSD_H + h + 1]
                                   - rcsT[c][SSD_H + h:SSD_H + h + 1, :], -jnp.inf))
            m = cb * (lf * dtT[c][h:h + 1, :] + lb * dtT[c][SSD_H + h:SSD_H + h + 1, :])
            y_h = _dot(m, xs[sl, h * SSD_P:(h + 1) * SSD_P])
            if has_init or nc > 1:
                y_h = y_h + ecs[:, h:h + 1] * _dot_nt(cm_g, sin_f[c][h])
                y_h = y_h + ercs[:, SSD_H + h:SSD_H + h + 1] * _dot_nt(cm_g, sin_b[c + 1][h])
            yacc[sl, h * SSD_P:(h + 1) * SSD_P] = y_h

    y = yacc[...] + xs * (d_ref[0:1, :] + d_ref[1:2, :])
    y = _rmsnorm(y * _silu(z), nw_ref[...])
    y_ref[...] = y.astype(BF16)
    if want_final:
        for h in range(SSD_H):
            sfin_ref[0, 0, h] = sin_f[nc][h]
            sfin_ref[0, 1, h] = sin_b[0][h]


def _ssd_call(x, mod, ng, w, cw, cb, dtb, alog, dexp, nw, ef, eb, s0, *, ctx):
    L = SEQ if ctx else DEC_SEQ
    nseq = BATCH if ctx else DEC_BATCH
    blk0 = 0 if ctx else N_CTX_TOK // L
    mod_of = (lambda i: (i, 0, 0)) if ctx else (lambda i: (N_CTX_BLK + LAT_BLK_PER_SEQ * i, 0, 0))
    in_specs = [pl.BlockSpec((L, D_MODEL), lambda i: (blk0 + i, 0)),
                pl.BlockSpec((1, N_MOD, D_MODEL), mod_of),
                _const_spec((1, D_MODEL)), _const_spec((D_MODEL, SSD_W)),
                _const_spec((3, SSD_XBC)), _const_spec((1, SSD_XBC)),
                _const_spec((1, LANE)), _const_spec((1, LANE)),
                _const_spec((2, D_MIX)), _const_spec((1, D_MIX)),
                _const_spec((LANE, D_MIX)), _const_spec((LANE, D_MIX))]
    args = [x, mod, ng, w, cw, cb, dtb, alog, dexp, nw, ef, eb]
    st_spec = pl.BlockSpec((1, 2, SSD_H, SSD_P, SSD_N), lambda i: (i, 0, 0, 0, 0))
    out_specs = [pl.BlockSpec((L, D_MIX), lambda i: (i, 0))]
    out_shape = [jax.ShapeDtypeStruct((nseq * L, D_MIX), BF16)]
    if ctx:
        out_specs.append(st_spec)
        out_shape.append(jax.ShapeDtypeStruct((nseq, 2, SSD_H, SSD_P, SSD_N), F32))
    else:
        in_specs.append(st_spec)
        args.append(s0)
    return pl.pallas_call(
        functools.partial(_ssd_kernel, L=L, has_init=not ctx, want_final=ctx),
        grid=(nseq,),
        in_specs=in_specs,
        out_specs=out_specs,
        out_shape=out_shape,
        scratch_shapes=[pltpu.VMEM((L, D_MIX), F32)],
        compiler_params=pltpu.CompilerParams(dimension_semantics=("arbitrary",),
                                             vmem_limit_bytes=VMEM_LIMIT),
        name="ssd_ctx" if ctx else "ssd_lat",
    )(*args)


def _gla_kernel(*refs, L, has_init, want_final):
    it = iter(refs)
    x_ref, mod_ref, ng_ref, w_ref = next(it), next(it), next(it), next(it)
    gkw_ref, gkb_ref, nw_ref = next(it), next(it), next(it)
    s0_ref = next(it) if has_init else None
    y_ref = next(it)
    sfin_ref = next(it) if want_final else None
    oacc = next(it)

    C = GLA_CHUNK
    B = RB
    cpb = B // C
    nb = L // B
    x = x_ref[...]
    u = _norm_mod(x, ng_ref[...], mod_ref[0, 3:4, :], mod_ref[0, 4:5, :]).astype(BF16)
    p = jnp.dot(u, w_ref[...], preferred_element_type=F32)
    q = p[:, :GLA_QK] * (GLA_DK ** -0.5)
    k = p[:, GLA_QK:2 * GLA_QK]
    v = p[:, 2 * GLA_QK:2 * GLA_QK + D_MIX]
    gg = p[:, 2 * GLA_QK + D_MIX:2 * GLA_QK + 2 * D_MIX]
    lr = p[:, 2 * GLA_QK + 2 * D_MIX:].astype(BF16)
    lg = []
    for d in range(2):
        pre = jnp.dot(lr, gkw_ref[d], preferred_element_type=F32) + gkb_ref[d:d + 1, :]
        lg.append(-_softplus(-pre) / GLA_GATE_NORM)

    low = _tri01(B, True, C)
    upp = _tri01(B, False, C)
    low_b = jnp.where(low, 1.0, 0.0).astype(BF16)
    upp_b = jnp.where(upp, 1.0, 0.0).astype(BF16)
    rowi = _iota((B, 1), 0)

    if has_init:
        s_f = [s0_ref[0, 0, h] for h in range(GLA_H)]
        s_b = [s0_ref[0, 1, h] for h in range(GLA_H)]
    else:
        s_f = [jnp.zeros((GLA_DK, GLA_DV), F32)] * GLA_H
        s_b = list(s_f)

    hpp = LANE // GLA_DK
    lane_head = _iota((1, LANE), 1) // GLA_DK

    def pair(h):
        return slice((h // hpp) * LANE, (h // hpp + 1) * LANE)

    def only(h, a):
        return jnp.where(lane_head == h % hpp, a[:, pair(h)], 0.0)

    def pair_state(s, h):
        j = h // hpp
        return jnp.concatenate(s[j * hpp:(j + 1) * hpp], axis=0)

    blocks = []
    for b in range(nb):
        sl = slice(b * B, (b + 1) * B)
        bf = _sel_dot(low_b, lg[0][sl])
        rb = _sel_dot(upp_b, lg[1][sl])
        q_b, k_b = q[sl], k[sl]
        qf, qb = q_b * jnp.exp(bf), q_b * jnp.exp(rb)
        blocks.append(dict(sl=sl, bf=bf, rb=rb, bfT=bf.T, rbT=rb.T, k=k_b,
                           kf=k_b * jnp.exp(-bf), kb=k_b * jnp.exp(-rb),
                           qf=[only(h, qf) for h in range(GLA_H)],
                           qb=[only(h, qb) for h in range(GLA_H)]))

    for blk in blocks:
        sl = blk["sl"]
        for h in range(GLA_H):
            a = jnp.where(low, _dot_nt(blk["qf"][h], blk["kf"][:, pair(h)]), 0.0)
            a = a + jnp.where(upp, _dot_nt(blk["qb"][h], blk["kb"][:, pair(h)]), 0.0)
            oacc[sl, h * GLA_DV:(h + 1) * GLA_DV] = _dot(a, v[sl, h * GLA_DV:(h + 1) * GLA_DV])

    def inter(blk, c, qd, cum, cumT, edge, s):
        sl = blk["sl"]
        rs = slice(sl.start + c * C, sl.start + (c + 1) * C)
        kdecT = (blk["k"] * jnp.exp(cum[edge:edge + 1, :] - cum)).T
        in_c = (rowi >= c * C) & (rowi < (c + 1) * C)
        s_in = [pair_state(s, h) for h in range(0, GLA_H, hpp)]
        for h in range(GLA_H):
            ks = slice(h * GLA_DK, (h + 1) * GLA_DK)
            vs = slice(h * GLA_DV, (h + 1) * GLA_DV)
            oacc[rs, vs] = oacc[rs, vs] + _dot(qd[h][c * C:(c + 1) * C], s_in[h // hpp])
            kv = _dot(kdecT[ks, :], jnp.where(in_c, v[sl, vs], 0.0))
            s[h] = jnp.exp(cumT[ks, edge:edge + 1]) * s[h] + kv

    for blk in blocks:
        for c in range(cpb):
            inter(blk, c, blk["qf"], blk["bf"], blk["bfT"], (c + 1) * C - 1, s_f)
    for blk in reversed(blocks):
        for c in range(cpb - 1, -1, -1):
            inter(blk, c, blk["qb"], blk["rb"], blk["rbT"], c * C, s_b)

    for h in range(GLA_H):
        vs = slice(h * GLA_DV, (h + 1) * GLA_DV)
        o_h = _rmsnorm(oacc[:, vs], nw_ref[...])
        y_ref[:, vs] = (o_h * _silu(gg[:, vs])).astype(BF16)
        if want_final:
            sfin_ref[0, 0, h] = s_f[h]
            sfin_ref[0, 1, h] = s_b[h]


def _gla_call(x, mod, ng, w, gkw, gkb, nw, s0, *, ctx):
    L = SEQ if ctx else DEC_SEQ
    nseq = BATCH if ctx else DEC_BATCH
    blk0 = 0 if ctx else N_CTX_TOK // L
    mod_of = (lambda i: (i, 0, 0)) if ctx else (lambda i: (N_CTX_BLK + LAT_BLK_PER_SEQ * i, 0, 0))
    in_specs = [pl.BlockSpec((L, D_MODEL), lambda i: (blk0 + i, 0)),
                pl.BlockSpec((1, N_MOD, D_MODEL), mod_of),
                _const_spec((1, D_MODEL)), _const_spec((D_MODEL, GLA_W)),
                _const_spec((2, LANE, GLA_QK)), _const_spec((2, GLA_QK)),
                _const_spec((1, GLA_DV))]
    args = [x, mod, ng, w, gkw, gkb, nw]
    st_spec = pl.BlockSpec((1, 2, GLA_H, GLA_DK, GLA_DV), lambda i: (i, 0, 0, 0, 0))
    out_specs = [pl.BlockSpec((L, D_MIX), lambda i: (i, 0))]
    out_shape = [jax.ShapeDtypeStruct((nseq * L, D_MIX), BF16)]
    if ctx:
        out_specs.append(st_spec)
        out_shape.append(jax.ShapeDtypeStruct((nseq, 2, GLA_H, GLA_DK, GLA_DV), F32))
    else:
        in_specs.append(st_spec)
        args.append(s0)
    return pl.pallas_call(
        functools.partial(_gla_kernel, L=L, has_init=not ctx, want_final=ctx),
        grid=(nseq,),
        in_specs=in_specs,
        out_specs=out_specs,
        out_shape=out_shape,
        scratch_shapes=[pltpu.VMEM((L, D_MIX), F32)],
        compiler_params=pltpu.CompilerParams(dimension_semantics=("arbitrary",),
                                             vmem_limit_bytes=VMEM_LIMIT),
        name="gla_ctx" if ctx else "gla_lat",
    )(*args)


def _rw_prep_kernel(x_ref, xp_ref, xn_ref, mod_ref, ng_ref, w_ref, mu_ref, a0_ref, a2_ref, g2_ref,
                    kkw_ref, ka_ref, rk_ref, w0_ref, w2_ref, bd_ref,
                    r_ref, wf_ref, wb_ref, k_ref, v_ref, kk_ref, nkka_ref, g_ref, bonus_ref):
    i = pl.program_id(0)
    j = (i - N_CTX_BLK) % LAT_BLK_PER_SEQ
    is_first = (i < N_CTX_BLK) | (j == 0)
    is_last = (i < N_CTX_BLK) | (j == LAT_BLK_PER_SEQ - 1)
    x_all = jnp.concatenate([x_ref[...], xp_ref[...], xn_ref[...]], axis=0)
    u_all = _norm_mod(x_all, ng_ref[...], mod_ref[0, 3:4, :], mod_ref[0, 4:5, :]).astype(BF16)
    p_all = jnp.dot(u_all, w_ref[...], preferred_element_type=F32)
    p = p_all[:RB]
    p_prev = jnp.where(is_first, 0.0, p_all[RB + SUB - 1:RB + SUB, :])
    p_next = jnp.where(is_last, 0.0, p_all[RB + SUB:RB + SUB + 1, :])
    rows = _iota((RB, 1), 0)
    prev = jnp.where(rows == 0, p_prev, pltpu.roll(p, 1, 0))
    nxt = jnp.where(rows == RB - 1, p_next, pltpu.roll(p, RB - 1, 0))
    p = p + (0.5 * (prev + nxt) - p) * mu_ref[...]

    r = p[:, :D_MIX]
    k = p[:, D_MIX:2 * D_MIX]
    v = p[:, 2 * D_MIX:3 * D_MIX]
    wlr = p[:, 3 * D_MIX:3 * D_MIX + LANE]
    glr = p[:, 3 * D_MIX + LANE:3 * D_MIX + 2 * LANE]
    alr = p[:, 3 * D_MIX + 2 * LANE:]
    bd = bd_ref[...]
    a = _sigmoid(a0_ref[...] + _dot(alr, a2_ref[...]))
    g = _dot(_sigmoid(glr), g2_ref[...])
    kk = k * kkw_ref[...]
    kk = kk / jnp.maximum(jnp.sqrt(_dot_sel(kk * kk, bd)), 1e-12)
    k = k * (1.0 + (a - 1.0) * ka_ref[...])
    tw = jnp.tanh(wlr).astype(BF16)
    for d, o_ref in ((0, wf_ref), (1, wb_ref)):
        pre = w0_ref[d:d + 1, :] + jnp.dot(tw, w2_ref[d], preferred_element_type=F32)
        o_ref[0] = jnp.exp(-math.exp(-0.5) * _sigmoid(pre)).T
    r_ref[0] = r.T
    k_ref[0] = k.T
    v_ref[0] = v.T
    kk_ref[0] = kk.T
    nkka_ref[0] = (-(kk * a)).T
    g_ref[...] = g
    bonus_ref[...] = _dot_sel(r * k * rk_ref[...], bd, terms=2) * v


def _rw_prep_call(x, mod, ng, w, mu, a0, a2, g2, kkw, ka, rk, w0, w2, bd):
    hb = RB // SUB
    row_spec = pl.BlockSpec((RB, D_MODEL), lambda i: (i, 0))
    out_spec = pl.BlockSpec((RB, D_MIX), lambda i: (i, 0))
    in_specs = [row_spec,
                pl.BlockSpec((SUB, D_MODEL), lambda i: (jnp.maximum(i * hb - 1, 0), 0)),
                pl.BlockSpec((SUB, D_MODEL), lambda i: (jnp.minimum((i + 1) * hb, N_TOK // SUB - 1), 0)),
                pl.BlockSpec((1, N_MOD, D_MODEL), lambda i: (i, 0, 0)),
                _const_spec((1, D_MODEL)), _const_spec((D_MODEL, RW_W)), _const_spec((1, RW_W)),
                _const_spec((1, D_MIX)), _const_spec((LANE, D_MIX)), _const_spec((LANE, D_MIX)),
                _const_spec((1, D_MIX)), _const_spec((1, D_MIX)), _const_spec((1, D_MIX)),
                _const_spec((2, D_MIX)), _const_spec((2, LANE, D_MIX)), _const_spec((D_MIX, D_MIX))]
    t_spec = pl.BlockSpec((1, D_MIX, RB), lambda i: (i, 0, 0))
    t_shape = jax.ShapeDtypeStruct((N_BLK, D_MIX, RB), F32)
    tok_shape = jax.ShapeDtypeStruct((N_TOK, D_MIX), F32)
    return pl.pallas_call(
        _rw_prep_kernel,
        grid=(N_BLK,),
        in_specs=in_specs,
        out_specs=[t_spec] * 7 + [out_spec] * 2,
        out_shape=[t_shape] * 7 + [tok_shape] * 2,
        compiler_params=pltpu.CompilerParams(dimension_semantics=("arbitrary",),
                                             vmem_limit_bytes=VMEM_LIMIT),
        name="rw_prep",
    )(x, x, x, mod, ng, w, mu, a0, a2, g2, kkw, ka, rk, w0, w2, bd)


RW_VH = LANE // (2 * DEC_BATCH * RW_H)
RW_NV_LAT = RW_N // RW_VH
LAT_ROWS = DEC_BATCH * RW_H
LAT_VIEW = (N_BLK // LAT_BLK_PER_SEQ, LAT_BLK_PER_SEQ, RW_H, RW_N, RB)


def _rl_rows(ref, nl, lat):
    x = ref[:, 0, :, nl, :] if lat else ref[:, :, nl, :]
    return x.reshape(-1, RB)


def _time_flip(x):
    ex = _iota((RB, RB), 0) + _iota((RB, RB), 1) == RB - 1
    return _dot_sel(x, jnp.where(ex, 1.0, 0.0).astype(BF16))


def _lat_spec(n_of, back):
    def index(j, n):
        return (1, LAT_BLK_PER_SEQ - 1 - j if back else j, 0, n_of(n), 0)
    return pl.BlockSpec((DEC_BATCH, 1, RW_H, SUB, RB), index)


def _rl_k_ctx_kernel(r_ref, k_ref, kk_ref, a_ref, wf_ref, wb_ref, ro, ko, kko, ao, wo):
    for nl in range(SUB):
        ro[nl] = _rl_rows(r_ref, nl, False).T
        ko[nl] = _rl_rows(k_ref, nl, False).T
        kko[nl] = _rl_rows(kk_ref, nl, False).T
        ao[nl] = _rl_rows(a_ref, nl, False).T
        wo[0, nl] = _rl_rows(wf_ref, nl, False).T
        wo[1, nl] = _rl_rows(wb_ref, nl, False).T


def _rl_k_lat_kernel(rf, rb, kf, kb, kkf, kkb, af, ab, wff, wbb, ro, ko, kko, ao, wo):
    sets = ((rf, rb, ro), (kf, kb, ko), (kkf, kkb, kko), (af, ab, ao), (wff, wbb, wo))
    back = _time_flip(jnp.concatenate(
        [_rl_rows(b, nl, True) for _, b, _ in sets for nl in range(SUB)], axis=0))
    i = 0
    for f, _, o in sets:
        for nl in range(SUB):
            pair = [_rl_rows(f, nl, True), back[i * LAT_ROWS:(i + 1) * LAT_ROWS]]
            o[nl] = jnp.concatenate(pair * RW_VH, axis=0).T
            i += 1


def _rl_k_call(r, k, kk, nkka, wf, wb, *, lat):
    nb = RW_N // SUB
    if lat:
        L, G = DEC_SEQ, 1
        fwd = _lat_spec(lambda n: n, False)
        bwd = _lat_spec(lambda n: n, True)
        ospec = pl.BlockSpec((SUB, RB, LANE), lambda j, n: (n, j, 0))
        args = [a.reshape(LAT_VIEW) for a in (r, r, k, k, kk, kk, nkka, nkka, wf, wb)]
        call = dict(grid=(LAT_BLK_PER_SEQ, nb), in_specs=[fwd, bwd] * 5, out_specs=[ospec] * 5)
        body, name = _rl_k_lat_kernel, "rl_k_lat"
        wshape = jax.ShapeDtypeStruct((RW_N, L, LANE), F32)
    else:
        L, G = SEQ, 2
        ispec = pl.BlockSpec((BATCH, RW_H, SUB, RB), lambda n: (0, 0, n, 0))
        ospec = pl.BlockSpec((SUB, RB, LANE), lambda n: (n, 0, 0))
        wspec = pl.BlockSpec((2, SUB, RB, LANE), lambda n: (0, n, 0, 0))
        args = [a.reshape(N_BLK, RW_H, RW_N, RB) for a in (r, k, kk, nkka, wf, wb)]
        call = dict(grid=(nb,), in_specs=[ispec] * 6, out_specs=[ospec] * 4 + [wspec])
        body, name = _rl_k_ctx_kernel, "rl_k_ctx"
        wshape = jax.ShapeDtypeStruct((2, RW_N, L, LANE), F32)
    kshape = jax.ShapeDtypeStruct((RW_N, L, LANE), F32)
    outs = pl.pallas_call(
        body,
        out_shape=[kshape] * 4 + [wshape],
        compiler_params=pltpu.CompilerParams(
            dimension_semantics=("arbitrary",) * len(call["grid"]), vmem_limit_bytes=VMEM_LIMIT),
        name=name, **call,
    )(*args)
    return list(outs[:4]) + [outs[4].reshape(G, RW_N, L, LANE)]


def _rl_v_ctx_kernel(v_ref, o_ref):
    for nl in range(SUB):
        o_ref[:, nl, :] = _rl_rows(v_ref, nl, False).T


def _rl_v_lat_kernel(f0, f1, b0, b1, o_ref):
    back = _time_flip(jnp.concatenate(
        [_rl_rows(b, nl, True) for b in (b0, b1) for nl in range(SUB)], axis=0))
    for nl in range(SUB):
        parts = []
        for h, f in enumerate((f0, f1)):
            i = h * SUB + nl
            parts += [_rl_rows(f, nl, True), back[i * LAT_ROWS:(i + 1) * LAT_ROWS]]
        o_ref[:, nl, :] = jnp.concatenate(parts, axis=0).T


def _rl_v_call(v, *, lat):
    if lat:
        nvb = RW_NV_LAT // SUB
        half = lambda h: (lambda n: h * nvb + n)
        call = dict(grid=(LAT_BLK_PER_SEQ, nvb),
                    in_specs=[_lat_spec(half(0), False), _lat_spec(half(1), False),
                              _lat_spec(half(0), True), _lat_spec(half(1), True)],
                    out_specs=pl.BlockSpec((RB, SUB, LANE), lambda j, n: (j, n, 0)))
        args = [v.reshape(LAT_VIEW)] * 4
        oshape = jax.ShapeDtypeStruct((DEC_SEQ, RW_NV_LAT, LANE), F32)
        body, name = _rl_v_lat_kernel, "rl_v_lat"
    else:
        call = dict(grid=(RW_N // SUB,),
                    in_specs=[pl.BlockSpec((BATCH, RW_H, SUB, RB), lambda n: (0, 0, n, 0))],
                    out_specs=pl.BlockSpec((RB, SUB, LANE), lambda n: (0, n, 0)))
        args = [v.reshape(N_BLK, RW_H, RW_N, RB)]
        oshape = jax.ShapeDtypeStruct((SEQ, RW_N, LANE), F32)
        body, name = _rl_v_ctx_kernel, "rl_v_ctx"
    return pl.pallas_call(
        body,
        out_shape=oshape,
        compiler_params=pltpu.CompilerParams(
            dimension_semantics=("arbitrary",) * len(call["grid"]), vmem_limit_bytes=VMEM_LIMIT),
        name=name, **call,
    )(*args)


def _rl_out_ctx_kernel(o_ref, ot_ref):
    for nl in range(SUB):
        x = (o_ref[0, :, nl, :] + o_ref[1, :, nl, :]).T
        ot_ref[:, :, nl, :] = x.reshape(BATCH, RW_H, RB)


def _rl_out_lat_kernel(of_ref, ob_ref, ot_ref):
    fwd = [of_ref[0, :, nl, :].T for nl in range(SUB)]
    mir = [ob_ref[0, :, nl, :].T for nl in range(SUB)]
    lo = lambda h, d: (2 * h + d) * LAT_ROWS
    back = _time_flip(jnp.concatenate(
        [mir[nl][lo(h, 1):lo(h, 1) + LAT_ROWS] for nl in range(SUB) for h in range(RW_VH)], axis=0))
    for nl in range(SUB):
        for h in range(RW_VH):
            i = nl * RW_VH + h
            x = fwd[nl][lo(h, 0):lo(h, 0) + LAT_ROWS] + back[i * LAT_ROWS:(i + 1) * LAT_ROWS]
            ot_ref[:, 0, :, h, nl, :] = x.reshape(DEC_BATCH, RW_H, RB)


def _rl_out_call(o, *, lat):
    if lat:
        nvb = RW_NV_LAT // SUB
        call = dict(grid=(LAT_BLK_PER_SEQ, nvb),
                    in_specs=[pl.BlockSpec((1, RB, SUB, LANE), lambda j, n: (0, j, n, 0)),
                              pl.BlockSpec((1, RB, SUB, LANE),
                                           lambda j, n: (0, LAT_BLK_PER_SEQ - 1 - j, n, 0))],
                    out_specs=pl.BlockSpec((DEC_BATCH, 1, RW_H, RW_VH, SUB, RB),
                                           lambda j, n: (0, j, 0, 0, n, 0)))
        oshape = (DEC_BATCH, LAT_BLK_PER_SEQ, RW_H, RW_VH, RW_NV_LAT, RB)
        args, body, name = [o, o], _rl_out_lat_kernel, "rl_out_lat"
    else:
        call = dict(grid=(RW_N // SUB,),
                    in_specs=[pl.BlockSpec((2, RB, SUB, LANE), lambda n: (0, 0, n, 0))],
                    out_specs=pl.BlockSpec((BATCH, RW_H, SUB, RB), lambda n: (0, 0, n, 0)))
        oshape = (BATCH, RW_H, RW_N, RB)
        args, body, name = [o], _rl_out_ctx_kernel, "rl_out_ctx"
    out = pl.pallas_call(
        body,
        out_shape=jax.ShapeDtypeStruct(oshape, F32),
        compiler_params=pltpu.CompilerParams(
            dimension_semantics=("arbitrary",) * len(call["grid"]), vmem_limit_bytes=VMEM_LIMIT),
        name=name, **call,
    )(*args)
    return out.reshape(N_CTX_BLK, D_MIX, RB)


def _rw_scan_kernel(*refs, vb, npart, has_init, want_final):
    it = iter(refs)
    r_ref, w_ref, k_ref, v_ref, kk_ref, nkka_ref = (next(it), next(it), next(it), next(it),
                                                    next(it), next(it))
    s0_ref = next(it) if has_init else None
    o_ref = next(it)
    sfin_ref = next(it) if want_final else None
    s_scr = next(it)
    g = pl.program_id(0)
    tb = pl.program_id(1)

    @pl.when(tb == 0)
    def _():
        if has_init:
            s_scr[...] = s0_ref[0]
        else:
            s_scr[...] = jnp.zeros(s_scr.shape, F32)

    def bcast(ref, t, kx):
        return jnp.broadcast_to(ref[kx, pl.ds(t, 1), :], (SUB, LANE))

    def t_of(i):
        return jnp.where(g == 0, i, SCAN_TB - 1 - i)

    def zeros():
        return [[jnp.zeros((SUB, LANE), F32) for _ in range(vb)] for _ in range(npart)]

    def total(parts):
        return [functools.reduce(lambda a, b: a + b, [p[j] for p in parts]) for j in range(vb)]

    t0 = t_of(0)
    acc = zeros()
    for kx in range(RW_N):
        kkb = bcast(kk_ref, t0, kx)
        for j in range(vb):
            acc[kx % npart][j] = acc[kx % npart][j] + s_scr[kx, j * SUB:(j + 1) * SUB, :] * kkb

    def step(i, skk):
        t = t_of(i)
        tn = t_of(jnp.minimum(i + 1, SCAN_TB - 1))
        vv = [v_ref[t, j * SUB:(j + 1) * SUB, :] for j in range(vb)]
        oacc, nacc = zeros(), zeros()
        for kx in range(RW_N):
            wb = jnp.broadcast_to(w_ref[0, kx, pl.ds(t, 1), :], (SUB, LANE))
            ab = bcast(nkka_ref, t, kx)
            kb = bcast(k_ref, t, kx)
            rb = bcast(r_ref, t, kx)
            kkn = bcast(kk_ref, tn, kx)
            p = kx % npart
            for j in range(vb):
                s = s_scr[kx, j * SUB:(j + 1) * SUB, :] * wb + skk[j] * ab + vv[j] * kb
                s_scr[kx, j * SUB:(j + 1) * SUB, :] = s
                oacc[p][j] = oacc[p][j] + s * rb
                nacc[p][j] = nacc[p][j] + s * kkn
        for j, o in enumerate(total(oacc)):
            o_ref[0, t, j * SUB:(j + 1) * SUB, :] = o
        return tuple(total(nacc))

    lax.fori_loop(0, SCAN_TB, step, tuple(total(acc)))

    if want_final:
        @pl.when(tb == pl.num_programs(1) - 1)
        def _():
            sfin_ref[0] = s_scr[...]


def _rw_scan_call(r, w, k, v, kk, nkka, s0, *, want_final):
    G, L = w.shape[0], r.shape[1]
    nv = v.shape[1]
    ntb = L // SCAN_TB
    has_init = s0 is not None
    tmap = lambda g, t: t + g * (ntb - 1 - 2 * t)
    kspec = pl.BlockSpec((RW_N, SCAN_TB, LANE), lambda g, t: (0, tmap(g, t), 0))
    wspec = pl.BlockSpec((1, RW_N, SCAN_TB, LANE), lambda g, t: (g, 0, tmap(g, t), 0))
    vspec = pl.BlockSpec((SCAN_TB, nv, LANE), lambda g, t: (tmap(g, t), 0, 0))
    ospec = pl.BlockSpec((1, SCAN_TB, nv, LANE), lambda g, t: (g, tmap(g, t), 0, 0))
    sspec = pl.BlockSpec((1, RW_N, nv, LANE), lambda g, t: (g, 0, 0, 0))
    in_specs = [kspec, wspec, kspec, vspec, kspec, kspec]
    args = [r, w, k, v, kk, nkka]
    if has_init:
        in_specs.append(sspec)
        args.append(s0)
    out_specs = [ospec]
    out_shape = [jax.ShapeDtypeStruct((G, L, nv, LANE), F32)]
    if want_final:
        out_specs.append(sspec)
        out_shape.append(jax.ShapeDtypeStruct((G, RW_N, nv, LANE), F32))
    return pl.pallas_call(
        functools.partial(_rw_scan_kernel, vb=nv // SUB, npart=1,
                          has_init=has_init, want_final=want_final),
        grid=(G, ntb),
        in_specs=in_specs,
        out_specs=out_specs,
        out_shape=out_shape,
        scratch_shapes=[pltpu.VMEM((RW_N, nv, LANE), F32)],
        compiler_params=pltpu.CompilerParams(dimension_semantics=("arbitrary", "arbitrary"),
                                             vmem_limit_bytes=VMEM_LIMIT),
        name="rw_scan",
    )(*args)


def _merge_kernel(x_ref, mod_ref, ng_ref, wgate_ref, yssd_c, yssd_l, ygla_c, ygla_l, orw_c, orw_l,
                  bonus_ref, g_ref, lnw_ref, lnb_ref, bd_ref, wso_ref, wgo_ref, wro_ref, wout_ref,
                  o_ref):
    is_ctx = pl.program_id(0) < N_CTX_BLK
    x = x_ref[...]
    u = _norm_mod(x, ng_ref[...], mod_ref[0, 3:4, :], mod_ref[0, 4:5, :]).astype(BF16)
    bd = bd_ref[...]
    yssd = jnp.where(is_ctx, yssd_c[...], yssd_l[...])
    ygla = jnp.where(is_ctx, ygla_c[...], ygla_l[...])
    o = jnp.where(is_ctx, orw_c[0], orw_l[0]).T
    mu = _dot_sel(o, bd, terms=2) * (1.0 / RW_N)
    oc = o - mu
    var = _dot_sel(oc * oc, bd, terms=2) * (1.0 / RW_N)
    o = oc * lax.rsqrt(var + RW_GN_EPS) * lnw_ref[...] + lnb_ref[...]
    y_rw = ((o + bonus_ref[...]) * g_ref[...]).astype(BF16)
    merged = jnp.zeros((RB, D_MODEL), F32)
    for b, (y, wo_ref) in enumerate(((yssd, wso_ref), (ygla, wgo_ref), (y_rw, wro_ref))):
        gate = _sigmoid(jnp.dot(u, wgate_ref[:, b * D_MODEL:(b + 1) * D_MODEL],
                                preferred_element_type=F32))
        merged = merged + gate * jnp.dot(y, wo_ref[0], preferred_element_type=F32)
    m = jnp.dot(merged.astype(BF16), wout_ref[0], preferred_element_type=F32)
    o_ref[...] = x + mod_ref[0, 5:6, :] * m


def _merge_call(x, mod, ng, wgate, yssd_c, yssd_l, ygla_c, ygla_l, orw_c, orw_l, bonus, g, lnw, lnb,
                bd, wso, wgo, wro, wout, l):
    row_spec = pl.BlockSpec((RB, D_MODEL), lambda i: (i, 0))
    mix_spec = pl.BlockSpec((RB, D_MIX), lambda i: (i, 0))
    mix_c = pl.BlockSpec((RB, D_MIX), lambda i: (_ctx_idx(i), 0))
    mix_l = pl.BlockSpec((RB, D_MIX), lambda i: (_lat_idx(i), 0))
    t_c = pl.BlockSpec((1, D_MIX, RB), lambda i: (_ctx_idx(i), 0, 0))
    t_l = pl.BlockSpec((1, D_MIX, RB), lambda i: (_lat_idx(i), 0, 0))
    in_specs = [row_spec, pl.BlockSpec((1, N_MOD, D_MODEL), lambda i: (i, 0, 0)),
                _const_spec((1, D_MODEL)), _const_spec((D_MODEL, 3 * D_MODEL)),
                mix_c, mix_l, mix_c, mix_l, t_c, t_l, mix_spec, mix_spec,
                _const_spec((1, D_MIX)), _const_spec((1, D_MIX)), _const_spec((D_MIX, D_MIX)),
                _stacked_spec((D_MIX, D_MODEL), (l,)), _stacked_spec((D_MIX, D_MODEL), (l,)),
                _stacked_spec((D_MIX, D_MODEL), (l,)), _stacked_spec((D_MODEL, D_MODEL), (l,))]
    return pl.pallas_call(
        _merge_kernel,
        grid=(N_BLK,),
        in_specs=in_specs,
        out_specs=row_spec,
        out_shape=jax.ShapeDtypeStruct((N_TOK, D_MODEL), F32),
        compiler_params=pltpu.CompilerParams(dimension_semantics=("arbitrary",),
                                             vmem_limit_bytes=VMEM_LIMIT),
        name="merge",
    )(x, mod, ng, wgate, yssd_c, yssd_l, ygla_c, ygla_l, orw_c, orw_l, bonus, g, lnw, lnb, bd,
      wso, wgo, wro, wout)


def _grid_pos_embed(rows, cols, dim):
    quarter = dim // 4
    omega = 1.0 / (10000.0 ** (jnp.arange(quarter, dtype=F32) / quarter))
    er = jnp.arange(rows, dtype=F32)[:, None] * omega
    ec = jnp.arange(cols, dtype=F32)[:, None] * omega
    er = jnp.concatenate([jnp.sin(er), jnp.cos(er)], axis=-1)
    ec = jnp.concatenate([jnp.sin(ec), jnp.cos(ec)], axis=-1)
    emb = jnp.concatenate([jnp.broadcast_to(er[:, None], (rows, cols, dim // 2)),
                           jnp.broadcast_to(ec[None], (rows, cols, dim // 2))], axis=-1)
    return emb.reshape(rows * cols, dim)


def _pad_cols(a, n):
    return jnp.pad(a, [(0, 0)] * (a.ndim - 1) + [(0, n - a.shape[-1])])


def _rows_at(a, off, n):
    return jnp.pad(a, ((off, n - off - a.shape[0]), (0, 0)))


def _block_diag_ones(n, blk):
    i = np.arange(n)
    return jnp.asarray((i[:, None] // blk) == (i[None, :] // blk), BF16)


def _expand01(row0):
    m = np.zeros((LANE, D_MIX), np.float32)
    for h in range(SSD_H):
        m[row0 + h, h * SSD_P:(h + 1) * SSD_P] = 1.0
    return jnp.asarray(m, BF16)


def kernel(x_prompt, x_sample, state_ssd, state_gla, state_rwkv, c, c_ctx, norm_g, w_ada, b_ada,
           ffn_gate, ffn_up, ffn_down, w_in, ssd_conv_w, ssd_conv_b, ssd_dt_bias, ssd_A_log, ssd_D,
           ssd_norm, w_ssd_o, gla_gk_w, gla_gk_b, gla_norm, w_gla_o, rw_mu, rw_w0, rw_w2, rw_a0,
           rw_a2, rw_g2, rw_kk, rw_ka, rw_rk, rw_ln_w, rw_ln_b, w_rw_o, w_out, final_norm):
    pos = _grid_pos_embed(DEC_SEQ // GRID_W, GRID_W, D_MODEL)
    x = (x_prompt.reshape(N_CTX_TOK, D_MODEL), x_sample.reshape(-1, D_MODEL))
    s0_rw = state_rwkv.reshape(DEC_BATCH, DEPTH, 2, RW_H, RW_VH, RW_NV_LAT, RW_N).transpose(
        1, 6, 5, 4, 2, 0, 3).reshape(DEPTH, 1, RW_N, RW_NV_LAT, LANE)

    cond8 = jnp.concatenate([c_ctx[None], c, jnp.zeros((SUB - 1 - DEC_BATCH, D_MODEL), F32)])
    ada = _ada_call(cond8, w_ada, b_ada)
    cond_of_blk = np.concatenate([np.zeros(N_CTX_BLK, np.int32),
                                  1 + np.arange(N_BLK - N_CTX_BLK, dtype=np.int32) // LAT_BLK_PER_SEQ])

    bd = _block_diag_ones(D_MIX, RW_N)
    ef, eb = _expand01(0), _expand01(SSD_H)
    o_ssd = D_MIX + SSD_XBC + 2 * SSD_H
    o_gla = o_ssd + 2 * GLA_H * GLA_DK + 2 * D_MIX + 2 * GLA_LR
    o_rw = o_gla + 3 * D_MIX + 2 * RW_LW + RW_LA + RW_LG

    ffn_gate_b, ffn_up_b, ffn_down_b = (a.astype(BF16) for a in (ffn_gate, ffn_up, ffn_down))
    w_ssd_o_b, w_gla_o_b, w_rw_o_b, w_out_b = (a.astype(BF16)
                                               for a in (w_ssd_o, w_gla_o, w_rw_o, w_out))

    new_ssd, new_gla, new_rw = [], [], []
    for l in range(DEPTH):
        mod = ada[l][cond_of_blk].reshape(N_BLK, N_MOD, D_MODEL)
        ng = norm_g[l]
        wi = w_in[l]
        w_ssd = _pad_cols(wi[:, :o_ssd], SSD_W).astype(BF16)
        w_gla = _pad_cols(wi[:, o_ssd:o_gla], GLA_W).astype(BF16)
        wr_ = wi[:, o_gla:o_rw]
        w_rw = jnp.concatenate([wr_[:, :1664], wr_[:, 1728:1856], _pad_cols(wr_[:, 1664:1728], LANE)],
                               axis=1).astype(BF16)
        mu_ = rw_mu[l]
        mu = jnp.concatenate([mu_[:1664], mu_[1728:1856], _pad_cols(mu_[1664:1728], LANE)])[None]
        w_gate = wi[:, o_rw:].astype(BF16)

        x = _ffn_call(x, mod, ng[0:1], ffn_gate_b, ffn_up_b, ffn_down_b, (l, 0), mod_row=0,
                      pos=pos if l == 0 else None)

        ssd_args = (x, mod, ng[1:2], w_ssd, ssd_conv_w[l], ssd_conv_b[l][None],
                    _pad_cols(ssd_dt_bias[l].reshape(1, -1), LANE),
                    _pad_cols(ssd_A_log[l].reshape(1, -1), LANE),
                    jnp.repeat(ssd_D[l], SSD_P, axis=1), ssd_norm[l][None], ef, eb)
        y_ssd_c, s_ssd = _ssd_call(*ssd_args, None, ctx=True)
        (y_ssd_l,) = _ssd_call(*ssd_args, state_ssd[:, l], ctx=False)
        new_ssd.append(s_ssd)

        gkw = jnp.stack([_rows_at(gla_gk_w[l, d], d * GLA_LR, LANE) for d in range(2)])
        gla_args = (x, mod, ng[1:2], w_gla, gkw.astype(BF16), gla_gk_b[l],
                    gla_norm[l][None])
        y_gla_c, s_gla = _gla_call(*gla_args, None, ctx=True)
        (y_gla_l,) = _gla_call(*gla_args, state_gla[:, l], ctx=False)
        new_gla.append(s_gla)

        w2p = jnp.stack([_rows_at(rw_w2[l, d], d * RW_LW, LANE) for d in range(2)]).astype(BF16)
        r, wf, wb, k, v, kk, nkka, g, bonus = _rw_prep_call(
            x, mod, ng[1:2], w_rw, mu, rw_a0[l][None], _rows_at(rw_a2[l], 0, LANE).astype(BF16),
            rw_g2[l].astype(BF16), rw_kk[l][None], rw_ka[l][None], rw_rk[l].reshape(1, D_MIX),
            rw_w0[l], w2p, bd)
        rc, kc, kkc, ac, wc = _rl_k_call(r, k, kk, nkka, wf, wb, lat=False)
        o_c, s_rw = _rw_scan_call(rc, wc, kc, _rl_v_call(v, lat=False), kkc, ac, None,
                                  want_final=True)
        new_rw.append(s_rw.reshape(2, RW_N, RW_N, BATCH, RW_H).transpose(3, 0, 4, 2, 1))
        rl, kl, kkl, al, wl = _rl_k_call(r, k, kk, nkka, wf, wb, lat=True)
        (o_l,) = _rw_scan_call(rl, wl, kl, _rl_v_call(v, lat=True), kkl, al, s0_rw[l],
                               want_final=False)

        x = _merge_call(x, mod, ng[1:2], w_gate, y_ssd_c, y_ssd_l, y_gla_c, y_gla_l,
                        _rl_out_call(o_c, lat=False), _rl_out_call(o_l, lat=True), bonus, g,
                        rw_ln_w[l][None], rw_ln_b[l][None], bd, w_ssd_o_b, w_gla_o_b, w_rw_o_b,
                        w_out_b, l)

        x = _ffn_call(x, mod, ng[2:3], ffn_gate_b, ffn_up_b, ffn_down_b, (l, 1), mod_row=6,
                      final_g=final_norm[None] if l == DEPTH - 1 else None)

    y_prompt = x[0].reshape(BATCH, SEQ, D_MODEL)
    y_sample = x[1].reshape(DEC_BATCH, DEC_SEQ, D_MODEL)
    return (y_prompt, y_sample, jnp.stack(new_ssd, axis=1), jnp.stack(new_gla, axis=1),
            jnp.stack(new_rw, axis=1))
```

```python
import functools
import math

import numpy as np
import jax
import jax.numpy as jnp
from jax import lax
from jax.experimental import pallas as pl
from jax.experimental.pallas import tpu as pltpu

F32 = jnp.float32
BF16 = jnp.bfloat16

D_MODEL = 1024
BATCH = 16
SEQ = 256
DEPTH = 2
DEC_BATCH = 4
DEC_SEQ = 1024
GRID_W = 64
D_MIX = 512
D_FF = 2816
N_MOD = 9
SSD_P = 64
SSD_H = 8
SSD_N = 64
SSD_G = 2
SSD_XBC = 768
GLA_H = 4
GLA_DK = 64
GLA_DV = 128
GLA_LR = 16
GLA_GATE_NORM = 16.0
GLA_CHUNK = 64
RW_N = 64
RW_H = 8
RW_LW = 64
RW_LA = 64
RW_LG = 128
RMS_EPS = 1e-6
RW_GN_EPS = 64e-5

LANE = 128
SUB = 8
RB = 256
N_CTX_TOK = BATCH * SEQ
N_TOK = N_CTX_TOK + DEC_BATCH * DEC_SEQ
N_BLK = N_TOK // RB
N_CTX_BLK = N_CTX_TOK // RB
LAT_BLK_PER_SEQ = DEC_SEQ // RB
SSD_W = D_MIX + SSD_XBC + LANE
GLA_QK = GLA_H * GLA_DK
GLA_W = 2 * GLA_QK + 2 * D_MIX + LANE
RW_W = 3 * D_MIX + 3 * LANE
SCAN_TB = 32
FFN_SUB = 1
VMEM_LIMIT = 56 * 1024 * 1024


def _dot(a, b):
    return jnp.dot(a.astype(BF16), b.astype(BF16), preferred_element_type=F32)


def _dot_nt(a, b):
    return lax.dot_general(a.astype(BF16), b.astype(BF16), (((1,), (1,)), ((), ())),
                           preferred_element_type=F32)


def _split3(x):
    hi = x.astype(BF16)
    r1 = x - hi.astype(F32)
    mid = r1.astype(BF16)
    lo = (r1 - mid.astype(F32)).astype(BF16)
    return hi, mid, lo


def _sel_dot(m01, x):
    hi, mid, lo = _split3(x)
    f = lambda p: jnp.dot(m01, p, preferred_element_type=F32)
    return f(hi) + f(mid) + f(lo)


def _dot_sel(x, m01, terms=3):
    f = lambda p: jnp.dot(p, m01, preferred_element_type=F32)
    return functools.reduce(lambda a, b: a + b, [f(p) for p in _split3(x)[:terms]])


def _sigmoid(x):
    return 0.5 * jnp.tanh(0.5 * x) + 0.5


def _silu(x):
    return x * _sigmoid(x)


def _softplus(x):
    return jnp.maximum(x, 0.0) + jnp.log(1.0 + jnp.exp(-jnp.abs(x)))


def _rmsnorm(x, g):
    return x * lax.rsqrt(jnp.mean(x * x, axis=-1, keepdims=True) + RMS_EPS) * g


def _norm_mod(x, g, shift, scale):
    return _rmsnorm(x, g) * (1.0 + scale) + shift


def _iota(shape, dim):
    return lax.broadcasted_iota(jnp.int32, shape, dim)


def _tri01(n, lower, chunk=None):
    t = _iota((n, n), 0)
    s = _iota((n, n), 1)
    m = (s <= t) if lower else (s >= t)
    if chunk is not None:
        m = m & ((t // chunk) == (s // chunk))
    return m


def _ada_kernel(c_ref, w0_ref, w1_ref, b_ref, o_ref):
    c = _silu(c_ref[...])
    half = D_MODEL // 2
    o_ref[0] = _dot(c[:, :half], w0_ref[0]) + _dot(c[:, half:], w1_ref[0]) + b_ref[0]


def _ada_call(cond8, w_ada, b_ada):
    tn = 2304
    nj = (N_MOD * D_MODEL) // tn
    return pl.pallas_call(
        _ada_kernel,
        grid=(DEPTH, nj),
        in_specs=[pl.BlockSpec((SUB, D_MODEL), lambda l, j: (0, 0)),
                  pl.BlockSpec((1, D_MODEL // 2, tn), lambda l, j: (l, 0, j)),
                  pl.BlockSpec((1, D_MODEL // 2, tn), lambda l, j: (l, 1, j)),
                  pl.BlockSpec((1, 1, tn), lambda l, j: (l, 0, j))],
        out_specs=pl.BlockSpec((1, SUB, tn), lambda l, j: (l, 0, j)),
        out_shape=jax.ShapeDtypeStruct((DEPTH, SUB, N_MOD * D_MODEL), F32),
        compiler_params=pltpu.CompilerParams(dimension_semantics=("arbitrary", "arbitrary"),
                                             vmem_limit_bytes=VMEM_LIMIT),
        name="ada",
    )(cond8, w_ada, w_ada, b_ada.reshape(DEPTH, 1, N_MOD * D_MODEL))


def _ffn_kernel(*refs, mod_row, first, last):
    it = iter(refs)
    is_ctx = pl.program_id(0) < N_CTX_BLK // FFN_SUB
    if first:
        xc_ref, xl_ref, pos_ref = next(it), next(it), next(it)
        x = jnp.where(is_ctx, xc_ref[...], xl_ref[...] + pos_ref[...])
    else:
        x = next(it)[...]
    mod_ref, ng_ref, wg_ref, wu_ref, wd_ref = next(it), next(it), next(it), next(it), next(it)
    fin_ref = next(it) if last else None
    shift = mod_ref[0, mod_row:mod_row + 1, :]
    scale = mod_ref[0, mod_row + 1:mod_row + 2, :]
    gate = mod_ref[0, mod_row + 2:mod_row + 3, :]
    h = _norm_mod(x, ng_ref[...], shift, scale).astype(BF16)
    a = _silu(jnp.dot(h, wg_ref[0, 0], preferred_element_type=F32))
    a = (a * jnp.dot(h, wu_ref[0, 0], preferred_element_type=F32)).astype(BF16)
    y = x + 0.5 * gate * jnp.dot(a, wd_ref[0, 0], preferred_element_type=F32)
    if last:
        y = _rmsnorm(y, fin_ref[...])
        oc_ref, ol_ref = next(it), next(it)

        @pl.when(is_ctx)
        def _():
            oc_ref[...] = y

        @pl.when(jnp.logical_not(is_ctx))
        def _():
            ol_ref[...] = y
    else:
        next(it)[...] = y


def _const_spec(shape):
    nd = len(shape)
    return pl.BlockSpec(shape, lambda i: (0,) * nd, pipeline_mode=pl.Buffered(1))


def _stacked_spec(shape, lead):
    nd = len(shape)
    return pl.BlockSpec((1,) * len(lead) + shape, lambda i: tuple(lead) + (0,) * nd,
                        pipeline_mode=pl.Buffered(1))


def _ctx_idx(i):
    return jnp.minimum(i, N_CTX_BLK - 1)


def _lat_idx(i):
    return jnp.maximum(i - N_CTX_BLK, 0)


def _ffn_call(x, mod, ng, wg, wu, wd, lj, *, mod_row, pos=None, final_g=None):
    first = pos is not None
    last = final_g is not None
    fb = FFN_SUB * RB
    nctx = N_CTX_BLK // FFN_SUB
    row_spec = pl.BlockSpec((fb, D_MODEL), lambda i: (i, 0))
    ctx_spec = pl.BlockSpec((fb, D_MODEL), lambda i: (jnp.minimum(i, nctx - 1), 0))
    lat_spec = pl.BlockSpec((fb, D_MODEL), lambda i: (jnp.maximum(i - nctx, 0), 0))
    if first:
        in_specs = [ctx_spec, lat_spec,
                    pl.BlockSpec((fb, D_MODEL),
                                 lambda i: (jnp.maximum(i - nctx, 0) % (DEC_SEQ // fb), 0))]
        args = [x[0], x[1], pos]
    else:
        in_specs = [row_spec]
        args = [x]
    in_specs += [pl.BlockSpec((1, N_MOD, D_MODEL), lambda i: (FFN_SUB * i, 0, 0)),
                 _const_spec((1, D_MODEL)), _stacked_spec((D_MODEL, D_FF), lj),
                 _stacked_spec((D_MODEL, D_FF), lj), _stacked_spec((D_FF, D_MODEL), lj)]
    args += [mod, ng, wg, wu, wd]
    if last:
        in_specs.append(_const_spec((1, D_MODEL)))
        args.append(final_g)
        half = jax.ShapeDtypeStruct((N_TOK // 2, D_MODEL), F32)
        out_specs, out_shape = [ctx_spec, lat_spec], [half, half]
    else:
        out_specs, out_shape = row_spec, jax.ShapeDtypeStruct((N_TOK, D_MODEL), F32)
    return pl.pallas_call(
        functools.partial(_ffn_kernel, mod_row=mod_row, first=first, last=last),
        grid=(N_BLK // FFN_SUB,),
        in_specs=in_specs,
        out_specs=out_specs,
        out_shape=out_shape,
        compiler_params=pltpu.CompilerParams(dimension_semantics=("arbitrary",),
                                             vmem_limit_bytes=VMEM_LIMIT),
        name="ffn",
    )(*args)


def _ssd_kernel(*refs, L, has_init, want_final):
    it = iter(refs)
    x_ref, mod_ref, ng_ref, w_ref = next(it), next(it), next(it), next(it)
    cw_ref, cb_ref, dtb_ref, alog_ref, d_ref, nw_ref = (next(it), next(it), next(it), next(it),
                                                        next(it), next(it))
    ef_ref, eb_ref = next(it), next(it)
    s0_ref = next(it) if has_init else None
    y_ref = next(it)
    sfin_ref = next(it) if want_final else None
    yacc = next(it)

    C = RB
    nc = L // C
    x = x_ref[...]
    u = _norm_mod(x, ng_ref[...], mod_ref[0, 3:4, :], mod_ref[0, 4:5, :]).astype(BF16)
    p = jnp.dot(u, w_ref[...], preferred_element_type=F32)
    z = p[:, :D_MIX]
    xbc = p[:, D_MIX:D_MIX + SSD_XBC]
    dtp = p[:, D_MIX + SSD_XBC:]
    rows = _iota((L, 1), 0)
    prev = jnp.where(rows == 0, 0.0, pltpu.roll(xbc, 1, 0))
    nxt = jnp.where(rows == L - 1, 0.0, pltpu.roll(xbc, L - 1, 0))
    xc = cb_ref[...] + prev * cw_ref[0:1, :] + xbc * cw_ref[1:2, :] + nxt * cw_ref[2:3, :]
    xc = _silu(xc)
    xs = xc[:, :D_MIX]
    bm = xc[:, D_MIX:D_MIX + SSD_G * SSD_N]
    cm = xc[:, D_MIX + SSD_G * SSD_N:]
    dt = _softplus(dtp + dtb_ref[...])
    adt = dt * (-jnp.exp(alog_ref[...]))

    tril = _tri01(C, True)
    triu = _tri01(C, False)
    tril_b = jnp.where(tril, 1.0, 0.0).astype(BF16)
    triu_b = jnp.where(triu, 1.0, 0.0).astype(BF16)
    ef = ef_ref[...]
    eb = eb_ref[...]

    cs, csT, rcs, rcsT, dtT, loc_f, loc_b = [], [], [], [], [], [], []
    for c in range(nc):
        a_c = adt[c * C:(c + 1) * C]
        a_cT = a_c.T
        cs.append(_sel_dot(tril_b, a_c))
        rcs.append(_sel_dot(triu_b, a_c))
        csT.append(_dot_sel(a_cT, triu_b))
        rcsT.append(_dot_sel(a_cT, tril_b))
        dtT.append(dt[c * C:(c + 1) * C].T)

    need_states = want_final or nc > 1
    if need_states:
        for c in range(nc):
            xs_c = xs[c * C:(c + 1) * C]
            dt_c = dt[c * C:(c + 1) * C]
            wf = jnp.exp(cs[c][C - 1:C, :] - cs[c]) * dt_c
            wb = jnp.exp(rcs[c][0:1, :] - rcs[c]) * dt_c
            xwf = (xs_c * _dot_sel(wf, ef, terms=2)).T
            xwb = (xs_c * _dot_sel(wb, eb, terms=2)).T
            lf, lb = [], []
            for h in range(SSD_H):
                g = h // (SSD_H // SSD_G)
                bm_g = bm[c * C:(c + 1) * C, g * SSD_N:(g + 1) * SSD_N]
                lf.append(_dot(xwf[h * SSD_P:(h + 1) * SSD_P, :], bm_g))
                lb.append(_dot(xwb[h * SSD_P:(h + 1) * SSD_P, :], bm_g))
            loc_f.append(lf)
            loc_b.append(lb)

    zero_s = jnp.zeros((SSD_P, SSD_N), F32)
    sin_f = [[None] * SSD_H for _ in range(nc + 1)]
    sin_b = [[None] * SSD_H for _ in range(nc + 1)]
    for h in range(SSD_H):
        sin_f[0][h] = s0_ref[0, 0, h] if has_init else zero_s
        sin_b[nc][h] = s0_ref[0, 1, h] if has_init else zero_s
    if need_states:
        for c in range(nc):
            dec = jnp.exp(cs[c][C - 1:C, :])
            for h in range(SSD_H):
                sin_f[c + 1][h] = dec[:, h:h + 1] * sin_f[c][h] + loc_f[c][h]
        for c in range(nc - 1, -1, -1):
            dec = jnp.exp(rcs[c][0:1, :])
            for h in range(SSD_H):
                sin_b[c][h] = dec[:, SSD_H + h:SSD_H + h + 1] * sin_b[c + 1][h] + loc_b[c][h]

    for c in range(nc):
        sl = slice(c * C, (c + 1) * C)
        ecs = jnp.exp(cs[c])
        ercs = jnp.exp(rcs[c])
        for h in range(SSD_H):
            g = h // (SSD_H // SSD_G)
            cm_g = cm[sl, g * SSD_N:(g + 1) * SSD_N]
            bm_g = bm[sl, g * SSD_N:(g + 1) * SSD_N]
            cb = _dot_nt(cm_g, bm_g)
            lf = jnp.exp(jnp.where(tril, cs[c][:, h:h + 1] - csT[c][h:h + 1, :], -jnp.inf))
            lb = jnp.exp(jnp.where(triu, rcs[c][:, SSD_H + h:SSD_H + h + 1]
                                   - rcsT[c][SSD_H + h:SSD_H + h + 1, :], -jnp.inf))
            m = cb * (lf * dtT[c][h:h + 1, :] + lb * dtT[c][SSD_H + h:SSD_H + h + 1, :])
            y_h = _dot(m, xs[sl, h * SSD_P:(h + 1) * SSD_P])
            if has_init or nc > 1:
                y_h = y_h + ecs[:, h:h + 1] * _dot_nt(cm_g, sin_f[c][h])
                y_h = y_h + ercs[:, SSD_H + h:SSD_H + h + 1] * _dot_nt(cm_g, sin_b[c + 1][h])
            yacc[sl, h * SSD_P:(h + 1) * SSD_P] = y_h

    y = yacc[...] + xs * (d_ref[0:1, :] + d_ref[1:2, :])
    y = _rmsnorm(y * _silu(z), nw_ref[...])
    y_ref[...] = y.astype(BF16)
    if want_final:
        for h in range(SSD_H):
            sfin_ref[0, 0, h] = sin_f[nc][h]
            sfin_ref[0, 1, h] = sin_b[0][h]


def _ssd_call(x, mod, ng, w, cw, cb, dtb, alog, dexp, nw, ef, eb, s0, *, ctx):
    L = SEQ if ctx else DEC_SEQ
    nseq = BATCH if ctx else DEC_BATCH
    blk0 = 0 if ctx else N_CTX_TOK // L
    mod_of = (lambda i: (i, 0, 0)) if ctx else (lambda i: (N_CTX_BLK + LAT_BLK_PER_SEQ * i, 0, 0))
    in_specs = [pl.BlockSpec((L, D_MODEL), lambda i: (blk0 + i, 0)),
                pl.BlockSpec((1, N_MOD, D_MODEL), mod_of),
                _const_spec((1, D_MODEL)), _const_spec((D_MODEL, SSD_W)),
                _const_spec((3, SSD_XBC)), _const_spec((1, SSD_XBC)),
                _const_spec((1, LANE)), _const_spec((1, LANE)),
                _const_spec((2, D_MIX)), _const_spec((1, D_MIX)),
                _const_spec((LANE, D_MIX)), _const_spec((LANE, D_MIX))]
    args = [x, mod, ng, w, cw, cb, dtb, alog, dexp, nw, ef, eb]
    st_spec = pl.BlockSpec((1, 2, SSD_H, SSD_P, SSD_N), lambda i: (i, 0, 0, 0, 0))
    out_specs = [pl.BlockSpec((L, D_MIX), lambda i: (i, 0))]
    out_shape = [jax.ShapeDtypeStruct((nseq * L, D_MIX), BF16)]
    if ctx:
        out_specs.append(st_spec)
        out_shape.append(jax.ShapeDtypeStruct((nseq, 2, SSD_H, SSD_P, SSD_N), F32))
    else:
        in_specs.append(st_spec)
        args.append(s0)
    return pl.pallas_call(
        functools.partial(_ssd_kernel, L=L, has_init=not ctx, want_final=ctx),
        grid=(nseq,),
        in_specs=in_specs,
        out_specs=out_specs,
        out_shape=out_shape,
        scratch_shapes=[pltpu.VMEM((L, D_MIX), F32)],
        compiler_params=pltpu.CompilerParams(dimension_semantics=("arbitrary",),
                                             vmem_limit_bytes=VMEM_LIMIT),
        name="ssd_ctx" if ctx else "ssd_lat",
    )(*args)


def _gla_kernel(*refs, L, has_init, want_final):
    it = iter(refs)
    x_ref, mod_ref, ng_ref, w_ref = next(it), next(it), next(it), next(it)
    gkw_ref, gkb_ref, nw_ref = next(it), next(it), next(it)
    s0_ref = next(it) if has_init else None
    y_ref = next(it)
    sfin_ref = next(it) if want_final else None
    oacc = next(it)

    C = GLA_CHUNK
    B = RB
    cpb = B // C
    nb = L // B
    x = x_ref[...]
    u = _norm_mod(x, ng_ref[...], mod_ref[0, 3:4, :], mod_ref[0, 4:5, :]).astype(BF16)
    p = jnp.dot(u, w_ref[...], preferred_element_type=F32)
    q = p[:, :GLA_QK] * (GLA_DK ** -0.5)
    k = p[:, GLA_QK:2 * GLA_QK]
    v = p[:, 2 * GLA_QK:2 * GLA_QK + D_MIX]
    gg = p[:, 2 * GLA_QK + D_MIX:2 * GLA_QK + 2 * D_MIX]
    lr = p[:, 2 * GLA_QK + 2 * D_MIX:].astype(BF16)
    lg = []
    for d in range(2):
        pre = jnp.dot(lr, gkw_ref[d], preferred_element_type=F32) + gkb_ref[d:d + 1, :]
        lg.append(-_softplus(-pre) / GLA_GATE_NORM)

    low = _tri01(B, True, C)
    upp = _tri01(B, False, C)
    low_b = jnp.where(low, 1.0, 0.0).astype(BF16)
    upp_b = jnp.where(upp, 1.0, 0.0).astype(BF16)
    rowi = _iota((B, 1), 0)

    if has_init:
        s_f = [s0_ref[0, 0, h] for h in range(GLA_H)]
        s_b = [s0_ref[0, 1, h] for h in range(GLA_H)]
    else:
        s_f = [jnp.zeros((GLA_DK, GLA_DV), F32)] * GLA_H
        s_b = list(s_f)

    hpp = LANE // GLA_DK
    lane_head = _iota((1, LANE), 1) // GLA_DK

    def pair(h):
        return slice((h // hpp) * LANE, (h // hpp + 1) * LANE)

    def only(h, a):
        return jnp.where(lane_head == h % hpp, a[:, pair(h)], 0.0)

    def pair_state(s, h):
        j = h // hpp
        return jnp.concatenate(s[j * hpp:(j + 1) * hpp], axis=0)

    blocks = []
    for b in range(nb):
        sl = slice(b * B, (b + 1) * B)
        bf = _sel_dot(low_b, lg[0][sl])
        rb = _sel_dot(upp_b, lg[1][sl])
        q_b, k_b = q[sl], k[sl]
        qf, qb = q_b * jnp.exp(bf), q_b * jnp.exp(rb)
        blocks.append(dict(sl=sl, bf=bf, rb=rb, bfT=bf.T, rbT=rb.T, k=k_b,
                           kf=k_b * jnp.exp(-bf), kb=k_b * jnp.exp(-rb),
                           qf=[only(h, qf) for h in range(GLA_H)],
                           qb=[only(h, qb) for h in range(GLA_H)]))

    for blk in blocks:
        sl = blk["sl"]
        for h in range(GLA_H):
            a = jnp.where(low, _dot_nt(blk["qf"][h], blk["kf"][:, pair(h)]), 0.0)
            a = a + jnp.where(upp, _dot_nt(blk["qb"][h], blk["kb"][:, pair(h)]), 0.0)
            oacc[sl, h * GLA_DV:(h + 1) * GLA_DV] = _dot(a, v[sl, h * GLA_DV:(h + 1) * GLA_DV])

    def inter(blk, c, qd, cum, cumT, edge, s):
        sl = blk["sl"]
        rs = slice(sl.start + c * C, sl.start + (c + 1) * C)
        kdecT = (blk["k"] * jnp.exp(cum[edge:edge + 1, :] - cum)).T
        in_c = (rowi >= c * C) & (rowi < (c + 1) * C)
        s_in = [pair_state(s, h) for h in range(0, GLA_H, hpp)]
        for h in range(GLA_H):
            ks = slice(h * GLA_DK, (h + 1) * GLA_DK)
            vs = slice(h * GLA_DV, (h + 1) * GLA_DV)
            oacc[rs, vs] = oacc[rs, vs] + _dot(qd[h][c * C:(c + 1) * C], s_in[h // hpp])
            kv = _dot(kdecT[ks, :], jnp.where(in_c, v[sl, vs], 0.0))
            s[h] = jnp.exp(cumT[ks, edge:edge + 1]) * s[h] + kv

    for blk in blocks:
        for c in range(cpb):
            inter(blk, c, blk["qf"], blk["bf"], blk["bfT"], (c + 1) * C - 1, s_f)
    for blk in reversed(blocks):
        for c in range(cpb - 1, -1, -1):
            inter(blk, c, blk["qb"], blk["rb"], blk["rbT"], c * C, s_b)

    for h in range(GLA_H):
        vs = slice(h * GLA_DV, (h + 1) * GLA_DV)
        o_h = _rmsnorm(oacc[:, vs], nw_ref[...])
        y_ref[:, vs] = (o_h * _silu(gg[:, vs])).astype(BF16)
        if want_final:
            sfin_ref[0, 0, h] = s_f[h]
            sfin_ref[0, 1, h] = s_b[h]


def _gla_call(x, mod, ng, w, gkw, gkb, nw, s0, *, ctx):
    L = SEQ if ctx else DEC_SEQ
    nseq = BATCH if ctx else DEC_BATCH
    blk0 = 0 if ctx else N_CTX_TOK // L
    mod_of = (lambda i: (i, 0, 0)) if ctx else (lambda i: (N_CTX_BLK + LAT_BLK_PER_SEQ * i, 0, 0))
    in_specs = [pl.BlockSpec((L, D_MODEL), lambda i: (blk0 + i, 0)),
                pl.BlockSpec((1, N_MOD, D_MODEL), mod_of),
                _const_spec((1, D_MODEL)), _const_spec((D_MODEL, GLA_W)),
                _const_spec((2, LANE, GLA_QK)), _const_spec((2, GLA_QK)),
                _const_spec((1, GLA_DV))]
    args = [x, mod, ng, w, gkw, gkb, nw]
    st_spec = pl.BlockSpec((1, 2, GLA_H, GLA_DK, GLA_DV), lambda i: (i, 0, 0, 0, 0))
    out_specs = [pl.BlockSpec((L, D_MIX), lambda i: (i, 0))]
    out_shape = [jax.ShapeDtypeStruct((nseq * L, D_MIX), BF16)]
    if ctx:
        out_specs.append(st_spec)
        out_shape.append(jax.ShapeDtypeStruct((nseq, 2, GLA_H, GLA_DK, GLA_DV), F32))
    else:
        in_specs.append(st_spec)
        args.append(s0)
    return pl.pallas_call(
        functools.partial(_gla_kernel, L=L, has_init=not ctx, want_final=ctx),
        grid=(nseq,),
        in_specs=in_specs,
        out_specs=out_specs,
        out_shape=out_shape,
        scratch_shapes=[pltpu.VMEM((L, D_MIX), F32)],
        compiler_params=pltpu.CompilerParams(dimension_semantics=("arbitrary",),
                                             vmem_limit_bytes=VMEM_LIMIT),
        name="gla_ctx" if ctx else "gla_lat",
    )(*args)


def _rw_prep_kernel(x_ref, xp_ref, xn_ref, mod_ref, ng_ref, w_ref, mu_ref, a0_ref, a2_ref, g2_ref,
                    kkw_ref, ka_ref, rk_ref, w0_ref, w2_ref, bd_ref,
                    r_ref, wf_ref, wb_ref, k_ref, v_ref, kk_ref, nkka_ref, g_ref, bonus_ref):
    i = pl.program_id(0)
    j = (i - N_CTX_BLK) % LAT_BLK_PER_SEQ
    is_first = (i < N_CTX_BLK) | (j == 0)
    is_last = (i < N_CTX_BLK) | (j == LAT_BLK_PER_SEQ - 1)
    x_all = jnp.concatenate([x_ref[...], xp_ref[...], xn_ref[...]], axis=0)
    u_all = _norm_mod(x_all, ng_ref[...], mod_ref[0, 3:4, :], mod_ref[0, 4:5, :]).astype(BF16)
    p_all = jnp.dot(u_all, w_ref[...], preferred_element_type=F32)
    p = p_all[:RB]
    p_prev = jnp.where(is_first, 0.0, p_all[RB + SUB - 1:RB + SUB, :])
    p_next = jnp.where(is_last, 0.0, p_all[RB + SUB:RB + SUB + 1, :])
    rows = _iota((RB, 1), 0)
    prev = jnp.where(rows == 0, p_prev, pltpu.roll(p, 1, 0))
    nxt = jnp.where(rows == RB - 1, p_next, pltpu.roll(p, RB - 1, 0))
    p = p + (0.5 * (prev + nxt) - p) * mu_ref[...]

    r = p[:, :D_MIX]
    k = p[:, D_MIX:2 * D_MIX]
    v = p[:, 2 * D_MIX:3 * D_MIX]
    wlr = p[:, 3 * D_MIX:3 * D_MIX + LANE]
    glr = p[:, 3 * D_MIX + LANE:3 * D_MIX + 2 * LANE]
    alr = p[:, 3 * D_MIX + 2 * LANE:]
    bd = bd_ref[...]
    a = _sigmoid(a0_ref[...] + _dot(alr, a2_ref[...]))
    g = _dot(_sigmoid(glr), g2_ref[...])
    kk = k * kkw_ref[...]
    kk = kk / jnp.maximum(jnp.sqrt(_dot_sel(kk * kk, bd)), 1e-12)
    k = k * (1.0 + (a - 1.0) * ka_ref[...])
    tw = jnp.tanh(wlr).astype(BF16)
    for d, o_ref in ((0, wf_ref), (1, wb_ref)):
        pre = w0_ref[d:d + 1, :] + jnp.dot(tw, w2_ref[d], preferred_element_type=F32)
        o_ref[0] = jnp.exp(-math.exp(-0.5) * _sigmoid(pre)).T
    r_ref[0] = r.T
    k_ref[0] = k.T
    v_ref[0] = v.T
    kk_ref[0] = kk.T
    nkka_ref[0] = (-(kk * a)).T
    g_ref[...] = g
    bonus_ref[...] = _dot_sel(r * k * rk_ref[...], bd, terms=2) * v


def _rw_prep_call(x, mod, ng, w, mu, a0, a2, g2, kkw, ka, rk, w0, w2, bd):
    hb = RB // SUB
    row_spec = pl.BlockSpec((RB, D_MODEL), lambda i: (i, 0))
    out_spec = pl.BlockSpec((RB, D_MIX), lambda i: (i, 0))
    in_specs = [row_spec,
                pl.BlockSpec((SUB, D_MODEL), lambda i: (jnp.maximum(i * hb - 1, 0), 0)),
                pl.BlockSpec((SUB, D_MODEL), lambda i: (jnp.minimum((i + 1) * hb, N_TOK // SUB - 1), 0)),
                pl.BlockSpec((1, N_MOD, D_MODEL), lambda i: (i, 0, 0)),
                _const_spec((1, D_MODEL)), _const_spec((D_MODEL, RW_W)), _const_spec((1, RW_W)),
                _const_spec((1, D_MIX)), _const_spec((LANE, D_MIX)), _const_spec((LANE, D_MIX)),
                _const_spec((1, D_MIX)), _const_spec((1, D_MIX)), _const_spec((1, D_MIX)),
                _const_spec((2, D_MIX)), _const_spec((2, LANE, D_MIX)), _const_spec((D_MIX, D_MIX))]
    t_spec = pl.BlockSpec((1, D_MIX, RB), lambda i: (i, 0, 0))
    t_shape = jax.ShapeDtypeStruct((N_BLK, D_MIX, RB), F32)
    tok_shape = jax.ShapeDtypeStruct((N_TOK, D_MIX), F32)
    return pl.pallas_call(
        _rw_prep_kernel,
        grid=(N_BLK,),
        in_specs=in_specs,
        out_specs=[t_spec] * 7 + [out_spec] * 2,
        out_shape=[t_shape] * 7 + [tok_shape] * 2,
        compiler_params=pltpu.CompilerParams(dimension_semantics=("arbitrary",),
                                             vmem_limit_bytes=VMEM_LIMIT),
        name="rw_prep",
    )(x, x, x, mod, ng, w, mu, a0, a2, g2, kkw, ka, rk, w0, w2, bd)


RW_VH = LANE // (2 * DEC_BATCH * RW_H)
RW_NV_LAT = RW_N // RW_VH
LAT_ROWS = DEC_BATCH * RW_H
LAT_VIEW = (N_BLK // LAT_BLK_PER_SEQ, LAT_BLK_PER_SEQ, RW_H, RW_N, RB)


def _rl_rows(ref, nl, lat):
    x = ref[:, 0, :, nl, :] if lat else ref[:, :, nl, :]
    return x.reshape(-1, RB)


def _time_flip(x):
    ex = _iota((RB, RB), 0) + _iota((RB, RB), 1) == RB - 1
    return _dot_sel(x, jnp.where(ex, 1.0, 0.0).astype(BF16))


def _lat_spec(n_of, back):
    def index(j, n):
        return (1, LAT_BLK_PER_SEQ - 1 - j if back else j, 0, n_of(n), 0)
    return pl.BlockSpec((DEC_BATCH, 1, RW_H, SUB, RB), index)


def _rl_k_ctx_kernel(r_ref, k_ref, kk_ref, a_ref, wf_ref, wb_ref, ro, ko, kko, ao, wo):
    for nl in range(SUB):
        ro[nl] = _rl_rows(r_ref, nl, False).T
        ko[nl] = _rl_rows(k_ref, nl, False).T
        kko[nl] = _rl_rows(kk_ref, nl, False).T
        ao[nl] = _rl_rows(a_ref, nl, False).T
        wo[0, nl] = _rl_rows(wf_ref, nl, False).T
        wo[1, nl] = _rl_rows(wb_ref, nl, False).T


def _rl_k_lat_kernel(rf, rb, kf, kb, kkf, kkb, af, ab, wff, wbb, ro, ko, kko, ao, wo):
    sets = ((rf, rb, ro), (kf, kb, ko), (kkf, kkb, kko), (af, ab, ao), (wff, wbb, wo))
    back = _time_flip(jnp.concatenate(
        [_rl_rows(b, nl, True) for _, b, _ in sets for nl in range(SUB)], axis=0))
    i = 0
    for f, _, o in sets:
        for nl in range(SUB):
            pair = [_rl_rows(f, nl, True), back[i * LAT_ROWS:(i + 1) * LAT_ROWS]]
            o[nl] = jnp.concatenate(pair * RW_VH, axis=0).T
            i += 1


def _rl_k_call(r, k, kk, nkka, wf, wb, *, lat):
    nb = RW_N // SUB
    if lat:
        L, G = DEC_SEQ, 1
        fwd = _lat_spec(lambda n: n, False)
        bwd = _lat_spec(lambda n: n, True)
        ospec = pl.BlockSpec((SUB, RB, LANE), lambda j, n: (n, j, 0))
        args = [a.reshape(LAT_VIEW) for a in (r, r, k, k, kk, kk, nkka, nkka, wf, wb)]
        call = dict(grid=(LAT_BLK_PER_SEQ, nb), in_specs=[fwd, bwd] * 5, out_specs=[ospec] * 5)
        body, name = _rl_k_lat_kernel, "rl_k_lat"
        wshape = jax.ShapeDtypeStruct((RW_N, L, LANE), F32)
    else:
        L, G = SEQ, 2
        ispec = pl.BlockSpec((BATCH, RW_H, SUB, RB), lambda n: (0, 0, n, 0))
        ospec = pl.BlockSpec((SUB, RB, LANE), lambda n: (n, 0, 0))
        wspec = pl.BlockSpec((2, SUB, RB, LANE), lambda n: (0, n, 0, 0))
        args = [a.reshape(N_BLK, RW_H, RW_N, RB) for a in (r, k, kk, nkka, wf, wb)]
        call = dict(grid=(nb,), in_specs=[ispec] * 6, out_specs=[ospec] * 4 + [wspec])
        body, name = _rl_k_ctx_kernel, "rl_k_ctx"
        wshape = jax.ShapeDtypeStruct((2, RW_N, L, LANE), F32)
    kshape = jax.ShapeDtypeStruct((RW_N, L, LANE), F32)
    outs = pl.pallas_call(
        body,
        out_shape=[kshape] * 4 + [wshape],
        compiler_params=pltpu.CompilerParams(
            dimension_semantics=("arbitrary",) * len(call["grid"]), vmem_limit_bytes=VMEM_LIMIT),
        name=name, **call,
    )(*args)
    return list(outs[:4]) + [outs[4].reshape(G, RW_N, L, LANE)]


def _rl_v_ctx_kernel(v_ref, o_ref):
    for nl in range(SUB):
        o_ref[:, nl, :] = _rl_rows(v_ref, nl, False).T


def _rl_v_lat_kernel(f0, f1, b0, b1, o_ref):
    back = _time_flip(jnp.concatenate(
        [_rl_rows(b, nl, True) for b in (b0, b1) for nl in range(SUB)], axis=0))
    for nl in range(SUB):
        parts = []
        for h, f in enumerate((f0, f1)):
            i = h * SUB + nl
            parts += [_rl_rows(f, nl, True), back[i * LAT_ROWS:(i + 1) * LAT_ROWS]]
        o_ref[:, nl, :] = jnp.concatenate(parts, axis=0).T


def _rl_v_call(v, *, lat):
    if lat:
        nvb = RW_NV_LAT // SUB
        half = lambda h: (lambda n: h * nvb + n)
        call = dict(grid=(LAT_BLK_PER_SEQ, nvb),
                    in_specs=[_lat_spec(half(0), False), _lat_spec(half(1), False),
                              _lat_spec(half(0), True), _lat_spec(half(1), True)],
                    out_specs=pl.BlockSpec((RB, SUB, LANE), lambda j, n: (j, n, 0)))
        args = [v.reshape(LAT_VIEW)] * 4
        oshape = jax.ShapeDtypeStruct((DEC_SEQ, RW_NV_LAT, LANE), F32)
        body, name = _rl_v_lat_kernel, "rl_v_lat"
    else:
        call = dict(grid=(RW_N // SUB,),
                    in_specs=[pl.BlockSpec((BATCH, RW_H, SUB, RB), lambda n: (0, 0, n, 0))],
                    out_specs=pl.BlockSpec((RB, SUB, LANE), lambda n: (0, n, 0)))
        args = [v.reshape(N_BLK, RW_H, RW_N, RB)]
        oshape = jax.ShapeDtypeStruct((SEQ, RW_N, LANE), F32)
        body, name = _rl_v_ctx_kernel, "rl_v_ctx"
    return pl.pallas_call(
        body,
        out_shape=oshape,
        compiler_params=pltpu.CompilerParams(
            dimension_semantics=("arbitrary",) * len(call["grid"]), vmem_limit_bytes=VMEM_LIMIT),
        name=name, **call,
    )(*args)


def _rl_out_ctx_kernel(o_ref, ot_ref):
    for nl in range(SUB):
        x = (o_ref[0, :, nl, :] + o_ref[1, :, nl, :]).T
        ot_ref[:, :, nl, :] = x.reshape(BATCH, RW_H, RB)


def _rl_out_lat_kernel(of_ref, ob_ref, ot_ref):
    fwd = [of_ref[0, :, nl, :].T for nl in range(SUB)]
    mir = [ob_ref[0, :, nl, :].T for nl in range(SUB)]
    lo = lambda h, d: (2 * h + d) * LAT_ROWS
    back = _time_flip(jnp.concatenate(
        [mir[nl][lo(h, 1):lo(h, 1) + LAT_ROWS] for nl in range(SUB) for h in range(RW_VH)], axis=0))
    for nl in range(SUB):
        for h in range(RW_VH):
            i = nl * RW_VH + h
            x = fwd[nl][lo(h, 0):lo(h, 0) + LAT_ROWS] + back[i * LAT_ROWS:(i + 1) * LAT_ROWS]
            ot_ref[:, 0, :, h, nl, :] = x.reshape(DEC_BATCH, RW_H, RB)


def _rl_out_call(o, *, lat):
    if lat:
        nvb = RW_NV_LAT // SUB
        call = dict(grid=(LAT_BLK_PER_SEQ, nvb),
                    in_specs=[pl.BlockSpec((1, RB, SUB, LANE), lambda j, n: (0, j, n, 0)),
                              pl.BlockSpec((1, RB, SUB, LANE),
                                           lambda j, n: (0, LAT_BLK_PER_SEQ - 1 - j, n, 0))],
                    out_specs=pl.BlockSpec((DEC_BATCH, 1, RW_H, RW_VH, SUB, RB),
                                           lambda j, n: (0, j, 0, 0, n, 0)))
        oshape = (DEC_BATCH, LAT_BLK_PER_SEQ, RW_H, RW_VH, RW_NV_LAT, RB)
        args, body, name = [o, o], _rl_out_lat_kernel, "rl_out_lat"
    else:
        call = dict(grid=(RW_N // SUB,),
                    in_specs=[pl.BlockSpec((2, RB, SUB, LANE), lambda n: (0, 0, n, 0))],
                    out_specs=pl.BlockSpec((BATCH, RW_H, SUB, RB), lambda n: (0, 0, n, 0)))
        oshape = (BATCH, RW_H, RW_N, RB)
        args, body, name = [o], _rl_out_ctx_kernel, "rl_out_ctx"
    out = pl.pallas_call(
        body,
        out_shape=jax.ShapeDtypeStruct(oshape, F32),
        compiler_params=pltpu.CompilerParams(
            dimension_semantics=("arbitrary",) * len(call["grid"]), vmem_limit_bytes=VMEM_LIMIT),
        name=name, **call,
    )(*args)
    return out.reshape(N_CTX_BLK, D_MIX, RB)


def _rw_scan_kernel(*refs, vb, npart, has_init, want_final):
    it = iter(refs)
    r_ref, w_ref, k_ref, v_ref, kk_ref, nkka_ref = (next(it), next(it), next(it), next(it),
                                                    next(it), next(it))
    s0_ref = next(it) if has_init else None
    o_ref = next(it)
    sfin_ref = next(it) if want_final else None
    s_scr = next(it)
    g = pl.program_id(0)
    tb = pl.program_id(1)

    @pl.when(tb == 0)
    def _():
        if has_init:
            s_scr[...] = s0_ref[0]
        else:
            s_scr[...] = jnp.zeros(s_scr.shape, F32)

    def bcast(ref, t, kx):
        return jnp.broadcast_to(ref[kx, pl.ds(t, 1), :], (SUB, LANE))

    def t_of(i):
        return jnp.where(g == 0, i, SCAN_TB - 1 - i)

    def zeros():
        return [[jnp.zeros((SUB, LANE), F32) for _ in range(vb)] for _ in range(npart)]

    def total(parts):
        return [functools.reduce(lambda a, b: a + b, [p[j] for p in parts]) for j in range(vb)]

    t0 = t_of(0)
    acc = zeros()
    for kx in range(RW_N):
        kkb = bcast(kk_ref, t0, kx)
        for j in range(vb):
            acc[kx % npart][j] = acc[kx % npart][j] + s_scr[kx, j * SUB:(j + 1) * SUB, :] * kkb

    def step(i, skk):
        t = t_of(i)
        tn = t_of(jnp.minimum(i + 1, SCAN_TB - 1))
        vv = [v_ref[t, j * SUB:(j + 1) * SUB, :] for j in range(vb)]
        oacc, nacc = zeros(), zeros()
        for kx in range(RW_N):
            wb = jnp.broadcast_to(w_ref[0, kx, pl.ds(t, 1), :], (SUB, LANE))
            ab = bcast(nkka_ref, t, kx)
            kb = bcast(k_ref, t, kx)
            rb = bcast(r_ref, t, kx)
            kkn = bcast(kk_ref, tn, kx)
            p = kx % npart
            for j in range(vb):
                s = s_scr[kx, j * SUB:(j + 1) * SUB, :] * wb + skk[j] * ab + vv[j] * kb
                s_scr[kx, j * SUB:(j + 1) * SUB, :] = s
                oacc[p][j] = oacc[p][j] + s * rb
                nacc[p][j] = nacc[p][j] + s * kkn
        for j, o in enumerate(total(oacc)):
            o_ref[0, t, j * SUB:(j + 1) * SUB, :] = o
        return tuple(total(nacc))

    lax.fori_loop(0, SCAN_TB, step, tuple(total(acc)))

    if want_final:
        @pl.when(tb == pl.num_programs(1) - 1)
        def _():
            sfin_ref[0] = s_scr[...]


def _rw_scan_call(r, w, k, v, kk, nkka, s0, *, want_final):
    G, L = w.shape[0], r.shape[1]
    nv = v.shape[1]
    ntb = L // SCAN_TB
    has_init = s0 is not None
    tmap = lambda g, t: t + g * (ntb - 1 - 2 * t)
    kspec = pl.BlockSpec((RW_N, SCAN_TB, LANE), lambda g, t: (0, tmap(g, t), 0))
    wspec = pl.BlockSpec((1, RW_N, SCAN_TB, LANE), lambda g, t: (g, 0, tmap(g, t), 0))
    vspec = pl.BlockSpec((SCAN_TB, nv, LANE), lambda g, t: (tmap(g, t), 0, 0))
    ospec = pl.BlockSpec((1, SCAN_TB, nv, LANE), lambda g, t: (g, tmap(g, t), 0, 0))
    sspec = pl.BlockSpec((1, RW_N, nv, LANE), lambda g, t: (g, 0, 0, 0))
    in_specs = [kspec, wspec, kspec, vspec, kspec, kspec]
    args = [r, w, k, v, kk, nkka]
    if has_init:
        in_specs.append(sspec)
        args.append(s0)
    out_specs = [ospec]
    out_shape = [jax.ShapeDtypeStruct((G, L, nv, LANE), F32)]
    if want_final:
        out_specs.append(sspec)
        out_shape.append(jax.ShapeDtypeStruct((G, RW_N, nv, LANE), F32))
    return pl.pallas_call(
        functools.partial(_rw_scan_kernel, vb=nv // SUB, npart=1,
                          has_init=has_init, want_final=want_final),
        grid=(G, ntb),
        in_specs=in_specs,
        out_specs=out_specs,
        out_shape=out_shape,
        scratch_shapes=[pltpu.VMEM((RW_N, nv, LANE), F32)],
        compiler_params=pltpu.CompilerParams(dimension_semantics=("arbitrary", "arbitrary"),
                                             vmem_limit_bytes=VMEM_LIMIT),
        name="rw_scan",
    )(*args)


def _merge_kernel(x_ref, mod_ref, ng_ref, wgate_ref, yssd_c, yssd_l, ygla_c, ygla_l, orw_c, orw_l,
                  bonus_ref, g_ref, lnw_ref, lnb_ref, bd_ref, wso_ref, wgo_ref, wro_ref, wout_ref,
                  o_ref):
    is_ctx = pl.program_id(0) < N_CTX_BLK
    x = x_ref[...]
    u = _norm_mod(x, ng_ref[...], mod_ref[0, 3:4, :], mod_ref[0, 4:5, :]).astype(BF16)
    bd = bd_ref[...]
    yssd = jnp.where(is_ctx, yssd_c[...], yssd_l[...])
    ygla = jnp.where(is_ctx, ygla_c[...], ygla_l[...])
    o = jnp.where(is_ctx, orw_c[0], orw_l[0]).T
    mu = _dot_sel(o, bd, terms=2) * (1.0 / RW_N)
    oc = o - mu
    var = _dot_sel(oc * oc, bd, terms=2) * (1.0 / RW_N)
    o = oc * lax.rsqrt(var + RW_GN_EPS) * lnw_ref[...] + lnb_ref[...]
    y_rw = ((o + bonus_ref[...]) * g_ref[...]).astype(BF16)
    merged = jnp.zeros((RB, D_MODEL), F32)
    for b, (y, wo_ref) in enumerate(((yssd, wso_ref), (ygla, wgo_ref), (y_rw, wro_ref))):
        gate = _sigmoid(jnp.dot(u, wgate_ref[:, b * D_MODEL:(b + 1) * D_MODEL],
                                preferred_element_type=F32))
        merged = merged + gate * jnp.dot(y, wo_ref[0], preferred_element_type=F32)
    m = jnp.dot(merged.astype(BF16), wout_ref[0], preferred_element_type=F32)
    o_ref[...] = x + mod_ref[0, 5:6, :] * m


def _merge_call(x, mod, ng, wgate, yssd_c, yssd_l, ygla_c, ygla_l, orw_c, orw_l, bonus, g, lnw, lnb,
                bd, wso, wgo, wro, wout, l):
    row_spec = pl.BlockSpec((RB, D_MODEL), lambda i: (i, 0))
    mix_spec = pl.BlockSpec((RB, D_MIX), lambda i: (i, 0))
    mix_c = pl.BlockSpec((RB, D_MIX), lambda i: (_ctx_idx(i), 0))
    mix_l = pl.BlockSpec((RB, D_MIX), lambda i: (_lat_idx(i), 0))
    t_c = pl.BlockSpec((1, D_MIX, RB), lambda i: (_ctx_idx(i), 0, 0))
    t_l = pl.BlockSpec((1, D_MIX, RB), lambda i: (_lat_idx(i), 0, 0))
    in_specs = [row_spec, pl.BlockSpec((1, N_MOD, D_MODEL), lambda i: (i, 0, 0)),
                _const_spec((1, D_MODEL)), _const_spec((D_MODEL, 3 * D_MODEL)),
                mix_c, mix_l, mix_c, mix_l, t_c, t_l, mix_spec, mix_spec,
                _const_spec((1, D_MIX)), _const_spec((1, D_MIX)), _const_spec((D_MIX, D_MIX)),
                _stacked_spec((D_MIX, D_MODEL), (l,)), _stacked_spec((D_MIX, D_MODEL), (l,)),
                _stacked_spec((D_MIX, D_MODEL), (l,)), _stacked_spec((D_MODEL, D_MODEL), (l,))]
    return pl.pallas_call(
        _merge_kernel,
        grid=(N_BLK,),
        in_specs=in_specs,
        out_specs=row_spec,
        out_shape=jax.ShapeDtypeStruct((N_TOK, D_MODEL), F32),
        compiler_params=pltpu.CompilerParams(dimension_semantics=("arbitrary",),
                                             vmem_limit_bytes=VMEM_LIMIT),
        name="merge",
    )(x, mod, ng, wgate, yssd_c, yssd_l, ygla_c, ygla_l, orw_c, orw_l, bonus, g, lnw, lnb, bd,
      wso, wgo, wro, wout)


def _grid_pos_embed(rows, cols, dim):
    quarter = dim // 4
    omega = 1.0 / (10000.0 ** (jnp.arange(quarter, dtype=F32) / quarter))
    er = jnp.arange(rows, dtype=F32)[:, None] * omega
    ec = jnp.arange(cols, dtype=F32)[:, None] * omega
    er = jnp.concatenate([jnp.sin(er), jnp.cos(er)], axis=-1)
    ec = jnp.concatenate([jnp.sin(ec), jnp.cos(ec)], axis=-1)
    emb = jnp.concatenate([jnp.broadcast_to(er[:, None], (rows, cols, dim // 2)),
                           jnp.broadcast_to(ec[None], (rows, cols, dim // 2))], axis=-1)
    return emb.reshape(rows * cols, dim)


def _pad_cols(a, n):
    return jnp.pad(a, [(0, 0)] * (a.ndim - 1) + [(0, n - a.shape[-1])])


def _rows_at(a, off, n):
    return jnp.pad(a, ((off, n - off - a.shape[0]), (0, 0)))


def _block_diag_ones(n, blk):
    i = np.arange(n)
    return jnp.asarray((i[:, None] // blk) == (i[None, :] // blk), BF16)


def _expand01(row0):
    m = np.zeros((LANE, D_MIX), np.float32)
    for h in range(SSD_H):
        m[row0 + h, h * SSD_P:(h + 1) * SSD_P] = 1.0
    return jnp.asarray(m, BF16)


def kernel(x_prompt, x_sample, state_ssd, state_gla, state_rwkv, c, c_ctx, norm_g, w_ada, b_ada,
           ffn_gate, ffn_up, ffn_down, w_in, ssd_conv_w, ssd_conv_b, ssd_dt_bias, ssd_A_log, ssd_D,
           ssd_norm, w_ssd_o, gla_gk_w, gla_gk_b, gla_norm, w_gla_o, rw_mu, rw_w0, rw_w2, rw_a0,
           rw_a2, rw_g2, rw_kk, rw_ka, rw_rk, rw_ln_w, rw_ln_b, w_rw_o, w_out, final_norm):
    pos = _grid_pos_embed(DEC_SEQ // GRID_W, GRID_W, D_MODEL)
    x = (x_prompt.reshape(N_CTX_TOK, D_MODEL), x_sample.reshape(-1, D_MODEL))
    s0_rw = state_rwkv.reshape(DEC_BATCH, DEPTH, 2, RW_H, RW_VH, RW_NV_LAT, RW_N).transpose(
        1, 6, 5, 4, 2, 0, 3).reshape(DEPTH, 1, RW_N, RW_NV_LAT, LANE)

    cond8 = jnp.concatenate([c_ctx[None], c, jnp.zeros((SUB - 1 - DEC_BATCH, D_MODEL), F32)])
    ada = _ada_call(cond8, w_ada, b_ada)
    cond_of_blk = np.concatenate([np.zeros(N_CTX_BLK, np.int32),
                                  1 + np.arange(N_BLK - N_CTX_BLK, dtype=np.int32) // LAT_BLK_PER_SEQ])

    bd = _block_diag_ones(D_MIX, RW_N)
    ef, eb = _expand01(0), _expand01(SSD_H)
    o_ssd = D_MIX + SSD_XBC + 2 * SSD_H
    o_gla = o_ssd + 2 * GLA_H * GLA_DK + 2 * D_MIX + 2 * GLA_LR
    o_rw = o_gla + 3 * D_MIX + 2 * RW_LW + RW_LA + RW_LG

    ffn_gate_b, ffn_up_b, ffn_down_b = (a.astype(BF16) for a in (ffn_gate, ffn_up, ffn_down))
    w_ssd_o_b, w_gla_o_b, w_rw_o_b, w_out_b = (a.astype(BF16)
                                               for a in (w_ssd_o, w_gla_o, w_rw_o, w_out))

    new_ssd, new_gla, new_rw = [], [], []
    for l in range(DEPTH):
        mod = ada[l][cond_of_blk].reshape(N_BLK, N_MOD, D_MODEL)
        ng = norm_g[l]
        wi = w_in[l]
        w_ssd = _pad_cols(wi[:, :o_ssd], SSD_W).astype(BF16)
        w_gla = _pad_cols(wi[:, o_ssd:o_gla], GLA_W).astype(BF16)
        wr_ = wi[:, o_gla:o_rw]
        w_rw = jnp.concatenate([wr_[:, :1664], wr_[:, 1728:1856], _pad_cols(wr_[:, 1664:1728], LANE)],
                               axis=1).astype(BF16)
        mu_ = rw_mu[l]
        mu = jnp.concatenate([mu_[:1664], mu_[1728:1856], _pad_cols(mu_[1664:1728], LANE)])[None]
        w_gate = wi[:, o_rw:].astype(BF16)

        x = _ffn_call(x, mod, ng[0:1], ffn_gate_b, ffn_up_b, ffn_down_b, (l, 0), mod_row=0,
                      pos=pos if l == 0 else None)

        ssd_args = (x, mod, ng[1:2], w_ssd, ssd_conv_w[l], ssd_conv_b[l][None],
                    _pad_cols(ssd_dt_bias[l].reshape(1, -1), LANE),
                    _pad_cols(ssd_A_log[l].reshape(1, -1), LANE),
                    jnp.repeat(ssd_D[l], SSD_P, axis=1), ssd_norm[l][None], ef, eb)
        y_ssd_c, s_ssd = _ssd_call(*ssd_args, None, ctx=True)
        (y_ssd_l,) = _ssd_call(*ssd_args, state_ssd[:, l], ctx=False)
        new_ssd.append(s_ssd)

        gkw = jnp.stack([_rows_at(gla_gk_w[l, d], d * GLA_LR, LANE) for d in range(2)])
        gla_args = (x, mod, ng[1:2], w_gla, gkw.astype(BF16), gla_gk_b[l],
                    gla_norm[l][None])
        y_gla_c, s_gla = _gla_call(*gla_args, None, ctx=True)
        (y_gla_l,) = _gla_call(*gla_args, state_gla[:, l], ctx=False)
        new_gla.append(s_gla)

        w2p = jnp.stack([_rows_at(rw_w2[l, d], d * RW_LW, LANE) for d in range(2)]).astype(BF16)
        r, wf, wb, k, v, kk, nkka, g, bonus = _rw_prep_call(
            x, mod, ng[1:2], w_rw, mu, rw_a0[l][None], _rows_at(rw_a2[l], 0, LANE).astype(BF16),
            rw_g2[l].astype(BF16), rw_kk[l][None], rw_ka[l][None], rw_rk[l].reshape(1, D_MIX),
            rw_w0[l], w2p, bd)
        rc, kc, kkc, ac, wc = _rl_k_call(r, k, kk, nkka, wf, wb, lat=False)
        o_c, s_rw = _rw_scan_call(rc, wc, kc, _rl_v_call(v, lat=False), kkc, ac, None,
                                  want_final=True)
        new_rw.append(s_rw.reshape(2, RW_N, RW_N, BATCH, RW_H).transpose(3, 0, 4, 2, 1))
        rl, kl, kkl, al, wl = _rl_k_call(r, k, kk, nkka, wf, wb, lat=True)
        (o_l,) = _rw_scan_call(rl, wl, kl, _rl_v_call(v, lat=True), kkl, al, s0_rw[l],
                               want_final=False)

        x = _merge_call(x, mod, ng[1:2], w_gate, y_ssd_c, y_ssd_l, y_gla_c, y_gla_l,
                        _rl_out_call(o_c, lat=False), _rl_out_call(o_l, lat=True), bonus, g,
                        rw_ln_w[l][None], rw_ln_b[l][None], bd, w_ssd_o_b, w_gla_o_b, w_rw_o_b,
                        w_out_b, l)

        x = _ffn_call(x, mod, ng[2:3], ffn_gate_b, ffn_up_b, ffn_down_b, (l, 1), mod_row=6,
                      final_g=final_norm[None] if l == DEPTH - 1 else None)

    y_prompt = x[0].reshape(BATCH, SEQ, D_MODEL)
    y_sample = x[1].reshape(DEC_BATCH, DEC_SEQ, D_MODEL)
    return (y_prompt, y_sample, jnp.stack(new_ssd, axis=1), jnp.stack(new_gla, axis=1),
            jnp.stack(new_rw, axis=1))
```
